```python
import jax, jax.numpy as jnp
from jax import lax
import numpy as np

D_MODEL = 1024
BATCH = 8
SEQ = 8192
DEPTH = 4

MEM_LEN = 256
HEAD_DIM = 64
N_CONV_GROUPS = 6
CONV_CH = N_CONV_GROUPS * HEAD_DIM
CONV_WIDTH = 31
N_Q_HEADS = 6
N_KV_HEADS = 2
SWA_Q = N_Q_HEADS * HEAD_DIM
SWA_KV = N_KV_HEADS * HEAD_DIM
WINDOW = 128
BLOCK = 128
N_MEM_HEADS = 4
MEM_W = N_MEM_HEADS * HEAD_DIM
D_MIX = CONV_CH + SWA_Q + MEM_W
D_IN = 2 * CONV_CH + SWA_Q + 2 * SWA_KV + MEM_W
D_FF = 2816
ROPE_THETA = 10000.0
EPS = 1e-6

kernel_name = "hymba_style_conformer_swa_memory_macaron"


def rms_norm(x, g):
    xf = x.astype(jnp.float32)
    y = xf * lax.rsqrt(jnp.mean(xf * xf, axis=-1, keepdims=True) + EPS)
    return (y * g.astype(jnp.float32)).astype(x.dtype)


def layer_norm(x, g, b):
    xf = x.astype(jnp.float32)
    mu = jnp.mean(xf, axis=-1, keepdims=True)
    var = jnp.mean(jnp.square(xf - mu), axis=-1, keepdims=True)
    y = (xf - mu) * lax.rsqrt(var + EPS)
    return (y * g.astype(jnp.float32) + b.astype(jnp.float32)).astype(x.dtype)


def swiglu(x, w1, w3, w2):
    return (jax.nn.silu(x @ w1) * (x @ w3)) @ w2


def rope_tables(positions):
    inv_freq = ROPE_THETA ** (-jnp.arange(0, HEAD_DIM, 2, dtype=jnp.float32) / HEAD_DIM)
    ang = positions.astype(jnp.float32)[..., None] * inv_freq
    return jnp.cos(ang)[:, :, None, :], jnp.sin(ang)[:, :, None, :]


def apply_rope(x, cos, sin):
    xf = x.astype(jnp.float32)
    x1, x2 = jnp.split(xf, 2, axis=-1)
    out = jnp.concatenate([x1 * cos - x2 * sin, x2 * cos + x1 * sin], axis=-1)
    return out.astype(x.dtype)


def conv_module(u, w_dw, b_dw, ln_g, ln_b):
    a, gate = jnp.split(u, 2, axis=-1)
    y = a * jax.nn.sigmoid(gate)
    y = lax.conv_general_dilated(
        y, w_dw[:, None, :], window_strides=(1,),
        padding=[(CONV_WIDTH - 1, 0)],
        dimension_numbers=('NWC', 'WIO', 'NWC'),
        feature_group_count=CONV_CH) + b_dw
    y = layer_norm(y, ln_g, ln_b)
    return jax.nn.silu(y)


def sliding_window_attention(q, k, v, sinks):
    B, S, _, Dh = q.shape
    nb = S // BLOCK
    g = N_Q_HEADS // N_KV_HEADS
    qb = q.reshape(B, nb, BLOCK, N_KV_HEADS, g, Dh)
    kb = k.reshape(B, nb, BLOCK, N_KV_HEADS, Dh)
    vb = v.reshape(B, nb, BLOCK, N_KV_HEADS, Dh)

    def with_prev(t):
        prev = jnp.concatenate([jnp.zeros_like(t[:, :1]), t[:, :-1]], axis=1)
        return jnp.concatenate([prev, t], axis=2)

    kw, vw = with_prev(kb), with_prev(vb)
    scores = jnp.einsum('bnqhgd,bnkhd->bnhgqk', qb, kw).astype(jnp.float32) * (Dh ** -0.5)
    qi = jnp.arange(BLOCK)[:, None] + BLOCK
    kj = jnp.arange(2 * BLOCK)[None, :]
    rel = qi - kj
    band = (rel >= 0) & (rel < WINDOW)
    first_ok = (jnp.arange(nb)[:, None, None] > 0) | (kj[None] >= BLOCK)
    mask = band[None] & first_ok
    scores = jnp.where(mask[None, :, None, None], scores, -jnp.inf)
    sink = jnp.broadcast_to(
        sinks.astype(jnp.float32).reshape(N_KV_HEADS, g)[None, None, :, :, None, None],
        scores.shape[:-1] + (1,))
    probs = jax.nn.softmax(jnp.concatenate([scores, sink], axis=-1), axis=-1)[..., :-1]
    out = jnp.einsum('bnhgqk,bnkhd->bnqhgd', probs.astype(v.dtype), vw)
    return out.reshape(B, S, N_Q_HEADS * Dh)


def memory_attention(q, mk, mv):
    B, S, _, Dh = q.shape
    scores = jnp.einsum('bshd,bmhd->bhsm', q, mk).astype(jnp.float32) * (Dh ** -0.5)
    probs = jax.nn.softmax(scores, axis=-1)
    out = jnp.einsum('bhsm,bmhd->bshd', probs.astype(mv.dtype), mv)
    return out.reshape(B, S, N_MEM_HEADS * Dh)


def _fwd_setup_inputs(seed: int = 0) -> dict:
    key = jax.random.key(seed)
    ks = iter(jax.random.split(key, 32))
    f32 = jnp.float32

    def w(shape, fan_in):
        return jax.random.normal(next(ks), shape, f32) * (fan_in ** -0.5)

    def gain(shape):
        return 1.0 + 0.02 * jax.random.normal(next(ks), shape, f32)

    def bias(shape):
        return 0.02 * jax.random.normal(next(ks), shape, f32)

    L = DEPTH
    x = jax.random.normal(next(ks), (BATCH, SEQ, D_MODEL), f32)
    mem = jax.random.normal(next(ks), (BATCH, MEM_LEN, D_MODEL), f32)
    positions = jnp.broadcast_to(jnp.arange(SEQ, dtype=jnp.int32)[None, :], (BATCH, SEQ))
    return {
        "x": x,
        "mem": mem,
        "positions": positions,
        "ffn1_norm": gain((L, D_MODEL)),
        "ffn1_w1": w((L, D_MODEL, D_FF), D_MODEL),
        "ffn1_w3": w((L, D_MODEL, D_FF), D_MODEL),
        "ffn1_w2": w((L, D_FF, D_MODEL), D_FF),
        "mix_norm": gain((L, D_MODEL)),
        "w_in": w((L, D_MODEL, D_IN), D_MODEL),
        "conv_w": w((L, CONV_WIDTH, CONV_CH), CONV_WIDTH),
        "conv_b": bias((L, CONV_CH)),
        "conv_ln_g": gain((L, CONV_CH)),
        "conv_ln_b": bias((L, CONV_CH)),
        "swa_q_norm": gain((L, HEAD_DIM)),
        "swa_k_norm": gain((L, HEAD_DIM)),
        "swa_sinks": jax.random.normal(next(ks), (L, N_Q_HEADS), f32),
        "mem_norm": gain((L, D_MODEL)),
        "w_mem_kv": w((L, D_MODEL, 2 * MEM_W), D_MODEL),
        "mem_q_norm": gain((L, HEAD_DIM)),
        "mem_k_norm": gain((L, HEAD_DIM)),
        "w_out": w((L, D_MIX, D_MODEL), D_MIX),
        "ffn2_norm": gain((L, D_MODEL)),
        "ffn2_w1": w((L, D_MODEL, D_FF), D_MODEL),
        "ffn2_w3": w((L, D_MODEL, D_FF), D_MODEL),
        "ffn2_w2": w((L, D_FF, D_MODEL), D_FF),
        "final_norm": gain((L, D_MODEL)),
    }


def _fwd_reference(x, mem, positions, ffn1_norm, ffn1_w1, ffn1_w3, ffn1_w2, mix_norm, w_in,
              conv_w, conv_b, conv_ln_g, conv_ln_b, swa_q_norm, swa_k_norm, swa_sinks,
              mem_norm, w_mem_kv, mem_q_norm, mem_k_norm, w_out,
              ffn2_norm, ffn2_w1, ffn2_w3, ffn2_w2, final_norm):
    B, S, _ = x.shape
    cos, sin = rope_tables(positions)
    splits = [2 * CONV_CH, 2 * CONV_CH + SWA_Q, 2 * CONV_CH + SWA_Q + SWA_KV,
              2 * CONV_CH + SWA_Q + 2 * SWA_KV]
    for l in range(DEPTH):
        h = x + 0.5 * swiglu(rms_norm(x, ffn1_norm[l]), ffn1_w1[l], ffn1_w3[l], ffn1_w2[l])

        n = rms_norm(h, mix_norm[l])
        p = n @ w_in[l]
        u_conv, q, k, v, q_mem = jnp.split(p, splits, axis=-1)

        y_conv = conv_module(u_conv, conv_w[l], conv_b[l], conv_ln_g[l], conv_ln_b[l])

        q = apply_rope(rms_norm(q.reshape(B, S, N_Q_HEADS, HEAD_DIM), swa_q_norm[l]), cos, sin)
        k = apply_rope(rms_norm(k.reshape(B, S, N_KV_HEADS, HEAD_DIM), swa_k_norm[l]), cos, sin)
        v = v.reshape(B, S, N_KV_HEADS, HEAD_DIM)
        y_swa = sliding_window_attention(q, k, v, swa_sinks[l])

        mkv = rms_norm(mem, mem_norm[l]) @ w_mem_kv[l]
        mk, mv = jnp.split(mkv, 2, axis=-1)
        mk = rms_norm(mk.reshape(B, MEM_LEN, N_MEM_HEADS, HEAD_DIM), mem_k_norm[l])
        mv = mv.reshape(B, MEM_LEN, N_MEM_HEADS, HEAD_DIM)
        qm = rms_norm(q_mem.reshape(B, S, N_MEM_HEADS, HEAD_DIM), mem_q_norm[l])
        y_mem = memory_attention(qm, mk, mv)

        y = jnp.concatenate([y_conv, y_swa, y_mem], axis=-1)
        h = h + y @ w_out[l]

        h = h + 0.5 * swiglu(rms_norm(h, ffn2_norm[l]), ffn2_w1[l], ffn2_w3[l], ffn2_w2[l])

        x = rms_norm(h, final_norm[l])
    return x


import jax as _jax
import jax.numpy as _jnp

TWIN_FORMAT = 'train_step'
FWD_PARAMS = ['x', 'mem', 'positions', 'ffn1_norm', 'ffn1_w1', 'ffn1_w3', 'ffn1_w2', 'mix_norm', 'w_in', 'conv_w', 'conv_b', 'conv_ln_g', 'conv_ln_b', 'swa_q_norm', 'swa_k_norm', 'swa_sinks', 'mem_norm', 'w_mem_kv', 'mem_q_norm', 'mem_k_norm', 'w_out', 'ffn2_norm', 'ffn2_w1', 'ffn2_w3', 'ffn2_w2', 'final_norm']
TWIN_WEIGHTS = ['ffn1_norm', 'ffn1_w1', 'ffn1_w3', 'ffn1_w2', 'mix_norm', 'w_in', 'conv_w', 'conv_b', 'conv_ln_g', 'conv_ln_b', 'swa_q_norm', 'swa_k_norm', 'swa_sinks', 'mem_norm', 'w_mem_kv', 'mem_q_norm', 'mem_k_norm', 'w_out', 'ffn2_norm', 'ffn2_w1', 'ffn2_w3', 'ffn2_w2', 'final_norm']
TWIN_DIFF_INPUT = 'x'
TWIN_INPUTS = ['x', 'mem', 'positions', 'ffn1_norm', 'ffn1_w1', 'ffn1_w3', 'ffn1_w2', 'mix_norm', 'w_in', 'conv_w', 'conv_b', 'conv_ln_g', 'conv_ln_b', 'swa_q_norm', 'swa_k_norm', 'swa_sinks', 'mem_norm', 'w_mem_kv', 'mem_q_norm', 'mem_k_norm', 'w_out', 'ffn2_norm', 'ffn2_w1', 'ffn2_w3', 'ffn2_w2', 'final_norm', 'loss_target', 'm_ffn1_norm', 'm_ffn1_w1', 'm_ffn1_w3', 'm_ffn1_w2', 'm_mix_norm', 'm_w_in', 'm_conv_w', 'm_conv_b', 'm_conv_ln_g', 'm_conv_ln_b', 'm_swa_q_norm', 'm_swa_k_norm', 'm_swa_sinks', 'm_mem_norm', 'm_w_mem_kv', 'm_mem_q_norm', 'm_mem_k_norm', 'm_w_out', 'm_ffn2_norm', 'm_ffn2_w1', 'm_ffn2_w3', 'm_ffn2_w2', 'm_final_norm', 'v_ffn1_norm', 'v_ffn1_w1', 'v_ffn1_w3', 'v_ffn1_w2', 'v_mix_norm', 'v_w_in', 'v_conv_w', 'v_conv_b', 'v_conv_ln_g', 'v_conv_ln_b', 'v_swa_q_norm', 'v_swa_k_norm', 'v_swa_sinks', 'v_mem_norm', 'v_w_mem_kv', 'v_mem_q_norm', 'v_mem_k_norm', 'v_w_out', 'v_ffn2_norm', 'v_ffn2_w1', 'v_ffn2_w3', 'v_ffn2_w2', 'v_final_norm']
TWIN_OUTPUTS = ['loss', 'grad_x', 'grad_ffn1_norm', 'grad_ffn1_w1', 'grad_ffn1_w3', 'grad_ffn1_w2', 'grad_mix_norm', 'grad_w_in', 'grad_conv_w', 'grad_conv_b', 'grad_conv_ln_g', 'grad_conv_ln_b', 'grad_swa_q_norm', 'grad_swa_k_norm', 'grad_swa_sinks', 'grad_mem_norm', 'grad_w_mem_kv', 'grad_mem_q_norm', 'grad_mem_k_norm', 'grad_w_out', 'grad_ffn2_norm', 'grad_ffn2_w1', 'grad_ffn2_w3', 'grad_ffn2_w2', 'grad_final_norm', 'delta_ffn1_norm', 'delta_ffn1_w1', 'delta_ffn1_w3', 'delta_ffn1_w2', 'delta_mix_norm', 'delta_w_in', 'delta_conv_w', 'delta_conv_b', 'delta_conv_ln_g', 'delta_conv_ln_b', 'delta_swa_q_norm', 'delta_swa_k_norm', 'delta_swa_sinks', 'delta_mem_norm', 'delta_w_mem_kv', 'delta_mem_q_norm', 'delta_mem_k_norm', 'delta_w_out', 'delta_ffn2_norm', 'delta_ffn2_w1', 'delta_ffn2_w3', 'delta_ffn2_w2', 'delta_final_norm', 'new_m_ffn1_norm', 'new_m_ffn1_w1', 'new_m_ffn1_w3', 'new_m_ffn1_w2', 'new_m_mix_norm', 'new_m_w_in', 'new_m_conv_w', 'new_m_conv_b', 'new_m_conv_ln_g', 'new_m_conv_ln_b', 'new_m_swa_q_norm', 'new_m_swa_k_norm', 'new_m_swa_sinks', 'new_m_mem_norm', 'new_m_w_mem_kv', 'new_m_mem_q_norm', 'new_m_mem_k_norm', 'new_m_w_out', 'new_m_ffn2_norm', 'new_m_ffn2_w1', 'new_m_ffn2_w3', 'new_m_ffn2_w2', 'new_m_final_norm', 'new_v_ffn1_norm', 'new_v_ffn1_w1', 'new_v_ffn1_w3', 'new_v_ffn1_w2', 'new_v_mix_norm', 'new_v_w_in', 'new_v_conv_w', 'new_v_conv_b', 'new_v_conv_ln_g', 'new_v_conv_ln_b', 'new_v_swa_q_norm', 'new_v_swa_k_norm', 'new_v_swa_sinks', 'new_v_mem_norm', 'new_v_w_mem_kv', 'new_v_mem_q_norm', 'new_v_mem_k_norm', 'new_v_w_out', 'new_v_ffn2_norm', 'new_v_ffn2_w1', 'new_v_ffn2_w3', 'new_v_ffn2_w2', 'new_v_final_norm']
TWIN_LEAF_KINDS = {'loss': 'loss', 'grad_x': 'grad_x', 'grad_ffn1_norm': 'grad_w', 'grad_ffn1_w1': 'grad_w', 'grad_ffn1_w3': 'grad_w', 'grad_ffn1_w2': 'grad_w', 'grad_mix_norm': 'grad_w', 'grad_w_in': 'grad_w', 'grad_conv_w': 'grad_w', 'grad_conv_b': 'grad_w', 'grad_conv_ln_g': 'grad_w', 'grad_conv_ln_b': 'grad_w', 'grad_swa_q_norm': 'grad_w', 'grad_swa_k_norm': 'grad_w', 'grad_swa_sinks': 'grad_w', 'grad_mem_norm': 'grad_w', 'grad_w_mem_kv': 'grad_w', 'grad_mem_q_norm': 'grad_w', 'grad_mem_k_norm': 'grad_w', 'grad_w_out': 'grad_w', 'grad_ffn2_norm': 'grad_w', 'grad_ffn2_w1': 'grad_w', 'grad_ffn2_w3': 'grad_w', 'grad_ffn2_w2': 'grad_w', 'grad_final_norm': 'grad_w', 'delta_ffn1_norm': 'delta_w', 'delta_ffn1_w1': 'delta_w', 'delta_ffn1_w3': 'delta_w', 'delta_ffn1_w2': 'delta_w', 'delta_mix_norm': 'delta_w', 'delta_w_in': 'delta_w', 'delta_conv_w': 'delta_w', 'delta_conv_b': 'delta_w', 'delta_conv_ln_g': 'delta_w', 'delta_conv_ln_b': 'delta_w', 'delta_swa_q_norm': 'delta_w', 'delta_swa_k_norm': 'delta_w', 'delta_swa_sinks': 'delta_w', 'delta_mem_norm': 'delta_w', 'delta_w_mem_kv': 'delta_w', 'delta_mem_q_norm': 'delta_w', 'delta_mem_k_norm': 'delta_w', 'delta_w_out': 'delta_w', 'delta_ffn2_norm': 'delta_w', 'delta_ffn2_w1': 'delta_w', 'delta_ffn2_w3': 'delta_w', 'delta_ffn2_w2': 'delta_w', 'delta_final_norm': 'delta_w', 'new_m_ffn1_norm': 'new_m', 'new_m_ffn1_w1': 'new_m', 'new_m_ffn1_w3': 'new_m', 'new_m_ffn1_w2': 'new_m', 'new_m_mix_norm': 'new_m', 'new_m_w_in': 'new_m', 'new_m_conv_w': 'new_m', 'new_m_conv_b': 'new_m', 'new_m_conv_ln_g': 'new_m', 'new_m_conv_ln_b': 'new_m', 'new_m_swa_q_norm': 'new_m', 'new_m_swa_k_norm': 'new_m', 'new_m_swa_sinks': 'new_m', 'new_m_mem_norm': 'new_m', 'new_m_w_mem_kv': 'new_m', 'new_m_mem_q_norm': 'new_m', 'new_m_mem_k_norm': 'new_m', 'new_m_w_out': 'new_m', 'new_m_ffn2_norm': 'new_m', 'new_m_ffn2_w1': 'new_m', 'new_m_ffn2_w3': 'new_m', 'new_m_ffn2_w2': 'new_m', 'new_m_final_norm': 'new_m', 'new_v_ffn1_norm': 'new_v', 'new_v_ffn1_w1': 'new_v', 'new_v_ffn1_w3': 'new_v', 'new_v_ffn1_w2': 'new_v', 'new_v_mix_norm': 'new_v', 'new_v_w_in': 'new_v', 'new_v_conv_w': 'new_v', 'new_v_conv_b': 'new_v', 'new_v_conv_ln_g': 'new_v', 'new_v_conv_ln_b': 'new_v', 'new_v_swa_q_norm': 'new_v', 'new_v_swa_k_norm': 'new_v', 'new_v_swa_sinks': 'new_v', 'new_v_mem_norm': 'new_v', 'new_v_w_mem_kv': 'new_v', 'new_v_mem_q_norm': 'new_v', 'new_v_mem_k_norm': 'new_v', 'new_v_w_out': 'new_v', 'new_v_ffn2_norm': 'new_v', 'new_v_ffn2_w1': 'new_v', 'new_v_ffn2_w3': 'new_v', 'new_v_ffn2_w2': 'new_v', 'new_v_final_norm': 'new_v'}


def _forward(args):
    return _fwd_reference(*[args[k] for k in FWD_PARAMS])


def _output_shape():
    def fwd():
        inp = _fwd_setup_inputs(0)
        return _fwd_reference(*[inp[k] for k in FWD_PARAMS])
    out = _jax.eval_shape(fwd)
    return out.shape, out.dtype

N_MICROBATCH = 1
ADAM_LR = 0.001
ADAM_B1 = 0.9
ADAM_B2 = 0.999
ADAM_EPS = 1e-08
ADAM_WD = 0.01
ADAM_STEP = 10
PER_EXAMPLE_BATCH_AXIS = {'x': 0, 'mem': 0, 'positions': 0, 'loss_target': 0}
SHARED_INPUTS = []
_WEIGHT_DTYPES = {'ffn1_norm': _jnp.float32, 'ffn1_w1': _jnp.float32, 'ffn1_w3': _jnp.float32, 'ffn1_w2': _jnp.float32, 'mix_norm': _jnp.float32, 'w_in': _jnp.float32, 'conv_w': _jnp.float32, 'conv_b': _jnp.float32, 'conv_ln_g': _jnp.float32, 'conv_ln_b': _jnp.float32, 'swa_q_norm': _jnp.float32, 'swa_k_norm': _jnp.float32, 'swa_sinks': _jnp.float32, 'mem_norm': _jnp.float32, 'w_mem_kv': _jnp.float32, 'mem_q_norm': _jnp.float32, 'mem_k_norm': _jnp.float32, 'w_out': _jnp.float32, 'ffn2_norm': _jnp.float32, 'ffn2_w1': _jnp.float32, 'ffn2_w3': _jnp.float32, 'ffn2_w2': _jnp.float32, 'final_norm': _jnp.float32}
MOMENT_SCALE = {'ffn1_norm': 1.268595e-01, 'ffn1_w1': 5.463281e-02, 'ffn1_w3': 5.301112e-02, 'ffn1_w2': 8.789727e-02, 'mix_norm': 1.273398e-01, 'w_in': 9.946452e-02, 'conv_w': 1.689281e-01, 'conv_b': 4.078521e-01, 'conv_ln_g': 2.299568e-01, 'conv_ln_b': 2.255073e-01, 'swa_q_norm': 1.160071e-01, 'swa_k_norm': 1.166376e-01, 'swa_sinks': 6.535624e-02, 'mem_norm': 2.152687e-02, 'w_mem_kv': 2.887035e-02, 'mem_q_norm': 6.415943e-02, 'mem_k_norm': 6.341934e-02, 'w_out': 1.118666e-01, 'ffn2_norm': 1.083859e-01, 'ffn2_w1': 4.674460e-02, 'ffn2_w3': 4.553962e-02, 'ffn2_w2': 7.541785e-02, 'final_norm': 3.209475e+01}


def _to_microbatches(a, axis):
    t = _jnp.moveaxis(a, axis, 0)
    t = t.reshape((N_MICROBATCH, t.shape[0] // N_MICROBATCH) + t.shape[1:])
    return _jnp.moveaxis(t, 1, axis + 1)


def setup_inputs(seed: int = 0) -> dict:
    inp = _fwd_setup_inputs(seed)
    key = _jax.random.fold_in(_jax.random.key(seed), 7919)
    shape, _ = _output_shape()
    out = dict(inp)
    out["loss_target"] = _jax.random.normal(_jax.random.fold_in(key, 0), shape, _jnp.float32)
    for i, name in enumerate(TWIN_WEIGHTS):
        w = inp[name].astype(_jnp.float32)
        if MOMENT_SCALE is None:
            s = _jnp.sqrt(_jnp.mean(_jnp.square(w)) + 1e-30)
        else:
            s = MOMENT_SCALE[name]
        km, kv = _jax.random.split(_jax.random.fold_in(key, i + 1))
        out[name] = w
        out["m_" + name] = s * _jax.random.normal(km, w.shape, _jnp.float32)
        out["v_" + name] = (s * s) * _jax.random.uniform(kv, w.shape, _jnp.float32, 0.5, 1.5)
    if N_MICROBATCH > 1:
        for name, axis in PER_EXAMPLE_BATCH_AXIS.items():
            out[name] = _to_microbatches(out[name], axis)
    return {'x': out['x'], 'mem': out['mem'], 'positions': out['positions'], 'ffn1_norm': out['ffn1_norm'], 'ffn1_w1': out['ffn1_w1'], 'ffn1_w3': out['ffn1_w3'], 'ffn1_w2': out['ffn1_w2'], 'mix_norm': out['mix_norm'], 'w_in': out['w_in'], 'conv_w': out['conv_w'], 'conv_b': out['conv_b'], 'conv_ln_g': out['conv_ln_g'], 'conv_ln_b': out['conv_ln_b'], 'swa_q_norm': out['swa_q_norm'], 'swa_k_norm': out['swa_k_norm'], 'swa_sinks': out['swa_sinks'], 'mem_norm': out['mem_norm'], 'w_mem_kv': out['w_mem_kv'], 'mem_q_norm': out['mem_q_norm'], 'mem_k_norm': out['mem_k_norm'], 'w_out': out['w_out'], 'ffn2_norm': out['ffn2_norm'], 'ffn2_w1': out['ffn2_w1'], 'ffn2_w3': out['ffn2_w3'], 'ffn2_w2': out['ffn2_w2'], 'final_norm': out['final_norm'], 'loss_target': out['loss_target'], 'm_ffn1_norm': out['m_ffn1_norm'], 'm_ffn1_w1': out['m_ffn1_w1'], 'm_ffn1_w3': out['m_ffn1_w3'], 'm_ffn1_w2': out['m_ffn1_w2'], 'm_mix_norm': out['m_mix_norm'], 'm_w_in': out['m_w_in'], 'm_conv_w': out['m_conv_w'], 'm_conv_b': out['m_conv_b'], 'm_conv_ln_g': out['m_conv_ln_g'], 'm_conv_ln_b': out['m_conv_ln_b'], 'm_swa_q_norm': out['m_swa_q_norm'], 'm_swa_k_norm': out['m_swa_k_norm'], 'm_swa_sinks': out['m_swa_sinks'], 'm_mem_norm': out['m_mem_norm'], 'm_w_mem_kv': out['m_w_mem_kv'], 'm_mem_q_norm': out['m_mem_q_norm'], 'm_mem_k_norm': out['m_mem_k_norm'], 'm_w_out': out['m_w_out'], 'm_ffn2_norm': out['m_ffn2_norm'], 'm_ffn2_w1': out['m_ffn2_w1'], 'm_ffn2_w3': out['m_ffn2_w3'], 'm_ffn2_w2': out['m_ffn2_w2'], 'm_final_norm': out['m_final_norm'], 'v_ffn1_norm': out['v_ffn1_norm'], 'v_ffn1_w1': out['v_ffn1_w1'], 'v_ffn1_w3': out['v_ffn1_w3'], 'v_ffn1_w2': out['v_ffn1_w2'], 'v_mix_norm': out['v_mix_norm'], 'v_w_in': out['v_w_in'], 'v_conv_w': out['v_conv_w'], 'v_conv_b': out['v_conv_b'], 'v_conv_ln_g': out['v_conv_ln_g'], 'v_conv_ln_b': out['v_conv_ln_b'], 'v_swa_q_norm': out['v_swa_q_norm'], 'v_swa_k_norm': out['v_swa_k_norm'], 'v_swa_sinks': out['v_swa_sinks'], 'v_mem_norm': out['v_mem_norm'], 'v_w_mem_kv': out['v_w_mem_kv'], 'v_mem_q_norm': out['v_mem_q_norm'], 'v_mem_k_norm': out['v_mem_k_norm'], 'v_w_out': out['v_w_out'], 'v_ffn2_norm': out['v_ffn2_norm'], 'v_ffn2_w1': out['v_ffn2_w1'], 'v_ffn2_w3': out['v_ffn2_w3'], 'v_ffn2_w2': out['v_ffn2_w2'], 'v_final_norm': out['v_final_norm']}


def _loss(weights, diff, rest, loss_target):
    with _jax.named_scope("forward"):
        args = {**rest, TWIN_DIFF_INPUT: diff, **{k: w.astype(_WEIGHT_DTYPES[k]) for k, w in weights.items()}}
        y = _forward(args)
    with _jax.named_scope("loss_head"):
        err = _jnp.square(y.astype(_jnp.float32) - loss_target)
        return 0.5 * _jnp.sum(_jnp.mean(err, axis=-1)) if err.ndim else 0.5 * err


def _adamw(w, g, m, v):
    m = ADAM_B1 * m + (1.0 - ADAM_B1) * g
    v = ADAM_B2 * v + (1.0 - ADAM_B2) * _jnp.square(g)
    m_hat = m / (1.0 - ADAM_B1 ** ADAM_STEP)
    v_hat = v / (1.0 - ADAM_B2 ** ADAM_STEP)
    delta = -ADAM_LR * (m_hat / (_jnp.sqrt(v_hat) + ADAM_EPS) + ADAM_WD * w)
    return delta, m, v


def reference(x, mem, positions, ffn1_norm, ffn1_w1, ffn1_w3, ffn1_w2, mix_norm, w_in, conv_w, conv_b, conv_ln_g, conv_ln_b, swa_q_norm, swa_k_norm, swa_sinks, mem_norm, w_mem_kv, mem_q_norm, mem_k_norm, w_out, ffn2_norm, ffn2_w1, ffn2_w3, ffn2_w2, final_norm, loss_target, m_ffn1_norm, m_ffn1_w1, m_ffn1_w3, m_ffn1_w2, m_mix_norm, m_w_in, m_conv_w, m_conv_b, m_conv_ln_g, m_conv_ln_b, m_swa_q_norm, m_swa_k_norm, m_swa_sinks, m_mem_norm, m_w_mem_kv, m_mem_q_norm, m_mem_k_norm, m_w_out, m_ffn2_norm, m_ffn2_w1, m_ffn2_w3, m_ffn2_w2, m_final_norm, v_ffn1_norm, v_ffn1_w1, v_ffn1_w3, v_ffn1_w2, v_mix_norm, v_w_in, v_conv_w, v_conv_b, v_conv_ln_g, v_conv_ln_b, v_swa_q_norm, v_swa_k_norm, v_swa_sinks, v_mem_norm, v_w_mem_kv, v_mem_q_norm, v_mem_k_norm, v_w_out, v_ffn2_norm, v_ffn2_w1, v_ffn2_w3, v_ffn2_w2, v_final_norm):
    given = dict(x=x, mem=mem, positions=positions, ffn1_norm=ffn1_norm, ffn1_w1=ffn1_w1, ffn1_w3=ffn1_w3, ffn1_w2=ffn1_w2, mix_norm=mix_norm, w_in=w_in, conv_w=conv_w, conv_b=conv_b, conv_ln_g=conv_ln_g, conv_ln_b=conv_ln_b, swa_q_norm=swa_q_norm, swa_k_norm=swa_k_norm, swa_sinks=swa_sinks, mem_norm=mem_norm, w_mem_kv=w_mem_kv, mem_q_norm=mem_q_norm, mem_k_norm=mem_k_norm, w_out=w_out, ffn2_norm=ffn2_norm, ffn2_w1=ffn2_w1, ffn2_w3=ffn2_w3, ffn2_w2=ffn2_w2, final_norm=final_norm, loss_target=loss_target, m_ffn1_norm=m_ffn1_norm, m_ffn1_w1=m_ffn1_w1, m_ffn1_w3=m_ffn1_w3, m_ffn1_w2=m_ffn1_w2, m_mix_norm=m_mix_norm, m_w_in=m_w_in, m_conv_w=m_conv_w, m_conv_b=m_conv_b, m_conv_ln_g=m_conv_ln_g, m_conv_ln_b=m_conv_ln_b, m_swa_q_norm=m_swa_q_norm, m_swa_k_norm=m_swa_k_norm, m_swa_sinks=m_swa_sinks, m_mem_norm=m_mem_norm, m_w_mem_kv=m_w_mem_kv, m_mem_q_norm=m_mem_q_norm, m_mem_k_norm=m_mem_k_norm, m_w_out=m_w_out, m_ffn2_norm=m_ffn2_norm, m_ffn2_w1=m_ffn2_w1, m_ffn2_w3=m_ffn2_w3, m_ffn2_w2=m_ffn2_w2, m_final_norm=m_final_norm, v_ffn1_norm=v_ffn1_norm, v_ffn1_w1=v_ffn1_w1, v_ffn1_w3=v_ffn1_w3, v_ffn1_w2=v_ffn1_w2, v_mix_norm=v_mix_norm, v_w_in=v_w_in, v_conv_w=v_conv_w, v_conv_b=v_conv_b, v_conv_ln_g=v_conv_ln_g, v_conv_ln_b=v_conv_ln_b, v_swa_q_norm=v_swa_q_norm, v_swa_k_norm=v_swa_k_norm, v_swa_sinks=v_swa_sinks, v_mem_norm=v_mem_norm, v_w_mem_kv=v_w_mem_kv, v_mem_q_norm=v_mem_q_norm, v_mem_k_norm=v_mem_k_norm, v_w_out=v_w_out, v_ffn2_norm=v_ffn2_norm, v_ffn2_w1=v_ffn2_w1, v_ffn2_w3=v_ffn2_w3, v_ffn2_w2=v_ffn2_w2, v_final_norm=v_final_norm)
    weights = {n: given[n] for n in TWIN_WEIGHTS}
    shared = {n: given[n] for n in SHARED_INPUTS}
    per_example = {n: given[n] for n in ['x', 'mem', 'positions']}
    grad_fn = _jax.value_and_grad(_loss, argnums=(0, 1))

    def one_microbatch(ex, loss_target):
        ex = dict(ex)
        diff = ex.pop(TWIN_DIFF_INPUT)
        return grad_fn(weights, diff, {**shared, **ex}, loss_target)

    if N_MICROBATCH == 1:
        loss, (grad_w, grad_x) = one_microbatch(per_example, given["loss_target"])
    else:
        def body(carry, xs):
            loss_sum, grad_sum = carry
            l_k, (gw_k, gx_k) = one_microbatch(xs[0], xs[1])
            with _jax.named_scope("update"):
                return (loss_sum + l_k, _jax.tree.map(_jnp.add, grad_sum, gw_k)), gx_k

        init = (_jnp.zeros((), _jnp.float32), _jax.tree.map(_jnp.zeros_like, weights))
        (loss, grad_w), grad_x = _jax.lax.scan(body, init, (per_example, given["loss_target"]))
    with _jax.named_scope("update"):
        delta_w, new_m, new_v = {}, {}, {}
        for n in TWIN_WEIGHTS:
            delta_w[n], new_m[n], new_v[n] = _adamw(weights[n], grad_w[n], given["m_" + n], given["v_" + n])
    return (loss, grad_x, *[grad_w[n] for n in TWIN_WEIGHTS], *[delta_w[n] for n in TWIN_WEIGHTS],
            *[new_m[n] for n in TWIN_WEIGHTS], *[new_v[n] for n in TWIN_WEIGHTS])
```

```python
import functools

import jax
import jax.numpy as jnp
from jax import lax
from jax.experimental import pallas as pl
from jax.experimental.pallas import tpu as pltpu

F32 = jnp.float32
BF = jnp.bfloat16
EPS = 1e-6
HEAD = 64
N_Q, N_KV, N_MEMH = 6, 2, 4
CONV_CH, CONV_W = 384, 31
BLOCK = 128
ROPE_THETA = 10000.0
N_DEV = 8
V7X_VMEM_LIMIT = 56 * 1024 * 1024
LANE = 128

ADAM_LR, ADAM_B1, ADAM_B2, ADAM_EPS, ADAM_WD, ADAM_STEP = 0.001, 0.9, 0.999, 1e-08, 0.01, 10

TILE, HALO, PARAM = "tile", "halo", "param"
MESH = pl.DeviceIdType.MESH


def _cparams(sem=None):
    kw = dict(vmem_limit_bytes=V7X_VMEM_LIMIT)
    if sem is not None:
        kw["dimension_semantics"] = sem
    return pltpu.CompilerParams(**kw)


def _dot(a, b, dims):
    return lax.dot_general(a, b, (dims, ((), ())), preferred_element_type=F32)


_NN, _NT, _TN = ((1,), (0,)), ((1,), (1,)), ((0,), (0,))


@jax.custom_vjp
def mm(a, b):
    return _dot(a.astype(BF), b.astype(BF), _NN)


def _mm_fwd(a, b):
    return mm(a, b), (a, b)


def _mm_bwd(res, g):
    a, b = res
    gb = g.astype(BF)
    return _dot(gb, b.astype(BF), _NT), _dot(a.astype(BF), gb, _TN)


mm.defvjp(_mm_fwd, _mm_bwd)


@jax.custom_vjp
def mm_nt(a, b):
    return _dot(a.astype(BF), b.astype(BF), _NT)


def _mm_nt_fwd(a, b):
    return mm_nt(a, b), (a, b)


def _mm_nt_bwd(res, g):
    a, b = res
    gb = g.astype(BF)
    return _dot(gb, b.astype(BF), _NN), _dot(gb, a.astype(BF), _TN)


mm_nt.defvjp(_mm_nt_fwd, _mm_nt_bwd)


def rms(x, g):
    return x * lax.rsqrt(jnp.mean(x * x, axis=-1, keepdims=True) + EPS) * g


def _swap_matrix():
    i = lax.broadcasted_iota(jnp.int32, (HEAD, HEAD), 0)
    j = lax.broadcasted_iota(jnp.int32, (HEAD, HEAD), 1)
    src = jnp.where(j < HEAD // 2, j + HEAD // 2, j - HEAD // 2)
    return (i == src).astype(F32)


def rope(x, c, s, swap):
    xs = jnp.dot(x, swap, precision=lax.Precision.HIGHEST, preferred_element_type=F32)
    return x * c + xs * s


@jax.custom_vjp
def causal_dw_conv(g, w):
    tm = g.shape[0] - 32
    acc = w[0:1, :] * g[2:2 + tm]
    for j in range(1, CONV_W):
        acc = acc + w[j:j + 1, :] * g[2 + j:2 + j + tm]
    return acc


def _conv_fwd(g, w):
    return causal_dw_conv(g, w), (g, w)


def _conv_bwd(res, dc):
    g, w = res
    tm, ch = dc.shape
    z = jnp.zeros((32, ch), F32)
    dcp = jnp.concatenate([z, dc, z], axis=0)
    rows = lax.broadcasted_iota(jnp.int32, (32, 1), 0)
    dg = w[0:1, :] * dcp[30:30 + tm + 32]
    dw = jnp.where(rows == 0, jnp.sum(dc * g[2:2 + tm], axis=0, keepdims=True), 0.0)
    for j in range(1, CONV_W):
        dg = dg + w[j:j + 1, :] * dcp[30 - j:30 - j + tm + 32]
        dw = dw + jnp.where(rows == j, jnp.sum(dc * g[2 + j:2 + j + tm], axis=0, keepdims=True), 0.0)
    return dg, dw


causal_dw_conv.defvjp(_conv_fwd, _conv_bwd)


def f_rms(first, x, g):
    return (rms(x, g),)


def f_proj_in(first, h, g, w):
    return (mm(rms(h, g), w),)


def f_out_proj(first, h, y, w):
    return (h + mm(y, w),)


def f_conv(first, a, gate, w, b, lg, lb):
    keep = 1.0 - first
    av = jnp.concatenate([a[0] * keep, a[1]], axis=0)
    gv = jnp.concatenate([gate[0], gate[1]], axis=0)
    glu = av * jax.nn.sigmoid(gv)
    c = causal_dw_conv(glu, w) + b
    mu = jnp.mean(c, axis=-1, keepdims=True)
    var = jnp.mean(jnp.square(c - mu), axis=-1, keepdims=True)
    z = (c - mu) * lax.rsqrt(var + EPS) * lg + lb
    return (z * jax.nn.sigmoid(z),)


def _softmax_with_extra(s, extra):
    m = jnp.max(s, axis=-1, keepdims=True)
    if extra is not None:
        m = jnp.maximum(m, extra)
    m = lax.stop_gradient(m)
    e = jnp.exp(s - m)
    den = jnp.sum(e, axis=-1, keepdims=True)
    if extra is not None:
        den = den + jnp.exp(extra - m)
    return e / den


def f_swa(first, q, k, v, ct, st, qn, kn, sinks):
    tm = q.shape[1]
    nb = tm // BLOCK
    g = N_Q // N_KV
    swap = _swap_matrix()
    c_all = jnp.concatenate([ct[0], ct[1]], axis=0)
    s_all = jnp.concatenate([st[0], st[1]], axis=0)
    qi = lax.broadcasted_iota(jnp.int32, (g * BLOCK, 2 * BLOCK), 0)
    kj = lax.broadcasted_iota(jnp.int32, (g * BLOCK, 2 * BLOCK), 1)
    qpos = jnp.where(qi >= 2 * BLOCK, qi - 2 * BLOCK, jnp.where(qi >= BLOCK, qi - BLOCK, qi)) + BLOCK
    rel = qpos - kj
    band = (rel >= 0) & (rel < BLOCK)
    band_first = band & ((kj >= BLOCK) | (first < 0.5))
    lane = lax.broadcasted_iota(jnp.int32, (g * BLOCK, LANE), 1)
    hrow = lax.broadcasted_iota(jnp.int32, (g * BLOCK, LANE), 0)
    head_in_group = jnp.where(hrow >= 2 * BLOCK, 2, jnp.where(hrow >= BLOCK, 1, 0))
    qr = [rope(rms(q[h], qn), ct[1], st[1], swap) for h in range(N_Q)]
    outs = [[None] * nb for _ in range(N_Q)]
    for hk in range(N_KV):
        kk = rope(rms(jnp.concatenate([k[0][hk], k[1][hk]], axis=0), kn), c_all, s_all, swap)
        vv = jnp.concatenate([v[0][hk], v[1][hk]], axis=0)
        sel = (lane == head_in_group + hk * g).astype(F32)
        sink_col = jnp.sum(sel * sinks, axis=1, keepdims=True)
        for j in range(nb):
            keys = kk[j * BLOCK:(j + 2) * BLOCK]
            vals = vv[j * BLOCK:(j + 2) * BLOCK]
            qs = jnp.concatenate([qr[hk * g + gg][j * BLOCK:(j + 1) * BLOCK] for gg in range(g)], axis=0)
            s = mm_nt(qs, keys) * (HEAD ** -0.5)
            s = jnp.where(band_first if j == 0 else band, s, -1e30)
            o = mm(_softmax_with_extra(s, sink_col), vals)
            for gg in range(g):
                outs[hk * g + gg][j] = o[gg * BLOCK:(gg + 1) * BLOCK]
    return (jnp.stack([jnp.concatenate(outs[h], axis=0) for h in range(N_Q)]),)


def f_mem(first, qm, mk, mv, qn, kn):
    outs = []
    for h in range(N_MEMH):
        qh = rms(qm[h], qn)
        kh = rms(mk[h], kn)
        s = mm_nt(qh, kh) * (HEAD ** -0.5)
        outs.append(mm(_softmax_with_extra(s, None), mv[h]))
    return (jnp.stack(outs),)


def _seq_len(arr):
    return arr.shape[0] if arr.ndim == 2 else arr.shape[1]


def _tile_spec(arr, rows, imap):
    if arr.ndim == 2:
        return pl.BlockSpec((rows, arr.shape[1]), lambda i: (imap(i), 0))
    return pl.BlockSpec((arr.shape[0], rows, arr.shape[2]), lambda i: (0, imap(i), 0))


def _full_spec(arr):
    nd = arr.ndim
    return pl.BlockSpec(arr.shape, lambda i: (0,) * nd)


def _in_specs(args, tm, hl, nt):
    specs = []
    ratio = tm // hl
    cur = lambda i: jnp.minimum(i, nt - 1)
    prev = lambda i: jnp.maximum(jnp.minimum(i, nt - 1) * ratio - 1, 0)
    for arr, kind, _ in args:
        if kind == TILE:
            specs.append(_tile_spec(arr, tm, cur))
        elif kind == HALO:
            specs.append(_tile_spec(arr, hl, prev))
            specs.append(_tile_spec(arr, tm, cur))
        else:
            specs.append(_full_spec(arr))
    return specs


def _operands(args):
    ops = []
    for arr, kind, _ in args:
        ops.append(arr)
        if kind == HALO:
            ops.append(arr)
    return ops


def _load_values(args, refs):
    vals, k = [], 0
    for arr, kind, _ in args:
        if kind == HALO:
            vals.append((refs[k][...].astype(F32), refs[k + 1][...].astype(F32)))
            k += 2
        else:
            vals.append(refs[k][...].astype(F32))
            k += 1
    return vals


def seq_fwd(f, args, outs, tm, hl, name):
    seq = _seq_len(outs_probe(outs))
    nt = seq // tm
    n_in = len(_operands(args))

    def body(*refs):
        first = (pl.program_id(0) == 0).astype(F32)
        res = f(first, *_load_values(args, refs[:n_in]))
        for r, o in zip(refs[n_in:], res):
            r[...] = o.astype(r.dtype)

    out_shape = [jax.ShapeDtypeStruct(s, d) for s, d in outs]
    cur = lambda i: i
    res = pl.pallas_call(
        body, grid=(nt,), in_specs=_in_specs(args, tm, hl, nt),
        out_specs=[_tile_spec(o, tm, cur) for o in out_shape], out_shape=out_shape,
        compiler_params=_cparams(("arbitrary",)), name=name,
    )(*_operands(args))
    return res


def outs_probe(outs):
    return jax.ShapeDtypeStruct(*outs[0])


def seq_bwd(f, args, douts, tm, hl, name):
    seq = _seq_len(douts[0])
    nt = seq // tm
    lag = any(kind == HALO and diff for _, kind, diff in args)
    steps = nt + 1 if lag else nt
    n_in = len(_operands(args))
    n_do = len(douts)
    dargs = [(arr, kind) for arr, kind, diff in args if diff]
    n_carry = sum(1 for _, kind in dargs if kind == HALO)

    def body(*refs):
        in_refs = refs[:n_in]
        do_refs = refs[n_in:n_in + n_do]
        g_refs = refs[n_in + n_do:n_in + n_do + len(dargs)]
        carries = refs[n_in + n_do + len(dargs):]
        i = pl.program_id(0)
        first = (i == 0).astype(F32)

        def compute():
            vals = _load_values(args, in_refs)
            dvals = [v for v, (_, _, diff) in zip(vals, args) if diff]

            def fd(*dv):
                it = iter(dv)
                return f(first, *[next(it) if diff else v for v, (_, _, diff) in zip(vals, args)])

            _, vjp = jax.vjp(fd, *dvals)
            grads = vjp(tuple(r[...].astype(F32) for r in do_refs))
            c = 0
            for gref, gval, (arr, kind) in zip(g_refs, grads, dargs):
                if kind == TILE:
                    gref[...] = gval.astype(gref.dtype)
                elif kind == PARAM:
                    @pl.when(i == 0)
                    def _():
                        gref[...] = gval

                    @pl.when(i > 0)
                    def _():
                        gref[...] += gval
                else:
                    carry = carries[c]
                    c += 1
                    g_prev, g_cur = gval

                    @pl.when(i > 0)
                    def _():
                        gref[...] = carry[...]
                        if arr.ndim == 2:
                            gref[tm - hl:tm, :] += g_prev
                        else:
                            gref[:, tm - hl:tm, :] += g_prev

                    carry[...] = g_cur

        if lag:
            pl.when(i < nt)(compute)

            @pl.when(i == nt)
            def _():
                c = 0
                for gref, (arr, kind) in zip(g_refs, dargs):
                    if kind == HALO:
                        gref[...] = carries[c][...]
                        c += 1
        else:
            compute()

    cur = lambda i: jnp.minimum(i, nt - 1)
    lagged = lambda i: jnp.maximum(i - 1, 0)
    out_shape, out_specs, scratch = [], [], []
    for arr, kind in dargs:
        if kind == PARAM:
            out_shape.append(jax.ShapeDtypeStruct(arr.shape, F32))
            out_specs.append(_full_spec(arr))
        else:
            out_shape.append(jax.ShapeDtypeStruct(arr.shape, F32))
            out_specs.append(_tile_spec(arr, tm, lagged if kind == HALO else cur))
            if kind == HALO:
                blk = (tm, arr.shape[1]) if arr.ndim == 2 else (arr.shape[0], tm, arr.shape[2])
                scratch.append(pltpu.VMEM(blk, F32))
    in_specs = _in_specs(args, tm, hl, nt) + [_tile_spec(d, tm, cur) for d in douts]
    return pl.pallas_call(
        body, grid=(steps,), in_specs=in_specs, out_specs=out_specs, out_shape=out_shape,
        scratch_shapes=scratch, compiler_params=_cparams(("arbitrary",)), name=name,
    )(*_operands(args), *douts)


def ffn_fwd(x, g, w1, w3, w2, tm, name):
    seq, dm = x.shape
    nc, _, fc = w1.shape
    nt = seq // tm

    def body(x_ref, g_ref, w1_ref, w3_ref, w2_ref, h_ref, xn_s, acc_s):
        c = pl.program_id(1)

        @pl.when(c == 0)
        def _():
            xn_s[...] = rms(x_ref[...], g_ref[...]).astype(BF)
            acc_s[...] = jnp.zeros_like(acc_s)

        xn = xn_s[...]
        a = _dot(xn, w1_ref[0], _NN)
        b = _dot(xn, w3_ref[0], _NN)
        hid = (a * jax.nn.sigmoid(a)) * b
        acc_s[...] += _dot(hid.astype(BF), w2_ref[0], _NN)

        @pl.when(c == nc - 1)
        def _():
            h_ref[...] = x_ref[...] + 0.5 * acc_s[...]

    return pl.pallas_call(
        body, grid=(nt, nc),
        in_specs=[pl.BlockSpec((tm, dm), lambda i, c: (i, 0)), pl.BlockSpec((1, dm), lambda i, c: (0, 0)),
                  pl.BlockSpec((1, dm, fc), lambda i, c: (c, 0, 0)), pl.BlockSpec((1, dm, fc), lambda i, c: (c, 0, 0)),
                  pl.BlockSpec((1, fc, dm), lambda i, c: (c, 0, 0))],
        out_specs=pl.BlockSpec((tm, dm), lambda i, c: (i, 0)),
        out_shape=jax.ShapeDtypeStruct((seq, dm), F32),
        scratch_shapes=[pltpu.VMEM((tm, dm), BF), pltpu.VMEM((tm, dm), F32)],
        compiler_params=_cparams(("arbitrary", "arbitrary")), name=name,
    )(x, g, w1, w3, w2)


def ffn_bwd_chunks(x, g, dh, w1, w3, w2, tm, name):
    seq, dm = x.shape
    nc, _, fc = w1.shape
    nt = seq // tm

    def body(x_ref, g_ref, dh_ref, w1_ref, w3_ref, w2_ref, dxn_ref, dw1_ref, dw3_ref, dw2_ref):
        i = pl.program_id(1)
        xn = rms(x_ref[...], g_ref[...]).astype(BF)
        dy = (0.5 * dh_ref[...]).astype(BF)
        w1v, w3v, w2v = w1_ref[0], w3_ref[0], w2_ref[0]
        a = _dot(xn, w1v, _NN)
        b = _dot(xn, w3v, _NN)
        sig = jax.nn.sigmoid(a)
        sa = a * sig
        dhid = _dot(dy, w2v, _NT)
        db = (dhid * sa).astype(BF)
        da = (dhid * b * (sig * (1.0 + a * (1.0 - sig)))).astype(BF)
        dxn_ref[0] = _dot(da, w1v, _NT) + _dot(db, w3v, _NT)
        g1 = _dot(xn, da, _TN)
        g3 = _dot(xn, db, _TN)
        g2 = _dot((sa * b).astype(BF), dy, _TN)

        @pl.when(i == 0)
        def _():
            dw1_ref[0] = g1
            dw3_ref[0] = g3
            dw2_ref[0] = g2

        @pl.when(i > 0)
        def _():
            dw1_ref[0] += g1
            dw3_ref[0] += g3
            dw2_ref[0] += g2

    return pl.pallas_call(
        body, grid=(nc, nt),
        in_specs=[pl.BlockSpec((tm, dm), lambda c, i: (i, 0)), pl.BlockSpec((1, dm), lambda c, i: (0, 0)),
                  pl.BlockSpec((tm, dm), lambda c, i: (i, 0)),
                  pl.BlockSpec((1, dm, fc), lambda c, i: (c, 0, 0)), pl.BlockSpec((1, dm, fc), lambda c, i: (c, 0, 0)),
                  pl.BlockSpec((1, fc, dm), lambda c, i: (c, 0, 0))],
        out_specs=[pl.BlockSpec((1, tm, dm), lambda c, i: (c, i, 0)),
                   pl.BlockSpec((1, dm, fc), lambda c, i: (c, 0, 0)), pl.BlockSpec((1, dm, fc), lambda c, i: (c, 0, 0)),
                   pl.BlockSpec((1, fc, dm), lambda c, i: (c, 0, 0))],
        out_shape=[jax.ShapeDtypeStruct((nc, seq, dm), F32), jax.ShapeDtypeStruct((nc, dm, fc), F32),
                   jax.ShapeDtypeStruct((nc, dm, fc), F32), jax.ShapeDtypeStruct((nc, fc, dm), F32)],
        compiler_params=_cparams(("arbitrary", "arbitrary")), name=name,
    )(x, g, dh, w1, w3, w2)


def ffn_bwd_norm(x, g, parts, dh, tm, name):
    seq, dm = x.shape
    nc = parts.shape[0]
    nt = seq // tm

    def body(x_ref, g_ref, p_ref, dh_ref, dx_ref, dg_ref):
        i = pl.program_id(0)
        dxn = p_ref[0]
        for c in range(1, nc):
            dxn = dxn + p_ref[c]
        _, vjp = jax.vjp(rms, x_ref[...], g_ref[...])
        dx, dg = vjp(dxn)
        dx_ref[...] = dx + dh_ref[...]

        @pl.when(i == 0)
        def _():
            dg_ref[...] = dg

        @pl.when(i > 0)
        def _():
            dg_ref[...] += dg

    return pl.pallas_call(
        body, grid=(nt,),
        in_specs=[pl.BlockSpec((tm, dm), lambda i: (i, 0)), pl.BlockSpec((1, dm), lambda i: (0, 0)),
                  pl.BlockSpec((nc, tm, dm), lambda i: (0, i, 0)), pl.BlockSpec((tm, dm), lambda i: (i, 0))],
        out_specs=[pl.BlockSpec((tm, dm), lambda i: (i, 0)), pl.BlockSpec((1, dm), lambda i: (0, 0))],
        out_shape=[jax.ShapeDtypeStruct((seq, dm), F32), jax.ShapeDtypeStruct((1, dm), F32)],
        compiler_params=_cparams(("arbitrary",)), name=name,
    )(x, g, parts, dh)


def rope_tables(pos_col, inv_freq, tm):
    seq = pos_col.shape[0]

    def body(p_ref, f_ref, c_ref, s_ref):
        ang = p_ref[...].astype(F32) * f_ref[...]
        lane = lax.broadcasted_iota(jnp.int32, ang.shape, 1)
        c_ref[...] = jnp.cos(ang)
        s_ref[...] = jnp.where(lane < HEAD // 2, -jnp.sin(ang), jnp.sin(ang))

    return pl.pallas_call(
        body, grid=(seq // tm,),
        in_specs=[pl.BlockSpec((tm, 1), lambda i: (i, 0)), pl.BlockSpec((1, HEAD), lambda i: (0, 0))],
        out_specs=[pl.BlockSpec((tm, HEAD), lambda i: (i, 0))] * 2,
        out_shape=[jax.ShapeDtypeStruct((seq, HEAD), F32)] * 2,
        compiler_params=_cparams(("arbitrary",)), name="rope_tables",
    )(pos_col, inv_freq)


def loss_and_grad(y, target, tm):
    seq, dm = y.shape

    def body(y_ref, t_ref, l_ref, dy_ref):
        i = pl.program_id(0)
        err = y_ref[...] - t_ref[...]
        dy_ref[...] = err * (1.0 / dm)
        part = 0.5 * jnp.sum(jnp.mean(err * err, axis=-1, keepdims=True), axis=0, keepdims=True)
        part = jnp.broadcast_to(part, (1, LANE))

        @pl.when(i == 0)
        def _():
            l_ref[...] = part

        @pl.when(i > 0)
        def _():
            l_ref[...] += part

    return pl.pallas_call(
        body, grid=(seq // tm,),
        in_specs=[pl.BlockSpec((tm, dm), lambda i: (i, 0))] * 2,
        out_specs=[pl.BlockSpec((1, LANE), lambda i: (0, 0)), pl.BlockSpec((tm, dm), lambda i: (i, 0))],
        out_shape=[jax.ShapeDtypeStruct((1, LANE), F32), jax.ShapeDtypeStruct((seq, dm), F32)],
        compiler_params=_cparams(("arbitrary",)), name="loss_and_grad",
    )(y, target)


def adamw(w, g, m, v, rows, name):
    r, c = w.shape

    def body(w_ref, g_ref, m_ref, v_ref, d_ref, nm_ref, nv_ref):
        gv = g_ref[...]
        nm = ADAM_B1 * m_ref[...] + (1.0 - ADAM_B1) * gv
        nv = ADAM_B2 * v_ref[...] + (1.0 - ADAM_B2) * (gv * gv)
        m_hat = nm / (1.0 - ADAM_B1 ** ADAM_STEP)
        v_hat = nv / (1.0 - ADAM_B2 ** ADAM_STEP)
        d_ref[...] = -ADAM_LR * (m_hat / (jnp.sqrt(v_hat) + ADAM_EPS) + ADAM_WD * w_ref[...])
        nm_ref[...] = nm
        nv_ref[...] = nv

    spec = pl.BlockSpec((rows, c), lambda i: (i, 0))
    return pl.pallas_call(
        body, grid=(r // rows,), in_specs=[spec] * 4, out_specs=[spec] * 3,
        out_shape=[jax.ShapeDtypeStruct((r, c), F32)] * 3,
        compiler_params=_cparams(("arbitrary",)), name=name,
    )(w, g, m, v)


def sum_blocks(blocks, rows, name):
    n, r, c = blocks.shape

    def body(b_ref, o_ref):
        acc = b_ref[0]
        for j in range(1, n):
            acc = acc + b_ref[j]
        o_ref[...] = acc

    return pl.pallas_call(
        body, grid=(r // rows,),
        in_specs=[pl.BlockSpec((n, rows, c), lambda i: (0, i, 0))],
        out_specs=pl.BlockSpec((rows, c), lambda i: (i, 0)),
        out_shape=jax.ShapeDtypeStruct((r, c), F32),
        compiler_params=_cparams(("arbitrary",)), name=name,
    )(blocks)


def _me():
    return lax.axis_index("x"), lax.axis_index("y"), lax.axis_index("c")


def _peer(k):
    x, y, c = _me()
    fx, fy, fc = (k >> 2) & 1, (k >> 1) & 1, k & 1
    px = 1 - x if fx else x
    py = 1 - y if fy else y
    pc = 1 - c if fc else c
    return (px, py, pc), 4 * px + 2 * py + pc


def all_gather(shard, name):
    r, c = shard.shape

    def body(s_ref, o_ref, send_sems, recv_sems, local_sem):
        x, y, cc = _me()
        mine = 4 * x + 2 * y + cc
        own = pltpu.make_async_copy(s_ref, o_ref.at[mine], local_sem)
        own.start()
        sends = []
        for k in range(1, N_DEV):
            peer, _ = _peer(k)
            cp = pltpu.make_async_remote_copy(src_ref=s_ref, dst_ref=o_ref.at[mine], send_sem=send_sems.at[k - 1],
                                              recv_sem=recv_sems.at[k - 1], device_id=peer, device_id_type=MESH)
            cp.start()
            sends.append(cp)
        for k in range(1, N_DEV):
            peer, pj = _peer(k)
            pltpu.make_async_remote_copy(src_ref=s_ref, dst_ref=o_ref.at[pj], send_sem=send_sems.at[k - 1],
                                         recv_sem=recv_sems.at[k - 1], device_id=peer, device_id_type=MESH).wait_recv()
        for cp in sends:
            cp.wait_send()
        own.wait()

    return pl.pallas_call(
        body, out_shape=jax.ShapeDtypeStruct((N_DEV, r, c), shard.dtype),
        in_specs=[pl.BlockSpec(memory_space=pl.ANY)], out_specs=pl.BlockSpec(memory_space=pl.ANY),
        scratch_shapes=[pltpu.SemaphoreType.DMA((N_DEV - 1,)), pltpu.SemaphoreType.DMA((N_DEV - 1,)),
                        pltpu.SemaphoreType.DMA],
        name=name,
    )(shard)


def exchange_blocks(blocks, name):
    n, r, c = blocks.shape

    def body(b_ref, o_ref, send_sems, recv_sems, local_sem):
        x, y, cc = _me()
        mine = 4 * x + 2 * y + cc
        own = pltpu.make_async_copy(b_ref.at[mine], o_ref.at[mine], local_sem)
        own.start()
        sends = []
        for k in range(1, N_DEV):
            peer, pj = _peer(k)
            cp = pltpu.make_async_remote_copy(src_ref=b_ref.at[pj], dst_ref=o_ref.at[mine], send_sem=send_sems.at[k - 1],
                                              recv_sem=recv_sems.at[k - 1], device_id=peer, device_id_type=MESH)
            cp.start()
            sends.append(cp)
        for k in range(1, N_DEV):
            peer, pj = _peer(k)
            pltpu.make_async_remote_copy(src_ref=b_ref.at[pj], dst_ref=o_ref.at[pj], send_sem=send_sems.at[k - 1],
                                         recv_sem=recv_sems.at[k - 1], device_id=peer, device_id_type=MESH).wait_recv()
        for cp in sends:
            cp.wait_send()
        own.wait()

    return pl.pallas_call(
        body, out_shape=jax.ShapeDtypeStruct((n, r, c), blocks.dtype),
        in_specs=[pl.BlockSpec(memory_space=pl.ANY)], out_specs=pl.BlockSpec(memory_space=pl.ANY),
        scratch_shapes=[pltpu.SemaphoreType.DMA((N_DEV - 1,)), pltpu.SemaphoreType.DMA((N_DEV - 1,)),
                        pltpu.SemaphoreType.DMA],
        name=name,
    )(blocks)


def _heads(t, n):
    return t.reshape(t.shape[0], n, HEAD).transpose(1, 0, 2)


def _unheads(t):
    return t.transpose(1, 0, 2).reshape(t.shape[1], t.shape[0] * HEAD)


def _split_p(p):
    o = 0
    parts = []
    for width, n in ((CONV_CH, 0), (CONV_CH, 0), (N_Q * HEAD, N_Q), (N_KV * HEAD, N_KV), (N_KV * HEAD, N_KV),
                     (N_MEMH * HEAD, N_MEMH)):
        t = p[:, o:o + width]
        parts.append(_heads(t, n) if n else t)
        o += width
    return parts


def _mix_args(parts, mkv_heads, tabs, sw):
    a, gate, q, k, v, qm = parts
    mk, mv = mkv_heads
    ct, st = tabs
    conv_args = [(a, HALO, True), (gate, HALO, True), (sw["conv_w"], PARAM, True), (sw["conv_b"], PARAM, True),
                 (sw["conv_ln_g"], PARAM, True), (sw["conv_ln_b"], PARAM, True)]
    swa_args = [(q, TILE, True), (k, HALO, True), (v, HALO, True), (ct, HALO, False), (st, HALO, False),
                (sw["swa_q_norm"], PARAM, True), (sw["swa_k_norm"], PARAM, True), (sw["swa_sinks"], PARAM, True)]
    mem_args = [(qm, TILE, True), (mk, PARAM, True), (mv, PARAM, True), (sw["mem_q_norm"], PARAM, True),
                (sw["mem_k_norm"], PARAM, True)]
    return conv_args, swa_args, mem_args


def _mkv_heads(mkv):
    mw = N_MEMH * HEAD
    return _heads(mkv[:, :mw], N_MEMH), _heads(mkv[:, mw:], N_MEMH)


def layer_fwd(x, mem, tabs, bw, sw, tm, l):
    seq, dm = x.shape
    tag = f"_l{l}"
    h1 = ffn_fwd(x, sw["ffn1_norm"], bw["ffn1_w1"], bw["ffn1_w3"], bw["ffn1_w2"], tm, "ffn1_fwd" + tag)
    (p,) = seq_fwd(f_proj_in, [(h1, TILE, True), (sw["mix_norm"], PARAM, True), (bw["w_in"], PARAM, True)],
                   [((seq, bw["w_in"].shape[1]), F32)], tm, tm, "proj_in_fwd" + tag)
    parts = _split_p(p)
    ml = mem.shape[0]
    (mkv,) = seq_fwd(f_proj_in, [(mem, TILE, False), (sw["mem_norm"], PARAM, True), (bw["w_mem_kv"], PARAM, True)],
                     [((ml, bw["w_mem_kv"].shape[1]), F32)], ml, ml, "mem_kv_fwd" + tag)
    conv_args, swa_args, mem_args = _mix_args(parts, _mkv_heads(mkv), tabs, sw)
    (y_conv,) = seq_fwd(f_conv, conv_args, [((seq, CONV_CH), F32)], tm, 32, "conv_fwd" + tag)
    (y_swa,) = seq_fwd(f_swa, swa_args, [((N_Q, seq, HEAD), F32)], tm, BLOCK, "swa_fwd" + tag)
    (y_mem,) = seq_fwd(f_mem, mem_args, [((N_MEMH, seq, HEAD), F32)], tm, tm, "mem_attn_fwd" + tag)
    y = jnp.concatenate([y_conv, _unheads(y_swa), _unheads(y_mem)], axis=-1)
    (h2,) = seq_fwd(f_out_proj, [(h1, TILE, True), (y, TILE, True), (bw["w_out"], PARAM, True)],
                    [((seq, dm), F32)], tm, tm, "out_proj_fwd" + tag)
    h3 = ffn_fwd(h2, sw["ffn2_norm"], bw["ffn2_w1"], bw["ffn2_w3"], bw["ffn2_w2"], tm, "ffn2_fwd" + tag)
    (xo,) = seq_fwd(f_rms, [(h3, TILE, True), (sw["final_norm"], PARAM, True)], [((seq, dm), F32)], tm, tm,
                    "final_norm_fwd" + tag)
    saved = dict(x=x, h1=h1, p=p, mkv=mkv, y=y, h2=h2, h3=h3)
    return xo, saved


def layer_bwd(dxo, saved, mem, tabs, bw, sw, tm, l):
    tag = f"_l{l}"
    gb, gs = {}, {}
    x, h1, p, mkv, y, h2, h3 = (saved[k] for k in ("x", "h1", "p", "mkv", "y", "h2", "h3"))
    dh3, gs["final_norm"] = seq_bwd(f_rms, [(h3, TILE, True), (sw["final_norm"], PARAM, True)], [dxo], tm, tm,
                                    "final_norm_bwd" + tag)
    parts2, gb["ffn2_w1"], gb["ffn2_w3"], gb["ffn2_w2"] = ffn_bwd_chunks(
        h2, sw["ffn2_norm"], dh3, bw["ffn2_w1"], bw["ffn2_w3"], bw["ffn2_w2"], tm, "ffn2_bwd" + tag)
    dh2, gs["ffn2_norm"] = ffn_bwd_norm(h2, sw["ffn2_norm"], parts2, dh3, tm, "ffn2_norm_bwd" + tag)
    dh1_a, dy, gb["w_out"] = seq_bwd(f_out_proj, [(h1, TILE, True), (y, TILE, True), (bw["w_out"], PARAM, True)],
                                     [dh2], tm // 2, tm // 2, "out_proj_bwd" + tag)
    dy_conv = dy[:, :CONV_CH]
    dy_swa = _heads(dy[:, CONV_CH:CONV_CH + N_Q * HEAD], N_Q)
    dy_mem = _heads(dy[:, CONV_CH + N_Q * HEAD:], N_MEMH)
    conv_args, swa_args, mem_args = _mix_args(_split_p(p), _mkv_heads(mkv), tabs, sw)
    da, dgate, gs["conv_w"], gs["conv_b"], gs["conv_ln_g"], gs["conv_ln_b"] = seq_bwd(
        f_conv, conv_args, [dy_conv], tm, 32, "conv_bwd" + tag)
    dq, dk, dv, gs["swa_q_norm"], gs["swa_k_norm"], gs["swa_sinks"] = seq_bwd(
        f_swa, swa_args, [dy_swa], tm, BLOCK, "swa_bwd" + tag)
    dqm, dmk, dmv, gs["mem_q_norm"], gs["mem_k_norm"] = seq_bwd(f_mem, mem_args, [dy_mem], tm, tm, "mem_attn_bwd" + tag)
    dmkv = jnp.concatenate([_unheads(dmk), _unheads(dmv)], axis=-1)
    ml = mem.shape[0]
    gs["mem_norm"], gb["w_mem_kv"] = seq_bwd(
        f_proj_in, [(mem, TILE, False), (sw["mem_norm"], PARAM, True), (bw["w_mem_kv"], PARAM, True)], [dmkv], ml, ml,
        "mem_kv_bwd" + tag)
    dp = jnp.concatenate([da, dgate, _unheads(dq), _unheads(dk), _unheads(dv), _unheads(dqm)], axis=-1)
    dh1_b, gs["mix_norm"], gb["w_in"] = seq_bwd(
        f_proj_in, [(h1, TILE, True), (sw["mix_norm"], PARAM, True), (bw["w_in"], PARAM, True)], [dp], tm // 2, tm // 2,
        "proj_in_bwd" + tag)
    dh1 = add2(dh1_a, dh1_b, "add_dh1" + tag)
    parts1, gb["ffn1_w1"], gb["ffn1_w3"], gb["ffn1_w2"] = ffn_bwd_chunks(
        x, sw["ffn1_norm"], dh1, bw["ffn1_w1"], bw["ffn1_w3"], bw["ffn1_w2"], tm, "ffn1_bwd" + tag)
    dx, gs["ffn1_norm"] = ffn_bwd_norm(x, sw["ffn1_norm"], parts1, dh1, tm, "ffn1_norm_bwd" + tag)
    return dx, gb, gs


def add2(a, b, name):
    r, c = a.shape
    rows = min(r, 512)

    def body(a_ref, b_ref, o_ref):
        o_ref[...] = a_ref[...] + b_ref[...]

    spec = pl.BlockSpec((rows, c), lambda i: (i, 0))
    return pl.pallas_call(body, grid=(r // rows,), in_specs=[spec, spec], out_specs=spec,
                          out_shape=jax.ShapeDtypeStruct((r, c), F32), compiler_params=_cparams(("arbitrary",)),
                          name=name)(a, b)


BIG = ("ffn1_w1", "ffn1_w3", "ffn1_w2", "w_in", "w_mem_kv", "w_out", "ffn2_w1", "ffn2_w3", "ffn2_w2")
COL_SHARDED = ("ffn1_w1", "ffn1_w3", "w_in", "ffn2_w1", "ffn2_w3")
FFN_UP = ("ffn1_w1", "ffn1_w3", "ffn2_w1", "ffn2_w3")
FFN_DOWN = ("ffn1_w2", "ffn2_w2")
SMALL = ("ffn1_norm", "mix_norm", "conv_b", "conv_ln_g", "conv_ln_b", "swa_q_norm", "swa_k_norm", "swa_sinks",
         "mem_norm", "mem_q_norm", "mem_k_norm", "ffn2_norm", "final_norm")
WEIGHTS = ("ffn1_norm", "ffn1_w1", "ffn1_w3", "ffn1_w2", "mix_norm", "w_in", "conv_w", "conv_b", "conv_ln_g",
           "conv_ln_b", "swa_q_norm", "swa_k_norm", "swa_sinks", "mem_norm", "w_mem_kv", "mem_q_norm", "mem_k_norm",
           "w_out", "ffn2_norm", "ffn2_w1", "ffn2_w3", "ffn2_w2", "final_norm")
FFN_CHUNKS = 4


def pack_big(t):
    depth = t[BIG[0]].shape[0]
    return jnp.concatenate([t[n].reshape(depth, -1) for n in BIG], axis=1)


def unpack_big(flat, shapes):
    out, o = {}, 0
    for n in BIG:
        size = 1
        for s in shapes[n][1:]:
            size *= s
        out[n] = flat[:, o:o + size].reshape((flat.shape[0],) + tuple(shapes[n][1:]))
        o += size
    return out


def gathered_to_full(g, shard_shapes):
    out, o = {}, 0
    for n in BIG:
        a, b = shard_shapes[n][1:]
        t = g[:, o:o + a * b].reshape(N_DEV, a, b)
        o += a * b
        if n in FFN_UP:
            per = N_DEV // FFN_CHUNKS
            t = t.reshape(FFN_CHUNKS, per, a, b).transpose(0, 2, 1, 3).reshape(FFN_CHUNKS, a, per * b)
        elif n in FFN_DOWN:
            t = t.reshape(FFN_CHUNKS, N_DEV // FFN_CHUNKS * a, b)
        elif n in COL_SHARDED:
            t = t.transpose(1, 0, 2).reshape(a, N_DEV * b)
        else:
            t = t.reshape(N_DEV * a, b)
        out[n] = t
    return out


def full_to_blocks(gb, shard_shapes):
    cols = []
    for n in BIG:
        a, b = shard_shapes[n][1:]
        t = gb[n]
        if n in FFN_UP:
            per = N_DEV // FFN_CHUNKS
            t = t.reshape(FFN_CHUNKS, a, per, b).transpose(0, 2, 1, 3).reshape(N_DEV, a * b)
        elif n in FFN_DOWN:
            t = t.reshape(N_DEV, a * b)
        elif n in COL_SHARDED:
            t = t.reshape(a, N_DEV, b).transpose(1, 0, 2).reshape(N_DEV, a * b)
        else:
            t = t.reshape(N_DEV, a * b)
        cols.append(t)
    return jnp.concatenate(cols, axis=1)


def _row_block(rows, target):
    best = rows
    for cand in range(8, min(rows, target) + 1, 8):
        if rows % cand == 0:
            best = cand
    return best


def _pack_cols(n):
    return 1024 if n % (1024 * 16) == 0 else LANE


def _pad_rows(flat, cols, mult):
    n = flat.shape[0]
    rows = -(-n // cols)
    rows = -(-rows // mult) * mult
    return jnp.pad(flat, (0, rows * cols - n)).reshape(rows, cols)


def small_layer_params(w, l):
    sw = {n: w[n][l][None, :] for n in SMALL if n != "swa_sinks"}
    sw["swa_sinks"] = jnp.pad(w["swa_sinks"][l], (0, LANE - N_Q))[None, :]
    sw["conv_w"] = jnp.pad(w["conv_w_full"][l], ((0, 1), (0, 0)))
    return sw


def local_step(x, mem, pos, target, big_full, small, tm):
    depth = len(big_full)
    inv = ROPE_THETA ** (-jnp.arange(0, HEAD, 2, dtype=F32) / HEAD)
    inv = jnp.concatenate([inv, inv])[None, :]
    tabs = rope_tables(pos.reshape(-1, 1), inv, tm)
    saved = []
    h = x
    for l in range(depth):
        h, sv = layer_fwd(h, mem, tabs, big_full[l], small_layer_params(small, l), tm, l)
        saved.append(sv)
    loss, dh = loss_and_grad(h, target, tm)
    gbs, gss = [None] * depth, [None] * depth
    for l in reversed(range(depth)):
        dh, gbs[l], gss[l] = layer_bwd(dh, saved[l], mem, tabs, big_full[l], small_layer_params(small, l), tm, l)
    return loss, dh, gbs, gss


def kernel(x, mem, positions, ffn1_norm, ffn1_w1, ffn1_w3, ffn1_w2, mix_norm, w_in, conv_w, conv_b, conv_ln_g, conv_ln_b, swa_q_norm, swa_k_norm, swa_sinks, mem_norm, w_mem_kv, mem_q_norm, mem_k_norm, w_out, ffn2_norm, ffn2_w1, ffn2_w3, ffn2_w2, final_norm, loss_target, m_ffn1_norm, m_ffn1_w1, m_ffn1_w3, m_ffn1_w2, m_mix_norm, m_w_in, m_conv_w, m_conv_b, m_conv_ln_g, m_conv_ln_b, m_swa_q_norm, m_swa_k_norm, m_swa_sinks, m_mem_norm, m_w_mem_kv, m_mem_q_norm, m_mem_k_norm, m_w_out, m_ffn2_norm, m_ffn2_w1, m_ffn2_w3, m_ffn2_w2, m_final_norm, v_ffn1_norm, v_ffn1_w1, v_ffn1_w3, v_ffn1_w2, v_mix_norm, v_w_in, v_conv_w, v_conv_b, v_conv_ln_g, v_conv_ln_b, v_swa_q_norm, v_swa_k_norm, v_swa_sinks, v_mem_norm, v_w_mem_kv, v_mem_q_norm, v_mem_k_norm, v_w_out, v_ffn2_norm, v_ffn2_w1, v_ffn2_w3, v_ffn2_w2, v_final_norm):
    loc = locals()
    w = {n: loc[n] for n in WEIGHTS}
    m = {n: loc["m_" + n] for n in WEIGHTS}
    v = {n: loc["v_" + n] for n in WEIGHTS}
    depth = ffn1_norm.shape[0]
    seq = x.shape[1]
    tm = min(512, seq)
    me = 4 * lax.axis_index("x") + 2 * lax.axis_index("y") + lax.axis_index("c")

    shard_shapes = {n: w[n].shape for n in BIG}
    packed_w = pack_big(w)
    n_pack = packed_w.shape[1]
    pack_cols = _pack_cols(n_pack)
    rows = n_pack // pack_cols
    packed_bf = packed_w.astype(BF).reshape(depth, rows, pack_cols)
    big_full = []
    for l in range(depth):
        g = all_gather(packed_bf[l], f"all_gather_l{l}")
        big_full.append(gathered_to_full(g.reshape(N_DEV, n_pack), shard_shapes))

    cw = conv_w.shape[2]
    conv_blocks = jnp.zeros((N_DEV,) + conv_w.shape, F32)
    conv_blocks = lax.dynamic_update_slice(conv_blocks, conv_w[None], (me, 0, 0, 0))
    conv_rows = _pad_rows(conv_blocks.reshape(-1), LANE, 8)
    conv_all = sum_blocks(exchange_blocks(jnp.broadcast_to(conv_rows[None], (N_DEV,) + conv_rows.shape),
                                          "conv_w_gather"), conv_rows.shape[0], "conv_w_sum")
    conv_full = conv_all.reshape(-1)[:N_DEV * conv_w.size].reshape((N_DEV,) + conv_w.shape)
    conv_full = conv_full.transpose(1, 2, 0, 3).reshape(depth, CONV_W, N_DEV * cw)
    small = {n: w[n] for n in SMALL}
    small["conv_w_full"] = conv_full

    loss_sum, dx, gbs, gss = local_step(x[0], mem[0], positions[0], loss_target[0], big_full, small, tm)
    loss = lax.psum(loss_sum[0, 0], ("x", "y", "c"))

    red = []
    for l in reversed(range(depth)):
        blocks = full_to_blocks(gbs[l], shard_shapes).reshape(N_DEV, rows, pack_cols)
        got = exchange_blocks(blocks, f"grad_exchange_l{l}")
        red.append(sum_blocks(got, _row_block(rows, 64 * 1024 // pack_cols), f"grad_sum_l{l}"))
    red = red[::-1]
    g_packed = jnp.stack(red).reshape(depth, n_pack)
    g_big = unpack_big(g_packed, shard_shapes)

    small_names = SMALL + ("conv_w",)
    sizes = {n: (w[n].shape[1] if n != "conv_w" else CONV_W * N_DEV * cw) for n in SMALL + ("conv_w",)}
    flat = []
    for l in range(depth):
        for n in small_names:
            t = gss[l][n]
            if n == "swa_sinks":
                t = t[:, :N_Q]
            elif n == "conv_w":
                t = t[:CONV_W]
            flat.append(t.reshape(-1))
    flat = _pad_rows(jnp.concatenate(flat), LANE, 8)
    summed = sum_blocks(exchange_blocks(jnp.broadcast_to(flat[None], (N_DEV,) + flat.shape), "small_grad_exchange"),
                        flat.shape[0], "small_grad_sum").reshape(-1)
    g_small = {n: [] for n in small_names}
    o = 0
    for l in range(depth):
        for n in small_names:
            g_small[n].append(summed[o:o + sizes[n]])
            o += sizes[n]
    g_all = dict(g_big)
    for n in SMALL:
        g_all[n] = jnp.stack(g_small[n])
    gcw = jnp.stack(g_small["conv_w"]).reshape(depth, CONV_W, N_DEV, cw)
    g_all["conv_w"] = lax.dynamic_slice(gcw, (0, 0, me, 0), (depth, CONV_W, 1, cw)).reshape(depth, CONV_W, cw)

    def packed(t):
        return pack_big(t).reshape(depth * rows, pack_cols)

    d_b, m_b, v_b = adamw(packed(w), g_packed.reshape(depth * rows, pack_cols), packed(m), packed(v),
                          _row_block(depth * rows, 256 * 1024 // pack_cols), "adamw_big")
    outs = {"delta": unpack_big(d_b.reshape(depth, n_pack), shard_shapes),
            "new_m": unpack_big(m_b.reshape(depth, n_pack), shard_shapes),
            "new_v": unpack_big(v_b.reshape(depth, n_pack), shard_shapes)}

    def packed_small(t):
        return _pad_rows(jnp.concatenate([t[n].reshape(-1) for n in small_names]), LANE, 8)

    ps = [packed_small(t) for t in (w, g_all, m, v)]
    d_s, m_s, v_s = adamw(*ps, ps[0].shape[0], "adamw_small")
    for key, arr in (("delta", d_s), ("new_m", m_s), ("new_v", v_s)):
        fl, o = arr.reshape(-1), 0
        for n in small_names:
            outs[key][n] = fl[o:o + w[n].size].reshape(w[n].shape)
            o += w[n].size

    return (loss, dx[None], *[g_all[n] for n in WEIGHTS], *[outs["delta"][n] for n in WEIGHTS],
            *[outs["new_m"][n] for n in WEIGHTS], *[outs["new_v"][n] for n in WEIGHTS])
```

```python
import jax
import jax.numpy as jnp
from jax import lax
from jax.experimental import pallas as pl
from jax.experimental.pallas import tpu as pltpu

F32 = jnp.float32
BF = jnp.bfloat16
EPS = 1e-6
HEAD = 64
N_Q, N_KV, N_MEMH = 6, 2, 4
CONV_CH, CONV_W = 384, 31
BLOCK = 128
ROPE_THETA = 10000.0
N_DEV = 8
V7X_VMEM_LIMIT = 56 * 1024 * 1024
LANE = 128
BF16_ROWS = 16

ADAM_LR, ADAM_B1, ADAM_B2, ADAM_EPS, ADAM_WD, ADAM_STEP = 0.001, 0.9, 0.999, 1e-08, 0.01, 10

TILE, HALO, PARAM = "tile", "halo", "param"
MESH = pl.DeviceIdType.MESH
ANY = pl.BlockSpec(memory_space=pl.ANY)


def _cparams(sem=None):
    kw = dict(vmem_limit_bytes=V7X_VMEM_LIMIT)
    if sem is not None:
        kw["dimension_semantics"] = sem
    return pltpu.CompilerParams(**kw)


def _dot(a, b, dims):
    return lax.dot_general(a, b, (dims, ((), ())), preferred_element_type=F32)


_NN, _NT, _TN = ((1,), (0,)), ((1,), (1,)), ((0,), (0,))


@jax.custom_vjp
def mm(a, b):
    return _dot(a.astype(BF), b.astype(BF), _NN)


def _mm_fwd(a, b):
    return mm(a, b), (a, b)


def _mm_bwd(res, g):
    a, b = res
    gb = g.astype(BF)
    return _dot(gb, b.astype(BF), _NT), _dot(a.astype(BF), gb, _TN)


mm.defvjp(_mm_fwd, _mm_bwd)


@jax.custom_vjp
def mm_nt(a, b):
    return _dot(a.astype(BF), b.astype(BF), _NT)


def _mm_nt_fwd(a, b):
    return mm_nt(a, b), (a, b)


def _mm_nt_bwd(res, g):
    a, b = res
    gb = g.astype(BF)
    return _dot(gb, b.astype(BF), _NN), _dot(gb, a.astype(BF), _TN)


mm_nt.defvjp(_mm_nt_fwd, _mm_nt_bwd)


def rms(x, g):
    return x * lax.rsqrt(jnp.mean(x * x, axis=-1, keepdims=True) + EPS) * g


def _swap_matrix():
    i = lax.broadcasted_iota(jnp.int32, (HEAD, HEAD), 0)
    j = lax.broadcasted_iota(jnp.int32, (HEAD, HEAD), 1)
    src = jnp.where(j < HEAD // 2, j + HEAD // 2, j - HEAD // 2)
    return (i == src).astype(F32)


def rope(x, c, s, swap):
    xs = jnp.dot(x, swap, precision=lax.Precision.HIGHEST, preferred_element_type=F32)
    return x * c + xs * s


@jax.custom_vjp
def causal_dw_conv(g, w):
    tm = g.shape[0] - 32
    acc = w[0:1, :] * g[2:2 + tm]
    for j in range(1, CONV_W):
        acc = acc + w[j:j + 1, :] * g[2 + j:2 + j + tm]
    return acc


def _conv_fwd(g, w):
    return causal_dw_conv(g, w), (g, w)


def _conv_bwd(res, dc):
    g, w = res
    tm, ch = dc.shape
    z = jnp.zeros((32, ch), F32)
    dcp = jnp.concatenate([z, dc, z], axis=0)
    rows = lax.broadcasted_iota(jnp.int32, (32, 1), 0)
    dg = w[0:1, :] * dcp[30:30 + tm + 32]
    dw = jnp.where(rows == 0, jnp.sum(dc * g[2:2 + tm], axis=0, keepdims=True), 0.0)
    for j in range(1, CONV_W):
        dg = dg + w[j:j + 1, :] * dcp[30 - j:30 - j + tm + 32]
        dw = dw + jnp.where(rows == j, jnp.sum(dc * g[2 + j:2 + j + tm], axis=0, keepdims=True), 0.0)
    return dg, dw


causal_dw_conv.defvjp(_conv_fwd, _conv_bwd)


def f_rms(first, x, g):
    return (rms(x, g),)


def f_proj_in(first, h, g, w):
    return (mm(rms(h, g), w),)


def _to_heads(t, n):
    return jnp.stack([t[:, h * HEAD:(h + 1) * HEAD] for h in range(n)])


def f_proj_split(first, h, g, w):
    p = mm(rms(h, g), w)
    o, outs = 0, []
    for width, n in ((CONV_CH, 0), (CONV_CH, 0), (N_Q * HEAD, N_Q), (N_KV * HEAD, N_KV), (N_KV * HEAD, N_KV),
                     (N_MEMH * HEAD, N_MEMH)):
        t = p[:, o:o + width]
        outs.append(_to_heads(t, n) if n else t)
        o += width
    return tuple(outs)


def f_out_join(first, h, y_conv, y_swa, y_mem, w):
    y = jnp.concatenate([y_conv] + [y_swa[i] for i in range(N_Q)] + [y_mem[i] for i in range(N_MEMH)], axis=1)
    return (h + mm(y, w),)


def f_conv(first, a, gate, w, b, lg, lb):
    keep = 1.0 - first
    av = jnp.concatenate([a[0] * keep, a[1]], axis=0)
    gv = jnp.concatenate([gate[0], gate[1]], axis=0)
    glu = av * jax.nn.sigmoid(gv)
    c = causal_dw_conv(glu, w) + b
    mu = jnp.mean(c, axis=-1, keepdims=True)
    var = jnp.mean(jnp.square(c - mu), axis=-1, keepdims=True)
    z = (c - mu) * lax.rsqrt(var + EPS) * lg + lb
    return (z * jax.nn.sigmoid(z),)


def _softmax_with_extra(s, extra):
    m = jnp.max(s, axis=-1, keepdims=True)
    if extra is not None:
        m = jnp.maximum(m, extra)
    m = lax.stop_gradient(m)
    e = jnp.exp(s - m)
    den = jnp.sum(e, axis=-1, keepdims=True)
    if extra is not None:
        den = den + jnp.exp(extra - m)
    return e / den


def f_swa(first, q, k, v, ct, st, qn, kn, sinks):
    tm = q.shape[1]
    nb = tm // BLOCK
    g = N_Q // N_KV
    swap = _swap_matrix()
    c_all = jnp.concatenate([ct[0], ct[1]], axis=0)
    s_all = jnp.concatenate([st[0], st[1]], axis=0)
    qi = lax.broadcasted_iota(jnp.int32, (g * BLOCK, 2 * BLOCK), 0)
    kj = lax.broadcasted_iota(jnp.int32, (g * BLOCK, 2 * BLOCK), 1)
    qpos = jnp.where(qi >= 2 * BLOCK, qi - 2 * BLOCK, jnp.where(qi >= BLOCK, qi - BLOCK, qi)) + BLOCK
    rel = qpos - kj
    band = (rel >= 0) & (rel < BLOCK)
    band_first = band & ((kj >= BLOCK) | (first < 0.5))
    lane = lax.broadcasted_iota(jnp.int32, (g * BLOCK, LANE), 1)
    hrow = lax.broadcasted_iota(jnp.int32, (g * BLOCK, LANE), 0)
    head_in_group = jnp.where(hrow >= 2 * BLOCK, 2, jnp.where(hrow >= BLOCK, 1, 0))
    qr = [rope(rms(q[h], qn), ct[1], st[1], swap) for h in range(N_Q)]
    outs = [[None] * nb for _ in range(N_Q)]
    for hk in range(N_KV):
        kk = rope(rms(jnp.concatenate([k[0][hk], k[1][hk]], axis=0), kn), c_all, s_all, swap)
        vv = jnp.concatenate([v[0][hk], v[1][hk]], axis=0)
        sel = (lane == head_in_group + hk * g).astype(F32)
        sink_col = jnp.sum(sel * sinks, axis=1, keepdims=True)
        for j in range(nb):
            keys = kk[j * BLOCK:(j + 2) * BLOCK]
            vals = vv[j * BLOCK:(j + 2) * BLOCK]
            qs = jnp.concatenate([qr[hk * g + gg][j * BLOCK:(j + 1) * BLOCK] for gg in range(g)], axis=0)
            s = mm_nt(qs, keys) * (HEAD ** -0.5)
            s = jnp.where(band_first if j == 0 else band, s, -1e30)
            o = mm(_softmax_with_extra(s, sink_col), vals)
            for gg in range(g):
                outs[hk * g + gg][j] = o[gg * BLOCK:(gg + 1) * BLOCK]
    return (jnp.stack([jnp.concatenate(outs[h], axis=0) for h in range(N_Q)]),)


def f_mem(first, qm, mk, mv, qn, kn):
    outs = []
    for h in range(N_MEMH):
        qh = rms(qm[h], qn)
        kh = rms(mk[h], kn)
        s = mm_nt(qh, kh) * (HEAD ** -0.5)
        outs.append(mm(_softmax_with_extra(s, None), mv[h]))
    return (jnp.stack(outs),)


def _seq_len(arr):
    return arr.shape[0] if arr.ndim == 2 else arr.shape[1]


def _tile_spec(arr, rows, imap):
    if arr.ndim == 2:
        return pl.BlockSpec((rows, arr.shape[1]), lambda i: (imap(i), 0))
    return pl.BlockSpec((arr.shape[0], rows, arr.shape[2]), lambda i: (0, imap(i), 0))


def _full_spec(arr):
    nd = arr.ndim
    return pl.BlockSpec(arr.shape, lambda i: (0,) * nd)


def _in_specs(args, tm, hl, nt):
    specs = []
    ratio = tm // hl
    cur = lambda i: jnp.minimum(i, nt - 1)
    prev = lambda i: jnp.maximum(jnp.minimum(i, nt - 1) * ratio - 1, 0)
    for arr, kind, _ in args:
        if kind == TILE:
            specs.append(_tile_spec(arr, tm, cur))
        elif kind == HALO:
            specs.append(_tile_spec(arr, hl, prev))
            specs.append(_tile_spec(arr, tm, cur))
        else:
            specs.append(_full_spec(arr))
    return specs


def _operands(args):
    ops = []
    for arr, kind, _ in args:
        ops.append(arr)
        if kind == HALO:
            ops.append(arr)
    return ops


def _load_values(args, refs):
    vals, k = [], 0
    for arr, kind, _ in args:
        if kind == HALO:
            vals.append((refs[k][...].astype(F32), refs[k + 1][...].astype(F32)))
            k += 2
        else:
            vals.append(refs[k][...].astype(F32))
            k += 1
    return vals


def seq_fwd(f, args, outs, tm, hl, name):
    out_shape = [jax.ShapeDtypeStruct(s, d) for s, d in outs]
    nt = _seq_len(out_shape[0]) // tm
    n_in = len(_operands(args))

    def body(*refs):
        first = (pl.program_id(0) == 0).astype(F32)
        res = f(first, *_load_values(args, refs[:n_in]))
        for r, o in zip(refs[n_in:], res):
            r[...] = o.astype(r.dtype)

    cur = lambda i: i
    return pl.pallas_call(
        body, grid=(nt,), in_specs=_in_specs(args, tm, hl, nt),
        out_specs=[_tile_spec(o, tm, cur) for o in out_shape], out_shape=out_shape,
        compiler_params=_cparams(("arbitrary",)), name=name,
    )(*_operands(args))


def seq_bwd(f, args, douts, tm, hl, name):
    seq = _seq_len(douts[0])
    nt = seq // tm
    lag = any(kind == HALO and diff for _, kind, diff in args)
    steps = nt + 1 if lag else nt
    n_in = len(_operands(args))
    n_do = len(douts)
    dargs = [(arr, kind) for arr, kind, diff in args if diff]

    def body(*refs):
        in_refs = refs[:n_in]
        do_refs = refs[n_in:n_in + n_do]
        g_refs = refs[n_in + n_do:n_in + n_do + len(dargs)]
        carries = refs[n_in + n_do + len(dargs):]
        i = pl.program_id(0)
        first = (i == 0).astype(F32)

        def compute():
            vals = _load_values(args, in_refs)
            dvals = [v for v, (_, _, diff) in zip(vals, args) if diff]

            def fd(*dv):
                it = iter(dv)
                return f(first, *[next(it) if diff else v for v, (_, _, diff) in zip(vals, args)])

            _, vjp = jax.vjp(fd, *dvals)
            grads = vjp(tuple(r[...].astype(F32) for r in do_refs))
            c = 0
            for gref, gval, (arr, kind) in zip(g_refs, grads, dargs):
                if kind == TILE:
                    gref[...] = gval.astype(gref.dtype)
                elif kind == PARAM:
                    @pl.when(i == 0)
                    def _():
                        gref[...] = gval

                    @pl.when(i > 0)
                    def _():
                        gref[...] += gval
                else:
                    carry = carries[c]
                    c += 1
                    g_prev, g_cur = gval

                    @pl.when(i > 0)
                    def _():
                        gref[...] = carry[...]
                        if arr.ndim == 2:
                            gref[tm - hl:tm, :] += g_prev
                        else:
                            gref[:, tm - hl:tm, :] += g_prev

                    carry[...] = g_cur

        if lag:
            pl.when(i < nt)(compute)

            @pl.when(i == nt)
            def _():
                c = 0
                for gref, (arr, kind) in zip(g_refs, dargs):
                    if kind == HALO:
                        gref[...] = carries[c][...]
                        c += 1
        else:
            compute()

    cur = lambda i: jnp.minimum(i, nt - 1)
    lagged = lambda i: jnp.maximum(i - 1, 0)
    out_shape, out_specs, scratch = [], [], []
    for arr, kind in dargs:
        out_shape.append(jax.ShapeDtypeStruct(arr.shape, F32))
        if kind == PARAM:
            out_specs.append(_full_spec(arr))
        else:
            out_specs.append(_tile_spec(arr, tm, lagged if kind == HALO else cur))
            if kind == HALO:
                blk = (tm, arr.shape[1]) if arr.ndim == 2 else (arr.shape[0], tm, arr.shape[2])
                scratch.append(pltpu.VMEM(blk, F32))
    in_specs = _in_specs(args, tm, hl, nt) + [_tile_spec(d, tm, cur) for d in douts]
    return pl.pallas_call(
        body, grid=(steps,), in_specs=in_specs, out_specs=out_specs, out_shape=out_shape,
        scratch_shapes=scratch, compiler_params=_cparams(("arbitrary",)), name=name,
    )(*_operands(args), *douts)


class Item:
    def __init__(self, operand, out_shape, src, dst):
        self.operand, self.out_shape, self.src, self.dst = operand, out_shape, src, dst


def _dev(p):
    return 4 * p[0] + 2 * p[1] + p[2]


def gather_item(shard):
    return Item(shard, jax.ShapeDtypeStruct((N_DEV,) + shard.shape, shard.dtype),
                lambda r, peer: r, lambda r, s: r.at[_dev(s)])


def gather_cols_item(shard):
    d, w = shard.shape
    return Item(shard, jax.ShapeDtypeStruct((N_DEV // 2, d, 2 * w), shard.dtype),
                lambda r, peer: r, lambda r, s: r.at[2 * s[0] + s[1], :, pl.ds(s[2] * w, w)])


def scatter_item(blocks):
    return Item(blocks, jax.ShapeDtypeStruct(blocks.shape, blocks.dtype),
                lambda r, peer: r.at[_dev(peer)], lambda r, s: r.at[_dev(s)])


def scatter_cols_item(full):
    n, d, w2 = full.shape
    w = w2 // 2
    return Item(full, jax.ShapeDtypeStruct((N_DEV, d, w), full.dtype),
                lambda r, peer: r.at[2 * peer[0] + peer[1], :, pl.ds(peer[2] * w, w)], lambda r, s: r.at[_dev(s)])


def _comm_sems(items):
    n = len(items) * (N_DEV - 1)
    return [pltpu.SemaphoreType.DMA((n,)), pltpu.SemaphoreType.DMA((n,)), pltpu.SemaphoreType.DMA((len(items),))]


def _comm_copies(items, in_refs, out_refs, sems, x, y, c):
    send_sems, recv_sems, local_sems = sems
    me = (x, y, c)
    copies = []
    for t, it in enumerate(items):
        copies.append(pltpu.make_async_copy(it.src(in_refs[t], me), it.dst(out_refs[t], me), local_sems.at[t]))
        for k in range(1, N_DEV):
            peer = (1 - x if k & 4 else x, 1 - y if k & 2 else y, 1 - c if k & 1 else c)
            n = t * (N_DEV - 1) + k - 1
            copies.append(pltpu.make_async_remote_copy(
                src_ref=it.src(in_refs[t], peer), dst_ref=it.dst(out_refs[t], me), send_sem=send_sems.at[n],
                recv_sem=recv_sems.at[n], device_id=(peer[0], peer[1], jnp.int32(peer[2])), device_id_type=MESH))
    return copies


def comm_start(items, in_refs, out_refs, sems):
    x, y, c = lax.axis_index("x"), lax.axis_index("y"), lax.axis_index("c")
    for cv in (0, 1):
        @pl.when(c == cv)
        def _():
            for cp in _comm_copies(items, in_refs, out_refs, sems, x, y, cv):
                cp.start()


def comm_wait(items, in_refs, out_refs, sems):
    x, y, c = lax.axis_index("x"), lax.axis_index("y"), lax.axis_index("c")
    for cv in (0, 1):
        @pl.when(c == cv)
        def _():
            for cp in _comm_copies(items, in_refs, out_refs, sems, x, y, cv):
                cp.wait()


def comm_call(items, name):
    n = len(items)

    def body(*refs):
        in_refs, out_refs, sems = refs[:n], refs[n:2 * n], refs[2 * n:]
        comm_start(items, in_refs, out_refs, sems)
        comm_wait(items, in_refs, out_refs, sems)

    return pl.pallas_call(
        body, in_specs=[ANY] * n, out_specs=[ANY] * n, out_shape=[it.out_shape for it in items],
        scratch_shapes=_comm_sems(items), name=name,
    )(*[it.operand for it in items])


def ffn_fwd(x, g, w1, w3, w2, tm, name, items=()):
    seq, dm = x.shape
    nc, _, fc = w1.shape
    nt = seq // tm
    n = len(items)

    def body(*refs):
        x_ref, g_ref, w1_ref, w3_ref, w2_ref = refs[:5]
        c_in, h_ref, c_out = refs[5:5 + n], refs[5 + n], refs[6 + n:6 + 2 * n]
        xn_s, acc_s = refs[6 + 2 * n:8 + 2 * n]
        sems = refs[8 + 2 * n:]
        i, c = pl.program_id(0), pl.program_id(1)

        if n:
            @pl.when((i == 0) & (c == 0))
            def _():
                comm_start(items, c_in, c_out, sems)

        @pl.when(c == 0)
        def _():
            xn_s[...] = rms(x_ref[...], g_ref[...]).astype(BF)
            acc_s[...] = jnp.zeros_like(acc_s)

        xn = xn_s[...]
        a = _dot(xn, w1_ref[0], _NN)
        b = _dot(xn, w3_ref[0], _NN)
        hid = (a * jax.nn.sigmoid(a)) * b
        acc_s[...] += _dot(hid.astype(BF), w2_ref[0], _NN)

        @pl.when(c == nc - 1)
        def _():
            h_ref[...] = x_ref[...] + 0.5 * acc_s[...]

        if n:
            @pl.when((i == nt - 1) & (c == nc - 1))
            def _():
                comm_wait(items, c_in, c_out, sems)

    res = pl.pallas_call(
        body, grid=(nt, nc),
        in_specs=[pl.BlockSpec((tm, dm), lambda i, c: (i, 0)), pl.BlockSpec((1, dm), lambda i, c: (0, 0)),
                  pl.BlockSpec((1, dm, fc), lambda i, c: (c, 0, 0)), pl.BlockSpec((1, dm, fc), lambda i, c: (c, 0, 0)),
                  pl.BlockSpec((1, fc, dm), lambda i, c: (c, 0, 0))] + [ANY] * n,
        out_specs=[pl.BlockSpec((tm, dm), lambda i, c: (i, 0))] + [ANY] * n,
        out_shape=[jax.ShapeDtypeStruct((seq, dm), F32)] + [it.out_shape for it in items],
        scratch_shapes=[pltpu.VMEM((tm, dm), BF), pltpu.VMEM((tm, dm), F32)] + (_comm_sems(items) if n else []),
        compiler_params=_cparams(("arbitrary", "arbitrary")), name=name,
    )(x, g, w1, w3, w2, *[it.operand for it in items])
    return res[0], list(res[1:])


def ffn_bwd_chunks(x, g, dh, w1, w3, w2, tm, name, items=()):
    seq, dm = x.shape
    nc, _, fc = w1.shape
    nt = seq // tm
    n = len(items)

    def body(*refs):
        x_ref, g_ref, dh_ref, w1_ref, w3_ref, w2_ref = refs[:6]
        c_in = refs[6:6 + n]
        dxn_ref, dw1_ref, dw3_ref, dw2_ref = refs[6 + n:10 + n]
        c_out = refs[10 + n:10 + 2 * n]
        a1_s, a3_s, a2_s = refs[10 + 2 * n:13 + 2 * n]
        sems = refs[13 + 2 * n:]
        c, i = pl.program_id(0), pl.program_id(1)

        if n:
            @pl.when((i == 0) & (c == 0))
            def _():
                comm_start(items, c_in, c_out, sems)

        xn = rms(x_ref[...], g_ref[...]).astype(BF)
        dy = (0.5 * dh_ref[...]).astype(BF)
        w1v, w3v, w2v = w1_ref[0], w3_ref[0], w2_ref[0]
        a = _dot(xn, w1v, _NN)
        b = _dot(xn, w3v, _NN)
        sig = jax.nn.sigmoid(a)
        sa = a * sig
        dhid = _dot(dy, w2v, _NT)
        db = (dhid * sa).astype(BF)
        da = (dhid * b * (sig * (1.0 + a * (1.0 - sig)))).astype(BF)
        dxn_ref[0] = _dot(da, w1v, _NT) + _dot(db, w3v, _NT)
        g1 = _dot(xn, da, _TN)
        g3 = _dot(xn, db, _TN)
        g2 = _dot((sa * b).astype(BF), dy, _TN)

        @pl.when(i == 0)
        def _():
            a1_s[...] = g1
            a3_s[...] = g3
            a2_s[...] = g2

        @pl.when(i > 0)
        def _():
            a1_s[...] += g1
            a3_s[...] += g3
            a2_s[...] += g2

        @pl.when(i == nt - 1)
        def _():
            dw1_ref[0] = a1_s[...].astype(BF)
            dw3_ref[0] = a3_s[...].astype(BF)
            dw2_ref[0] = a2_s[...].astype(BF)

        if n:
            @pl.when((i == nt - 1) & (c == nc - 1))
            def _():
                comm_wait(items, c_in, c_out, sems)

    res = pl.pallas_call(
        body, grid=(nc, nt),
        in_specs=[pl.BlockSpec((tm, dm), lambda c, i: (i, 0)), pl.BlockSpec((1, dm), lambda c, i: (0, 0)),
                  pl.BlockSpec((tm, dm), lambda c, i: (i, 0)),
                  pl.BlockSpec((1, dm, fc), lambda c, i: (c, 0, 0)), pl.BlockSpec((1, dm, fc), lambda c, i: (c, 0, 0)),
                  pl.BlockSpec((1, fc, dm), lambda c, i: (c, 0, 0))] + [ANY] * n,
        out_specs=[pl.BlockSpec((1, tm, dm), lambda c, i: (c, i, 0)),
                   pl.BlockSpec((1, dm, fc), lambda c, i: (c, 0, 0)), pl.BlockSpec((1, dm, fc), lambda c, i: (c, 0, 0)),
                   pl.BlockSpec((1, fc, dm), lambda c, i: (c, 0, 0))] + [ANY] * n,
        out_shape=[jax.ShapeDtypeStruct((nc, seq, dm), F32), jax.ShapeDtypeStruct((nc, dm, fc), BF),
                   jax.ShapeDtypeStruct((nc, dm, fc), BF), jax.ShapeDtypeStruct((nc, fc, dm), BF)]
                  + [it.out_shape for it in items],
        scratch_shapes=[pltpu.VMEM((dm, fc), F32), pltpu.VMEM((dm, fc), F32), pltpu.VMEM((fc, dm), F32)]
                       + (_comm_sems(items) if n else []),
        compiler_params=_cparams(("arbitrary", "arbitrary")), name=name,
    )(x, g, dh, w1, w3, w2, *[it.operand for it in items])
    return res[0], res[1], res[2], res[3], list(res[4:])


def ffn_bwd_norm(x, g, parts, dh, tm, name):
    seq, dm = x.shape
    nc = parts.shape[0]
    nt = seq // tm

    def body(x_ref, g_ref, p_ref, dh_ref, dx_ref, dg_ref):
        i = pl.program_id(0)
        dxn = p_ref[0]
        for c in range(1, nc):
            dxn = dxn + p_ref[c]
        _, vjp = jax.vjp(rms, x_ref[...], g_ref[...])
        dx, dg = vjp(dxn)
        dx_ref[...] = dx + dh_ref[...]

        @pl.when(i == 0)
        def _():
            dg_ref[...] = dg

        @pl.when(i > 0)
        def _():
            dg_ref[...] += dg

    return pl.pallas_call(
        body, grid=(nt,),
        in_specs=[pl.BlockSpec((tm, dm), lambda i: (i, 0)), pl.BlockSpec((1, dm), lambda i: (0, 0)),
                  pl.BlockSpec((nc, tm, dm), lambda i: (0, i, 0)), pl.BlockSpec((tm, dm), lambda i: (i, 0))],
        out_specs=[pl.BlockSpec((tm, dm), lambda i: (i, 0)), pl.BlockSpec((1, dm), lambda i: (0, 0))],
        out_shape=[jax.ShapeDtypeStruct((seq, dm), F32), jax.ShapeDtypeStruct((1, dm), F32)],
        compiler_params=_cparams(("arbitrary",)), name=name,
    )(x, g, parts, dh)


def rope_tables(pos_col, inv_freq, tm):
    seq = pos_col.shape[0]

    def body(p_ref, f_ref, c_ref, s_ref):
        ang = p_ref[...].astype(F32) * f_ref[...]
        lane = lax.broadcasted_iota(jnp.int32, ang.shape, 1)
        c_ref[...] = jnp.cos(ang)
        s_ref[...] = jnp.where(lane < HEAD // 2, -jnp.sin(ang), jnp.sin(ang))

    return pl.pallas_call(
        body, grid=(seq // tm,),
        in_specs=[pl.BlockSpec((tm, 1), lambda i: (i, 0)), pl.BlockSpec((1, HEAD), lambda i: (0, 0))],
        out_specs=[pl.BlockSpec((tm, HEAD), lambda i: (i, 0))] * 2,
        out_shape=[jax.ShapeDtypeStruct((seq, HEAD), F32)] * 2,
        compiler_params=_cparams(("arbitrary",)), name="rope_tables",
    )(pos_col, inv_freq)


def loss_and_grad(y, target, tm):
    seq, dm = y.shape

    def body(y_ref, t_ref, l_ref, dy_ref):
        i = pl.program_id(0)
        err = y_ref[...] - t_ref[...]
        dy_ref[...] = err * (1.0 / dm)
        part = 0.5 * jnp.sum(jnp.mean(err * err, axis=-1, keepdims=True), axis=0, keepdims=True)
        part = jnp.broadcast_to(part, (1, LANE))

        @pl.when(i == 0)
        def _():
            l_ref[...] = part

        @pl.when(i > 0)
        def _():
            l_ref[...] += part

    return pl.pallas_call(
        body, grid=(seq // tm,),
        in_specs=[pl.BlockSpec((tm, dm), lambda i: (i, 0))] * 2,
        out_specs=[pl.BlockSpec((1, LANE), lambda i: (0, 0)), pl.BlockSpec((tm, dm), lambda i: (i, 0))],
        out_shape=[jax.ShapeDtypeStruct((1, LANE), F32), jax.ShapeDtypeStruct((seq, dm), F32)],
        compiler_params=_cparams(("arbitrary",)), name="loss_and_grad",
    )(y, target)


def add2(a, b, name):
    r, c = a.shape
    rows = min(r, 512)

    def body(a_ref, b_ref, o_ref):
        o_ref[...] = a_ref[...] + b_ref[...]

    spec = pl.BlockSpec((rows, c), lambda i: (i, 0))
    return pl.pallas_call(body, grid=(r // rows,), in_specs=[spec, spec], out_specs=spec,
                          out_shape=jax.ShapeDtypeStruct((r, c), F32), compiler_params=_cparams(("arbitrary",)),
                          name=name)(a, b)


def adamw(w, g, m, v, rows, name):
    r, c = w.shape

    def body(w_ref, g_ref, m_ref, v_ref, d_ref, nm_ref, nv_ref):
        gv = g_ref[...]
        nm = ADAM_B1 * m_ref[...] + (1.0 - ADAM_B1) * gv
        nv = ADAM_B2 * v_ref[...] + (1.0 - ADAM_B2) * (gv * gv)
        m_hat = nm / (1.0 - ADAM_B1 ** ADAM_STEP)
        v_hat = nv / (1.0 - ADAM_B2 ** ADAM_STEP)
        d_ref[...] = -ADAM_LR * (m_hat / (jnp.sqrt(v_hat) + ADAM_EPS) + ADAM_WD * w_ref[...])
        nm_ref[...] = nm
        nv_ref[...] = nv

    spec = pl.BlockSpec((rows, c), lambda i: (i, 0))
    return pl.pallas_call(
        body, grid=(r // rows,), in_specs=[spec] * 4, out_specs=[spec] * 3,
        out_shape=[jax.ShapeDtypeStruct((r, c), F32)] * 3,
        compiler_params=_cparams(("arbitrary",)), name=name,
    )(w, g, m, v)


def sum_layers(recvs, out_rows, out_cols, rows, name):
    depth = len(recvs)
    n, _, c_in = recvs[0].shape

    def body(*refs):
        o_ref = refs[depth]
        for l in range(depth):
            @pl.when(pl.program_id(0) == l)
            def _():
                acc = refs[l][0].astype(F32)
                for j in range(1, n):
                    acc = acc + refs[l][j].astype(F32)
                o_ref[0] = acc[:, :out_cols]

    return pl.pallas_call(
        body, grid=(depth, out_rows // rows),
        in_specs=[pl.BlockSpec((n, rows, c_in), lambda ll, i, l=l: (0, jnp.where(ll == l, i, 0), 0))
                  for l in range(depth)],
        out_specs=pl.BlockSpec((1, rows, out_cols), lambda ll, i: (ll, i, 0)),
        out_shape=jax.ShapeDtypeStruct((depth, out_rows, out_cols), F32),
        compiler_params=_cparams(("arbitrary", "arbitrary")), name=name,
    )(*recvs)


def _heads(t, n):
    return t.reshape(t.shape[0], n, HEAD).transpose(1, 0, 2)


def _unheads(t):
    return t.transpose(1, 0, 2).reshape(t.shape[1], t.shape[0] * HEAD)


def _mkv_heads(mkv):
    mw = N_MEMH * HEAD
    return _heads(mkv[:, :mw], N_MEMH), _heads(mkv[:, mw:], N_MEMH)


def _mix_args(sv, tabs, sw):
    mk, mv = _mkv_heads(sv["mkv"])
    ct, st = tabs
    conv_args = [(sv["a"], HALO, True), (sv["gate"], HALO, True), (sw["conv_w"], PARAM, True),
                 (sw["conv_b"], PARAM, True), (sw["conv_ln_g"], PARAM, True), (sw["conv_ln_b"], PARAM, True)]
    swa_args = [(sv["q"], TILE, True), (sv["k"], HALO, True), (sv["v"], HALO, True), (ct, HALO, False),
                (st, HALO, False), (sw["swa_q_norm"], PARAM, True), (sw["swa_k_norm"], PARAM, True),
                (sw["swa_sinks"], PARAM, True)]
    mem_args = [(sv["qm"], TILE, True), (mk, PARAM, True), (mv, PARAM, True), (sw["mem_q_norm"], PARAM, True),
                (sw["mem_k_norm"], PARAM, True)]
    return conv_args, swa_args, mem_args


def _proj_args(h1, sw, bw):
    return [(h1, TILE, True), (sw["mix_norm"], PARAM, True), (bw["w_in"], PARAM, True)]


def _join_args(sv, bw):
    return [(sv["h1"], TILE, True), (sv["y_conv"], TILE, True), (sv["y_swa"], TILE, True), (sv["y_mem"], TILE, True),
            (bw["w_out"], PARAM, True)]


def layer_fwd(x, mem, tabs, bw, sw, tm, l, items1=(), items2=()):
    seq, dm = x.shape
    tag = f"_l{l}"
    sv = dict(x=x)
    sv["h1"], got1 = ffn_fwd(x, sw["ffn1_norm"], bw["ffn1_w1"], bw["ffn1_w3"], bw["ffn1_w2"], tm, "ffn1_fwd" + tag,
                             items1)
    split_outs = [((seq, CONV_CH), F32), ((seq, CONV_CH), F32), ((N_Q, seq, HEAD), F32), ((N_KV, seq, HEAD), F32),
                  ((N_KV, seq, HEAD), F32), ((N_MEMH, seq, HEAD), F32)]
    sv["a"], sv["gate"], sv["q"], sv["k"], sv["v"], sv["qm"] = seq_fwd(
        f_proj_split, _proj_args(sv["h1"], sw, bw), split_outs, tm, tm, "proj_in_fwd" + tag)
    ml = mem.shape[0]
    (sv["mkv"],) = seq_fwd(f_proj_in, [(mem, TILE, False), (sw["mem_norm"], PARAM, True), (bw["w_mem_kv"], PARAM, True)],
                           [((ml, bw["w_mem_kv"].shape[1]), F32)], ml, ml, "mem_kv_fwd" + tag)
    conv_args, swa_args, mem_args = _mix_args(sv, tabs, sw)
    (sv["y_conv"],) = seq_fwd(f_conv, conv_args, [((seq, CONV_CH), F32)], tm, 32, "conv_fwd" + tag)
    (sv["y_swa"],) = seq_fwd(f_swa, swa_args, [((N_Q, seq, HEAD), F32)], tm, BLOCK, "swa_fwd" + tag)
    (sv["y_mem"],) = seq_fwd(f_mem, mem_args, [((N_MEMH, seq, HEAD), F32)], tm, tm, "mem_attn_fwd" + tag)
    (sv["h2"],) = seq_fwd(f_out_join, _join_args(sv, bw), [((seq, dm), F32)], tm, tm, "out_proj_fwd" + tag)
    sv["h3"], got2 = ffn_fwd(sv["h2"], sw["ffn2_norm"], bw["ffn2_w1"], bw["ffn2_w3"], bw["ffn2_w2"], tm,
                             "ffn2_fwd" + tag, items2)
    (xo,) = seq_fwd(f_rms, [(sv["h3"], TILE, True), (sw["final_norm"], PARAM, True)], [((seq, dm), F32)], tm, tm,
                    "final_norm_fwd" + tag)
    return xo, sv, got1, got2


def layer_bwd(dxo, sv, mem, tabs, bw, sw, tm, l, items=()):
    tag = f"_l{l}"
    gb, gs = {}, {}
    dh3, gs["final_norm"] = seq_bwd(f_rms, [(sv["h3"], TILE, True), (sw["final_norm"], PARAM, True)], [dxo], tm, tm,
                                    "final_norm_bwd" + tag)
    parts2, gb["ffn2_w1"], gb["ffn2_w3"], gb["ffn2_w2"], got = ffn_bwd_chunks(
        sv["h2"], sw["ffn2_norm"], dh3, bw["ffn2_w1"], bw["ffn2_w3"], bw["ffn2_w2"], tm, "ffn2_bwd" + tag, items)
    dh2, gs["ffn2_norm"] = ffn_bwd_norm(sv["h2"], sw["ffn2_norm"], parts2, dh3, tm, "ffn2_norm_bwd" + tag)
    dh1_a, dy_conv, dy_swa, dy_mem, gb["w_out"] = seq_bwd(f_out_join, _join_args(sv, bw), [dh2], tm // 2, tm // 2,
                                                           "out_proj_bwd" + tag)
    conv_args, swa_args, mem_args = _mix_args(sv, tabs, sw)
    da, dgate, gs["conv_w"], gs["conv_b"], gs["conv_ln_g"], gs["conv_ln_b"] = seq_bwd(
        f_conv, conv_args, [dy_conv], tm, 32, "conv_bwd" + tag)
    dq, dk, dv, gs["swa_q_norm"], gs["swa_k_norm"], gs["swa_sinks"] = seq_bwd(
        f_swa, swa_args, [dy_swa], tm, BLOCK, "swa_bwd" + tag)
    dqm, dmk, dmv, gs["mem_q_norm"], gs["mem_k_norm"] = seq_bwd(f_mem, mem_args, [dy_mem], tm, tm, "mem_attn_bwd" + tag)
    dmkv = jnp.concatenate([_unheads(dmk), _unheads(dmv)], axis=-1)
    ml = mem.shape[0]
    gs["mem_norm"], gb["w_mem_kv"] = seq_bwd(
        f_proj_in, [(mem, TILE, False), (sw["mem_norm"], PARAM, True), (bw["w_mem_kv"], PARAM, True)], [dmkv], ml, ml,
        "mem_kv_bwd" + tag)
    dh1_b, gs["mix_norm"], gb["w_in"] = seq_bwd(f_proj_split, _proj_args(sv["h1"], sw, bw),
                                                 [da, dgate, dq, dk, dv, dqm], tm // 2, tm // 2, "proj_in_bwd" + tag)
    dh1 = add2(dh1_a, dh1_b, "add_dh1" + tag)
    parts1, gb["ffn1_w1"], gb["ffn1_w3"], gb["ffn1_w2"], _ = ffn_bwd_chunks(
        sv["x"], sw["ffn1_norm"], dh1, bw["ffn1_w1"], bw["ffn1_w3"], bw["ffn1_w2"], tm, "ffn1_bwd" + tag)
    dx, gs["ffn1_norm"] = ffn_bwd_norm(sv["x"], sw["ffn1_norm"], parts1, dh1, tm, "ffn1_norm_bwd" + tag)
    return dx, gb, gs, got


FFN_UP = ("ffn1_w1", "ffn1_w3", "ffn2_w1", "ffn2_w3")
FFN_DOWN = ("ffn1_w2", "ffn2_w2")
ROW_SHARDED = ("w_mem_kv", "w_out")
GATHER_1 = ("ffn1_w1", "ffn1_w3", "ffn1_w2", "w_in")
GATHER_2 = ("ffn2_w1", "ffn2_w3", "ffn2_w2", "w_out", "w_mem_kv")
BIG = GATHER_1 + GATHER_2
SMALL = ("ffn1_norm", "mix_norm", "conv_b", "conv_ln_g", "conv_ln_b", "swa_q_norm", "swa_k_norm", "swa_sinks",
         "mem_norm", "mem_q_norm", "mem_k_norm", "ffn2_norm", "final_norm")
WEIGHTS = ("ffn1_norm", "ffn1_w1", "ffn1_w3", "ffn1_w2", "mix_norm", "w_in", "conv_w", "conv_b", "conv_ln_g",
           "conv_ln_b", "swa_q_norm", "swa_k_norm", "swa_sinks", "mem_norm", "w_mem_kv", "mem_q_norm", "mem_k_norm",
           "w_out", "ffn2_norm", "ffn2_w1", "ffn2_w3", "ffn2_w2", "final_norm")


def _round_up(n, m):
    return -(-n // m) * m


def _row_block(rows, target, mult=8):
    best = rows
    for cand in range(mult, min(rows, target) + 1, mult):
        if rows % cand == 0:
            best = cand
    return best


def _pad_rows(flat, cols, mult):
    n = flat.shape[0]
    rows = _round_up(-(-n // cols), mult)
    return jnp.pad(flat, (0, rows * cols - n)).reshape(rows, cols)


def send_form(w):
    out = {}
    for n in BIG:
        t = w[n].astype(BF)
        if n in FFN_UP:
            t = jnp.pad(t, ((0, 0), (0, 0), (0, _round_up(t.shape[2], LANE) - t.shape[2])))
        elif n in FFN_DOWN:
            t = jnp.pad(t, ((0, 0), (0, _round_up(t.shape[1], LANE) - t.shape[1]), (0, 0)))
        out[n] = t
    return out


def gather_items(sent, l, names):
    return [gather_cols_item(sent[n][l]) if n in FFN_UP else gather_item(sent[n][l]) for n in names]


def kernel_layouts(got, names):
    out = {}
    for n, t in zip(names, got):
        if n in FFN_DOWN:
            t = t.reshape(N_DEV // 2, 2 * t.shape[1], t.shape[2])
        elif n == "w_in":
            t = t.transpose(1, 0, 2).reshape(t.shape[1], N_DEV * t.shape[2])
        elif n in ROW_SHARDED:
            t = t.reshape(N_DEV * t.shape[1], t.shape[2])
        out[n] = t
    return out


def scatter_items(gb):
    items = []
    for n in BIG:
        t = gb[n]
        if n in FFN_UP:
            items.append(scatter_cols_item(t))
            continue
        if n in FFN_DOWN:
            t = t.reshape(N_DEV, t.shape[1] // 2, t.shape[2])
        elif n == "w_in":
            t = t.reshape(t.shape[0], N_DEV, t.shape[1] // N_DEV).transpose(1, 0, 2).astype(BF)
        else:
            t = t.reshape(N_DEV, t.shape[0] // N_DEV, t.shape[1]).astype(BF)
        items.append(scatter_item(t))
    return items


def small_layer_params(w, l):
    sw = {n: w[n][l][None, :] for n in SMALL if n != "swa_sinks"}
    sw["swa_sinks"] = jnp.pad(w["swa_sinks"][l], (0, LANE - N_Q))[None, :]
    sw["conv_w"] = jnp.pad(w["conv_w_full"][l], ((0, 1), (0, 0)))
    return sw


def kernel(x, mem, positions, ffn1_norm, ffn1_w1, ffn1_w3, ffn1_w2, mix_norm, w_in, conv_w, conv_b, conv_ln_g, conv_ln_b, swa_q_norm, swa_k_norm, swa_sinks, mem_norm, w_mem_kv, mem_q_norm, mem_k_norm, w_out, ffn2_norm, ffn2_w1, ffn2_w3, ffn2_w2, final_norm, loss_target, m_ffn1_norm, m_ffn1_w1, m_ffn1_w3, m_ffn1_w2, m_mix_norm, m_w_in, m_conv_w, m_conv_b, m_conv_ln_g, m_conv_ln_b, m_swa_q_norm, m_swa_k_norm, m_swa_sinks, m_mem_norm, m_w_mem_kv, m_mem_q_norm, m_mem_k_norm, m_w_out, m_ffn2_norm, m_ffn2_w1, m_ffn2_w3, m_ffn2_w2, m_final_norm, v_ffn1_norm, v_ffn1_w1, v_ffn1_w3, v_ffn1_w2, v_mix_norm, v_w_in, v_conv_w, v_conv_b, v_conv_ln_g, v_conv_ln_b, v_swa_q_norm, v_swa_k_norm, v_swa_sinks, v_mem_norm, v_w_mem_kv, v_mem_q_norm, v_mem_k_norm, v_w_out, v_ffn2_norm, v_ffn2_w1, v_ffn2_w3, v_ffn2_w2, v_final_norm):
    loc = locals()
    w = {n: loc[n] for n in WEIGHTS}
    m = {n: loc["m_" + n] for n in WEIGHTS}
    v = {n: loc["v_" + n] for n in WEIGHTS}
    depth = ffn1_norm.shape[0]
    seq = x.shape[1]
    tm = min(512, seq)
    me = 4 * lax.axis_index("x") + 2 * lax.axis_index("y") + lax.axis_index("c")
    xs, mems, target = x[0], mem[0], loss_target[0]

    sent = send_form(w)
    cw = conv_w.shape[2]
    conv_rows = _pad_rows(conv_w.reshape(-1), LANE, 8)
    got = comm_call(gather_items(sent, 0, BIG) + [gather_item(conv_rows)], "gather_l0")
    bw = kernel_layouts(got[:-1], BIG)
    conv_full = got[-1].reshape(N_DEV, -1)[:, :conv_w.size].reshape((N_DEV,) + conv_w.shape)
    small = {n: w[n] for n in SMALL}
    small["conv_w_full"] = conv_full.transpose(1, 2, 0, 3).reshape(depth, CONV_W, N_DEV * cw)

    inv = ROPE_THETA ** (-jnp.arange(0, HEAD, 2, dtype=F32) / HEAD)
    tabs = rope_tables(positions[0].reshape(-1, 1), jnp.concatenate([inv, inv])[None, :], tm)
    saved, bws = [], []
    h = xs
    for l in range(depth):
        nxt = l + 1 < depth
        items1 = gather_items(sent, l + 1, GATHER_1) if nxt else ()
        items2 = gather_items(sent, l + 1, GATHER_2) if nxt else ()
        h, sv, got1, got2 = layer_fwd(h, mems, tabs, bw, small_layer_params(small, l), tm, l, items1, items2)
        saved.append(sv)
        bws.append(bw)
        if nxt:
            bw = {**kernel_layouts(got1, GATHER_1), **kernel_layouts(got2, GATHER_2)}
    loss_sum, dh = loss_and_grad(h, target, tm)
    loss = lax.psum(loss_sum[0, 0], ("x", "y", "c"))

    recv = [None] * depth
    gss = [None] * depth
    items = ()
    for l in reversed(range(depth)):
        dh, gb, gss[l], got = layer_bwd(dh, saved[l], mems, tabs, bws[l], small_layer_params(small, l), tm, l, items)
        if items:
            recv[l + 1] = got
        items = scatter_items(gb)
    recv[0] = comm_call(items, "grad_exchange_l0")

    g_all = {}
    for t, n in enumerate(BIG):
        a, b = w[n].shape[1:]
        blocks = [recv[l][t] for l in range(depth)]
        if n in FFN_DOWN:
            rows = _row_block(a, 256, BF16_ROWS)
        else:
            rows = _row_block(a, 256)
        g_all[n] = sum_layers(blocks, a, b, rows, "grad_sum_" + n)

    small_names = SMALL + ("conv_w",)
    sizes = {n: (w[n].shape[1] if n != "conv_w" else CONV_W * N_DEV * cw) for n in small_names}
    flat = []
    for l in range(depth):
        for n in small_names:
            t = gss[l][n]
            if n == "swa_sinks":
                t = t[:, :N_Q]
            elif n == "conv_w":
                t = t[:CONV_W]
            flat.append(t.reshape(-1))
    flat = _pad_rows(jnp.concatenate(flat), LANE, 8)
    (small_got,) = comm_call([gather_item(flat)], "small_grad_gather")
    summed = sum_layers([small_got], flat.shape[0], LANE, flat.shape[0], "small_grad_sum").reshape(-1)
    g_small = {n: [] for n in small_names}
    o = 0
    for l in range(depth):
        for n in small_names:
            g_small[n].append(summed[o:o + sizes[n]])
            o += sizes[n]
    for n in SMALL:
        g_all[n] = jnp.stack(g_small[n])
    gcw = jnp.stack(g_small["conv_w"]).reshape(depth, CONV_W, N_DEV, cw)
    g_all["conv_w"] = lax.dynamic_slice(gcw, (0, 0, me, 0), (depth, CONV_W, 1, cw)).reshape(depth, CONV_W, cw)

    outs = {"delta": {}, "new_m": {}, "new_v": {}}
    for n in BIG:
        shape = w[n].shape
        two_d = (shape[0] * shape[1], shape[2])
        res = adamw(w[n].reshape(two_d), g_all[n].reshape(two_d), m[n].reshape(two_d), v[n].reshape(two_d),
                    _row_block(two_d[0], 512), "adamw_" + n)
        for key, arr in zip(("delta", "new_m", "new_v"), res):
            outs[key][n] = arr.reshape(shape)

    def packed_small(t):
        return _pad_rows(jnp.concatenate([t[n].reshape(-1) for n in small_names]), LANE, 8)

    ps = [packed_small(t) for t in (w, g_all, m, v)]
    res = adamw(*ps, ps[0].shape[0], "adamw_small")
    for key, arr in zip(("delta", "new_m", "new_v"), res):
        fl, o = arr.reshape(-1), 0
        for n in small_names:
            outs[key][n] = fl[o:o + w[n].size].reshape(w[n].shape)
            o += w[n].size

    return (loss, dh[None], *[g_all[n] for n in WEIGHTS], *[outs["delta"][n] for n in WEIGHTS],
            *[outs["new_m"][n] for n in WEIGHTS], *[outs["new_v"][n] for n in WEIGHTS])
```

```python
import jax
import jax.numpy as jnp
from jax import lax
from jax.experimental import pallas as pl
from jax.experimental.pallas import tpu as pltpu

F32 = jnp.float32
BF = jnp.bfloat16
EPS = 1e-6
HEAD = 64
N_Q, N_KV, N_MEMH = 6, 2, 4
CONV_CH, CONV_W = 384, 31
BLOCK = 128
ROPE_THETA = 10000.0
N_DEV = 8
V7X_VMEM_LIMIT = 56 * 1024 * 1024
LANE = 128
BF16_ROWS = 16

ADAM_LR, ADAM_B1, ADAM_B2, ADAM_EPS, ADAM_WD, ADAM_STEP = 0.001, 0.9, 0.999, 1e-08, 0.01, 10

TILE, HALO, PARAM = "tile", "halo", "param"
MESH = pl.DeviceIdType.MESH
ANY = pl.BlockSpec(memory_space=pl.ANY)


def _cparams(sem=None):
    kw = dict(vmem_limit_bytes=V7X_VMEM_LIMIT)
    if sem is not None:
        kw["dimension_semantics"] = sem
    return pltpu.CompilerParams(**kw)


def _dot(a, b, dims):
    return lax.dot_general(a, b, (dims, ((), ())), preferred_element_type=F32)


_NN, _NT, _TN = ((1,), (0,)), ((1,), (1,)), ((0,), (0,))


@jax.custom_vjp
def mm(a, b):
    return _dot(a.astype(BF), b.astype(BF), _NN)


def _mm_fwd(a, b):
    return mm(a, b), (a, b)


def _mm_bwd(res, g):
    a, b = res
    gb = g.astype(BF)
    return _dot(gb, b.astype(BF), _NT), _dot(a.astype(BF), gb, _TN)


mm.defvjp(_mm_fwd, _mm_bwd)


@jax.custom_vjp
def mm_nt(a, b):
    return _dot(a.astype(BF), b.astype(BF), _NT)


def _mm_nt_fwd(a, b):
    return mm_nt(a, b), (a, b)


def _mm_nt_bwd(res, g):
    a, b = res
    gb = g.astype(BF)
    return _dot(gb, b.astype(BF), _NN), _dot(gb, a.astype(BF), _TN)


mm_nt.defvjp(_mm_nt_fwd, _mm_nt_bwd)


def rms(x, g):
    return x * lax.rsqrt(jnp.mean(x * x, axis=-1, keepdims=True) + EPS) * g


def _swap_matrix():
    i = lax.broadcasted_iota(jnp.int32, (HEAD, HEAD), 0)
    j = lax.broadcasted_iota(jnp.int32, (HEAD, HEAD), 1)
    src = jnp.where(j < HEAD // 2, j + HEAD // 2, j - HEAD // 2)
    return (i == src).astype(F32)


def rope(x, c, s, swap):
    xs = jnp.dot(x, swap, precision=lax.Precision.HIGHEST, preferred_element_type=F32)
    return x * c + xs * s


SUBLANES = 8


def _row_shifts(t, rows):
    return [t] + [t[b:b + rows + 24] for b in range(1, SUBLANES)]


def _window(shifts, offset, rows):
    return shifts[offset % SUBLANES][offset - offset % SUBLANES:offset - offset % SUBLANES + rows]


def _conv_taps(gs, w, tm):
    acc = w[0:1, :] * _window(gs, 2, tm)
    for j in range(1, CONV_W):
        acc = acc + w[j:j + 1, :] * _window(gs, 2 + j, tm)
    return acc


@jax.custom_vjp
def causal_dw_conv(g, w):
    tm = g.shape[0] - 32
    return _conv_taps(_row_shifts(g, tm), w, tm)


def _conv_fwd(g, w):
    tm = g.shape[0] - 32
    gs = _row_shifts(g, tm)
    return _conv_taps(gs, w, tm), (gs, w)


def _conv_bwd(res, dc):
    gs, w = res
    tm, ch = dc.shape
    z = jnp.zeros((32, ch), F32)
    ds = _row_shifts(jnp.concatenate([z, dc, z], axis=0), tm + 32)
    rows = lax.broadcasted_iota(jnp.int32, (32, 1), 0)
    dg = w[0:1, :] * _window(ds, 30, tm + 32)
    dw = jnp.where(rows == 0, jnp.sum(dc * _window(gs, 2, tm), axis=0, keepdims=True), 0.0)
    for j in range(1, CONV_W):
        dg = dg + w[j:j + 1, :] * _window(ds, 30 - j, tm + 32)
        dw = dw + jnp.where(rows == j, jnp.sum(dc * _window(gs, 2 + j, tm), axis=0, keepdims=True), 0.0)
    return dg, dw


causal_dw_conv.defvjp(_conv_fwd, _conv_bwd)


def f_rms(first, x, g):
    return (rms(x, g),)


def f_proj_in(first, h, g, w):
    return (mm(rms(h, g), w),)


def _to_heads(t, n):
    return jnp.stack([t[:, h * HEAD:(h + 1) * HEAD] for h in range(n)])


def f_proj_split(first, h, g, w):
    p = mm(rms(h, g), w)
    o, outs = 0, []
    for width, n in ((CONV_CH, 0), (CONV_CH, 0), (N_Q * HEAD, N_Q), (N_KV * HEAD, N_KV), (N_KV * HEAD, N_KV),
                     (N_MEMH * HEAD, N_MEMH)):
        t = p[:, o:o + width]
        outs.append(_to_heads(t, n) if n else t)
        o += width
    return tuple(outs)


def f_out_join(first, h, y_conv, y_swa, y_mem, w):
    y = jnp.concatenate([y_conv] + [y_swa[i] for i in range(N_Q)] + [y_mem[i] for i in range(N_MEMH)], axis=1)
    return (h + mm(y, w),)


def f_conv(first, a, gate, w, b, lg, lb):
    keep = 1.0 - first
    av = jnp.concatenate([a[0] * keep, a[1]], axis=0)
    gv = jnp.concatenate([gate[0], gate[1]], axis=0)
    glu = av * jax.nn.sigmoid(gv)
    c = causal_dw_conv(glu, w) + b
    mu = jnp.mean(c, axis=-1, keepdims=True)
    var = jnp.mean(jnp.square(c - mu), axis=-1, keepdims=True)
    z = (c - mu) * lax.rsqrt(var + EPS) * lg + lb
    return (z * jax.nn.sigmoid(z),)


def _softmax_with_extra(s, extra):
    m = jnp.max(s, axis=-1, keepdims=True)
    if extra is not None:
        m = jnp.maximum(m, extra)
    m = lax.stop_gradient(m)
    e = jnp.exp(s - m)
    den = jnp.sum(e, axis=-1, keepdims=True)
    if extra is not None:
        den = den + jnp.exp(extra - m)
    return e / den


def f_swa(first, q, k, v, ct, st, qn, kn, sinks):
    tm = q.shape[1]
    nb = tm // BLOCK
    g = N_Q // N_KV
    swap = _swap_matrix()
    c_all = jnp.concatenate([ct[0], ct[1]], axis=0)
    s_all = jnp.concatenate([st[0], st[1]], axis=0)
    qi = lax.broadcasted_iota(jnp.int32, (g * BLOCK, 2 * BLOCK), 0)
    kj = lax.broadcasted_iota(jnp.int32, (g * BLOCK, 2 * BLOCK), 1)
    qpos = jnp.where(qi >= 2 * BLOCK, qi - 2 * BLOCK, jnp.where(qi >= BLOCK, qi - BLOCK, qi)) + BLOCK
    rel = qpos - kj
    band = (rel >= 0) & (rel < BLOCK)
    band_first = band & ((kj >= BLOCK) | (first < 0.5))
    lane = lax.broadcasted_iota(jnp.int32, (g * BLOCK, LANE), 1)
    hrow = lax.broadcasted_iota(jnp.int32, (g * BLOCK, LANE), 0)
    head_in_group = jnp.where(hrow >= 2 * BLOCK, 2, jnp.where(hrow >= BLOCK, 1, 0))
    qr = [rope(rms(q[h], qn), ct[1], st[1], swap) for h in range(N_Q)]
    outs = [[None] * nb for _ in range(N_Q)]
    for hk in range(N_KV):
        kk = rope(rms(jnp.concatenate([k[0][hk], k[1][hk]], axis=0), kn), c_all, s_all, swap)
        vv = jnp.concatenate([v[0][hk], v[1][hk]], axis=0)
        sel = (lane == head_in_group + hk * g).astype(F32)
        sink_col = jnp.sum(sel * sinks, axis=1, keepdims=True)
        for j in range(nb):
            keys = kk[j * BLOCK:(j + 2) * BLOCK]
            vals = vv[j * BLOCK:(j + 2) * BLOCK]
            qs = jnp.concatenate([qr[hk * g + gg][j * BLOCK:(j + 1) * BLOCK] for gg in range(g)], axis=0)
            s = mm_nt(qs, keys) * (HEAD ** -0.5)
            s = jnp.where(band_first if j == 0 else band, s, -1e30)
            o = mm(_softmax_with_extra(s, sink_col), vals)
            for gg in range(g):
                outs[hk * g + gg][j] = o[gg * BLOCK:(gg + 1) * BLOCK]
    return (jnp.stack([jnp.concatenate(outs[h], axis=0) for h in range(N_Q)]),)


def f_mem(first, qm, mk, mv, qn, kn):
    outs = []
    for h in range(N_MEMH):
        qh = rms(qm[h], qn)
        kh = rms(mk[h], kn)
        s = mm_nt(qh, kh) * (HEAD ** -0.5)
        outs.append(mm(_softmax_with_extra(s, None), mv[h]))
    return (jnp.stack(outs),)


def _seq_len(arr):
    return arr.shape[0] if arr.ndim == 2 else arr.shape[1]


def _tile_spec(arr, rows, imap):
    if arr.ndim == 2:
        return pl.BlockSpec((rows, arr.shape[1]), lambda i: (imap(i), 0))
    return pl.BlockSpec((arr.shape[0], rows, arr.shape[2]), lambda i: (0, imap(i), 0))


def _full_spec(arr):
    nd = arr.ndim
    return pl.BlockSpec(arr.shape, lambda i: (0,) * nd)


def _in_specs(args, tm, hl, nt):
    specs = []
    ratio = tm // hl
    cur = lambda i: jnp.minimum(i, nt - 1)
    prev = lambda i: jnp.maximum(jnp.minimum(i, nt - 1) * ratio - 1, 0)
    for arr, kind, _ in args:
        if kind == TILE:
            specs.append(_tile_spec(arr, tm, cur))
        elif kind == HALO:
            specs.append(_tile_spec(arr, hl, prev))
            specs.append(_tile_spec(arr, tm, cur))
        else:
            specs.append(_full_spec(arr))
    return specs


def _operands(args):
    ops = []
    for arr, kind, _ in args:
        ops.append(arr)
        if kind == HALO:
            ops.append(arr)
    return ops


def _load_values(args, refs):
    vals, k = [], 0
    for arr, kind, _ in args:
        if kind == HALO:
            vals.append((refs[k][...].astype(F32), refs[k + 1][...].astype(F32)))
            k += 2
        else:
            vals.append(refs[k][...].astype(F32))
            k += 1
    return vals


def seq_fwd(f, args, outs, tm, hl, name):
    out_shape = [jax.ShapeDtypeStruct(s, d) for s, d in outs]
    nt = _seq_len(out_shape[0]) // tm
    n_in = len(_operands(args))

    def body(*refs):
        first = (pl.program_id(0) == 0).astype(F32)
        res = f(first, *_load_values(args, refs[:n_in]))
        for r, o in zip(refs[n_in:], res):
            r[...] = o.astype(r.dtype)

    cur = lambda i: i
    return pl.pallas_call(
        body, grid=(nt,), in_specs=_in_specs(args, tm, hl, nt),
        out_specs=[_tile_spec(o, tm, cur) for o in out_shape], out_shape=out_shape,
        compiler_params=_cparams(("arbitrary",)), name=name,
    )(*_operands(args))


def seq_bwd(f, args, douts, tm, hl, name):
    seq = _seq_len(douts[0])
    nt = seq // tm
    lag = any(kind == HALO and diff for _, kind, diff in args)
    steps = nt + 1 if lag else nt
    n_in = len(_operands(args))
    n_do = len(douts)
    dargs = [(arr, kind) for arr, kind, diff in args if diff]

    def body(*refs):
        in_refs = refs[:n_in]
        do_refs = refs[n_in:n_in + n_do]
        g_refs = refs[n_in + n_do:n_in + n_do + len(dargs)]
        carries = refs[n_in + n_do + len(dargs):]
        i = pl.program_id(0)
        first = (i == 0).astype(F32)

        def compute():
            vals = _load_values(args, in_refs)
            dvals = [v for v, (_, _, diff) in zip(vals, args) if diff]

            def fd(*dv):
                it = iter(dv)
                return f(first, *[next(it) if diff else v for v, (_, _, diff) in zip(vals, args)])

            _, vjp = jax.vjp(fd, *dvals)
            grads = vjp(tuple(r[...].astype(F32) for r in do_refs))
            c = 0
            for gref, gval, (arr, kind) in zip(g_refs, grads, dargs):
                if kind == TILE:
                    gref[...] = gval.astype(gref.dtype)
                elif kind == PARAM:
                    @pl.when(i == 0)
                    def _():
                        gref[...] = gval

                    @pl.when(i > 0)
                    def _():
                        gref[...] += gval
                else:
                    carry = carries[c]
                    c += 1
                    g_prev, g_cur = gval

                    @pl.when(i > 0)
                    def _():
                        gref[...] = carry[...]
                        if arr.ndim == 2:
                            gref[tm - hl:tm, :] += g_prev
                        else:
                            gref[:, tm - hl:tm, :] += g_prev

                    carry[...] = g_cur

        if lag:
            pl.when(i < nt)(compute)

            @pl.when(i == nt)
            def _():
                c = 0
                for gref, (arr, kind) in zip(g_refs, dargs):
                    if kind == HALO:
                        gref[...] = carries[c][...]
                        c += 1
        else:
            compute()

    cur = lambda i: jnp.minimum(i, nt - 1)
    lagged = lambda i: jnp.maximum(i - 1, 0)
    out_shape, out_specs, scratch = [], [], []
    for arr, kind in dargs:
        out_shape.append(jax.ShapeDtypeStruct(arr.shape, F32))
        if kind == PARAM:
            out_specs.append(_full_spec(arr))
        else:
            out_specs.append(_tile_spec(arr, tm, lagged if kind == HALO else cur))
            if kind == HALO:
                blk = (tm, arr.shape[1]) if arr.ndim == 2 else (arr.shape[0], tm, arr.shape[2])
                scratch.append(pltpu.VMEM(blk, F32))
    in_specs = _in_specs(args, tm, hl, nt) + [_tile_spec(d, tm, cur) for d in douts]
    return pl.pallas_call(
        body, grid=(steps,), in_specs=in_specs, out_specs=out_specs, out_shape=out_shape,
        scratch_shapes=scratch, compiler_params=_cparams(("arbitrary",)), name=name,
    )(*_operands(args), *douts)


class Item:
    def __init__(self, operand, out_shape, src, dst):
        self.operand, self.out_shape, self.src, self.dst = operand, out_shape, src, dst


def _dev(p):
    return 4 * p[0] + 2 * p[1] + p[2]


def gather_item(shard):
    return Item(shard, jax.ShapeDtypeStruct((N_DEV,) + shard.shape, shard.dtype),
                lambda r, peer: r, lambda r, s: r.at[_dev(s)])


def gather_cols_item(shard):
    d, w = shard.shape
    return Item(shard, jax.ShapeDtypeStruct((N_DEV // 2, d, 2 * w), shard.dtype),
                lambda r, peer: r, lambda r, s: r.at[2 * s[0] + s[1], :, pl.ds(s[2] * w, w)])


def scatter_item(blocks):
    return Item(blocks, jax.ShapeDtypeStruct(blocks.shape, blocks.dtype),
                lambda r, peer: r.at[_dev(peer)], lambda r, s: r.at[_dev(s)])


def scatter_cols_item(full):
    n, d, w2 = full.shape
    w = w2 // 2
    return Item(full, jax.ShapeDtypeStruct((N_DEV, d, w), full.dtype),
                lambda r, peer: r.at[2 * peer[0] + peer[1], :, pl.ds(peer[2] * w, w)], lambda r, s: r.at[_dev(s)])


def _comm_sems(items):
    n = len(items) * (N_DEV - 1)
    return [pltpu.SemaphoreType.DMA((n,)), pltpu.SemaphoreType.DMA((n,)), pltpu.SemaphoreType.DMA((len(items),))]


def _comm_copies(items, in_refs, out_refs, sems, x, y, c):
    send_sems, recv_sems, local_sems = sems
    me = (x, y, c)
    copies = []
    for t, it in enumerate(items):
        copies.append(pltpu.make_async_copy(it.src(in_refs[t], me), it.dst(out_refs[t], me), local_sems.at[t]))
        for k in range(1, N_DEV):
            peer = (1 - x if k & 4 else x, 1 - y if k & 2 else y, 1 - c if k & 1 else c)
            n = t * (N_DEV - 1) + k - 1
            copies.append(pltpu.make_async_remote_copy(
                src_ref=it.src(in_refs[t], peer), dst_ref=it.dst(out_refs[t], me), send_sem=send_sems.at[n],
                recv_sem=recv_sems.at[n], device_id=(peer[0], peer[1], jnp.int32(peer[2])), device_id_type=MESH))
    return copies


def comm_start(items, in_refs, out_refs, sems):
    x, y, c = lax.axis_index("x"), lax.axis_index("y"), lax.axis_index("c")
    for cv in (0, 1):
        @pl.when(c == cv)
        def _():
            for cp in _comm_copies(items, in_refs, out_refs, sems, x, y, cv):
                cp.start()


def comm_wait(items, in_refs, out_refs, sems):
    x, y, c = lax.axis_index("x"), lax.axis_index("y"), lax.axis_index("c")
    for cv in (0, 1):
        @pl.when(c == cv)
        def _():
            for cp in _comm_copies(items, in_refs, out_refs, sems, x, y, cv):
                cp.wait()


def comm_call(items, name):
    n = len(items)

    def body(*refs):
        in_refs, out_refs, sems = refs[:n], refs[n:2 * n], refs[2 * n:]
        comm_start(items, in_refs, out_refs, sems)
        comm_wait(items, in_refs, out_refs, sems)

    return pl.pallas_call(
        body, in_specs=[ANY] * n, out_specs=[ANY] * n, out_shape=[it.out_shape for it in items],
        scratch_shapes=_comm_sems(items), name=name,
    )(*[it.operand for it in items])


def ffn_fwd(x, g, w1, w3, w2, tm, name, items=()):
    seq, dm = x.shape
    nc, _, fc = w1.shape
    nt = seq // tm
    n = len(items)

    def body(*refs):
        x_ref, g_ref, w1_ref, w3_ref, w2_ref = refs[:5]
        c_in, h_ref, c_out = refs[5:5 + n], refs[5 + n], refs[6 + n:6 + 2 * n]
        xn_s, acc_s = refs[6 + 2 * n:8 + 2 * n]
        sems = refs[8 + 2 * n:]
        i, c = pl.program_id(0), pl.program_id(1)

        if n:
            @pl.when((i == 0) & (c == 0))
            def _():
                comm_start(items, c_in, c_out, sems)

        @pl.when(c == 0)
        def _():
            xn_s[...] = rms(x_ref[...], g_ref[...]).astype(BF)
            acc_s[...] = jnp.zeros_like(acc_s)

        xn = xn_s[...]
        a = _dot(xn, w1_ref[0], _NN)
        b = _dot(xn, w3_ref[0], _NN)
        hid = (a * jax.nn.sigmoid(a)) * b
        acc_s[...] += _dot(hid.astype(BF), w2_ref[0], _NN)

        @pl.when(c == nc - 1)
        def _():
            h_ref[...] = x_ref[...] + 0.5 * acc_s[...]

        if n:
            @pl.when((i == nt - 1) & (c == nc - 1))
            def _():
                comm_wait(items, c_in, c_out, sems)

    res = pl.pallas_call(
        body, grid=(nt, nc),
        in_specs=[pl.BlockSpec((tm, dm), lambda i, c: (i, 0)), pl.BlockSpec((1, dm), lambda i, c: (0, 0)),
                  pl.BlockSpec((1, dm, fc), lambda i, c: (c, 0, 0)), pl.BlockSpec((1, dm, fc), lambda i, c: (c, 0, 0)),
                  pl.BlockSpec((1, fc, dm), lambda i, c: (c, 0, 0))] + [ANY] * n,
        out_specs=[pl.BlockSpec((tm, dm), lambda i, c: (i, 0))] + [ANY] * n,
        out_shape=[jax.ShapeDtypeStruct((seq, dm), F32)] + [it.out_shape for it in items],
        scratch_shapes=[pltpu.VMEM((tm, dm), BF), pltpu.VMEM((tm, dm), F32)] + (_comm_sems(items) if n else []),
        compiler_params=_cparams(("arbitrary", "arbitrary")), name=name,
    )(x, g, w1, w3, w2, *[it.operand for it in items])
    return res[0], list(res[1:])


def ffn_bwd_chunks(x, g, dh, w1, w3, w2, tm, name, items=()):
    seq, dm = x.shape
    nc, _, fc = w1.shape
    nt = seq // tm
    n = len(items)

    def body(*refs):
        x_ref, g_ref, dh_ref, w1_ref, w3_ref, w2_ref = refs[:6]
        c_in = refs[6:6 + n]
        dxn_ref, dw1_ref, dw3_ref, dw2_ref = refs[6 + n:10 + n]
        c_out = refs[10 + n:10 + 2 * n]
        a1_s, a3_s, a2_s = refs[10 + 2 * n:13 + 2 * n]
        sems = refs[13 + 2 * n:]
        c, i = pl.program_id(0), pl.program_id(1)

        if n:
            @pl.when((i == 0) & (c == 0))
            def _():
                comm_start(items, c_in, c_out, sems)

        xn = rms(x_ref[...], g_ref[...]).astype(BF)
        dy = (0.5 * dh_ref[...]).astype(BF)
        w1v, w3v, w2v = w1_ref[0], w3_ref[0], w2_ref[0]
        a = _dot(xn, w1v, _NN)
        b = _dot(xn, w3v, _NN)
        sig = jax.nn.sigmoid(a)
        sa = a * sig
        dhid = _dot(dy, w2v, _NT)
        db = (dhid * sa).astype(BF)
        da = (dhid * b * (sig * (1.0 + a * (1.0 - sig)))).astype(BF)
        dxn_ref[0] = _dot(da, w1v, _NT) + _dot(db, w3v, _NT)
        g1 = _dot(xn, da, _TN)
        g3 = _dot(xn, db, _TN)
        g2 = _dot((sa * b).astype(BF), dy, _TN)

        @pl.when(i == 0)
        def _():
            a1_s[...] = g1
            a3_s[...] = g3
            a2_s[...] = g2

        @pl.when(i > 0)
        def _():
            a1_s[...] += g1
            a3_s[...] += g3
            a2_s[...] += g2

        @pl.when(i == nt - 1)
        def _():
            dw1_ref[0] = a1_s[...].astype(BF)
            dw3_ref[0] = a3_s[...].astype(BF)
            dw2_ref[0] = a2_s[...].astype(BF)

        if n:
            @pl.when((i == nt - 1) & (c == nc - 1))
            def _():
                comm_wait(items, c_in, c_out, sems)

    res = pl.pallas_call(
        body, grid=(nc, nt),
        in_specs=[pl.BlockSpec((tm, dm), lambda c, i: (i, 0)), pl.BlockSpec((1, dm), lambda c, i: (0, 0)),
                  pl.BlockSpec((tm, dm), lambda c, i: (i, 0)),
                  pl.BlockSpec((1, dm, fc), lambda c, i: (c, 0, 0)), pl.BlockSpec((1, dm, fc), lambda c, i: (c, 0, 0)),
                  pl.BlockSpec((1, fc, dm), lambda c, i: (c, 0, 0))] + [ANY] * n,
        out_specs=[pl.BlockSpec((1, tm, dm), lambda c, i: (c, i, 0)),
                   pl.BlockSpec((1, dm, fc), lambda c, i: (c, 0, 0)), pl.BlockSpec((1, dm, fc), lambda c, i: (c, 0, 0)),
                   pl.BlockSpec((1, fc, dm), lambda c, i: (c, 0, 0))] + [ANY] * n,
        out_shape=[jax.ShapeDtypeStruct((nc, seq, dm), F32), jax.ShapeDtypeStruct((nc, dm, fc), BF),
                   jax.ShapeDtypeStruct((nc, dm, fc), BF), jax.ShapeDtypeStruct((nc, fc, dm), BF)]
                  + [it.out_shape for it in items],
        scratch_shapes=[pltpu.VMEM((dm, fc), F32), pltpu.VMEM((dm, fc), F32), pltpu.VMEM((fc, dm), F32)]
                       + (_comm_sems(items) if n else []),
        compiler_params=_cparams(("arbitrary", "arbitrary")), name=name,
    )(x, g, dh, w1, w3, w2, *[it.operand for it in items])
    return res[0], res[1], res[2], res[3], list(res[4:])


def ffn_bwd_norm(x, g, parts, dh, tm, name):
    seq, dm = x.shape
    nc = parts.shape[0]
    nt = seq // tm

    def body(x_ref, g_ref, p_ref, dh_ref, dx_ref, dg_ref):
        i = pl.program_id(0)
        dxn = p_ref[0]
        for c in range(1, nc):
            dxn = dxn + p_ref[c]
        _, vjp = jax.vjp(rms, x_ref[...], g_ref[...])
        dx, dg = vjp(dxn)
        dx_ref[...] = dx + dh_ref[...]

        @pl.when(i == 0)
        def _():
            dg_ref[...] = dg

        @pl.when(i > 0)
        def _():
            dg_ref[...] += dg

    return pl.pallas_call(
        body, grid=(nt,),
        in_specs=[pl.BlockSpec((tm, dm), lambda i: (i, 0)), pl.BlockSpec((1, dm), lambda i: (0, 0)),
                  pl.BlockSpec((nc, tm, dm), lambda i: (0, i, 0)), pl.BlockSpec((tm, dm), lambda i: (i, 0))],
        out_specs=[pl.BlockSpec((tm, dm), lambda i: (i, 0)), pl.BlockSpec((1, dm), lambda i: (0, 0))],
        out_shape=[jax.ShapeDtypeStruct((seq, dm), F32), jax.ShapeDtypeStruct((1, dm), F32)],
        compiler_params=_cparams(("arbitrary",)), name=name,
    )(x, g, parts, dh)


def rope_tables(pos_col, inv_freq, tm):
    seq = pos_col.shape[0]

    def body(p_ref, f_ref, c_ref, s_ref):
        ang = p_ref[...].astype(F32) * f_ref[...]
        lane = lax.broadcasted_iota(jnp.int32, ang.shape, 1)
        c_ref[...] = jnp.cos(ang)
        s_ref[...] = jnp.where(lane < HEAD // 2, -jnp.sin(ang), jnp.sin(ang))

    return pl.pallas_call(
        body, grid=(seq // tm,),
        in_specs=[pl.BlockSpec((tm, 1), lambda i: (i, 0)), pl.BlockSpec((1, HEAD), lambda i: (0, 0))],
        out_specs=[pl.BlockSpec((tm, HEAD), lambda i: (i, 0))] * 2,
        out_shape=[jax.ShapeDtypeStruct((seq, HEAD), F32)] * 2,
        compiler_params=_cparams(("arbitrary",)), name="rope_tables",
    )(pos_col, inv_freq)


def loss_and_grad(y, target, tm):
    seq, dm = y.shape

    def body(y_ref, t_ref, l_ref, dy_ref):
        i = pl.program_id(0)
        err = y_ref[...] - t_ref[...]
        dy_ref[...] = err * (1.0 / dm)
        part = 0.5 * jnp.sum(jnp.mean(err * err, axis=-1, keepdims=True), axis=0, keepdims=True)
        part = jnp.broadcast_to(part, (1, LANE))

        @pl.when(i == 0)
        def _():
            l_ref[...] = part

        @pl.when(i > 0)
        def _():
            l_ref[...] += part

    return pl.pallas_call(
        body, grid=(seq // tm,),
        in_specs=[pl.BlockSpec((tm, dm), lambda i: (i, 0))] * 2,
        out_specs=[pl.BlockSpec((1, LANE), lambda i: (0, 0)), pl.BlockSpec((tm, dm), lambda i: (i, 0))],
        out_shape=[jax.ShapeDtypeStruct((1, LANE), F32), jax.ShapeDtypeStruct((seq, dm), F32)],
        compiler_params=_cparams(("arbitrary",)), name="loss_and_grad",
    )(y, target)


def add2(a, b, name):
    r, c = a.shape
    rows = min(r, 512)

    def body(a_ref, b_ref, o_ref):
        o_ref[...] = a_ref[...] + b_ref[...]

    spec = pl.BlockSpec((rows, c), lambda i: (i, 0))
    return pl.pallas_call(body, grid=(r // rows,), in_specs=[spec, spec], out_specs=spec,
                          out_shape=jax.ShapeDtypeStruct((r, c), F32), compiler_params=_cparams(("arbitrary",)),
                          name=name)(a, b)


def adamw(w, g, m, v, rows, name):
    r, c = w.shape

    def body(w_ref, g_ref, m_ref, v_ref, d_ref, nm_ref, nv_ref):
        gv = g_ref[...]
        nm = ADAM_B1 * m_ref[...] + (1.0 - ADAM_B1) * gv
        nv = ADAM_B2 * v_ref[...] + (1.0 - ADAM_B2) * (gv * gv)
        m_hat = nm / (1.0 - ADAM_B1 ** ADAM_STEP)
        v_hat = nv / (1.0 - ADAM_B2 ** ADAM_STEP)
        d_ref[...] = -ADAM_LR * (m_hat / (jnp.sqrt(v_hat) + ADAM_EPS) + ADAM_WD * w_ref[...])
        nm_ref[...] = nm
        nv_ref[...] = nv

    spec = pl.BlockSpec((rows, c), lambda i: (i, 0))
    return pl.pallas_call(
        body, grid=(r // rows,), in_specs=[spec] * 4, out_specs=[spec] * 3,
        out_shape=[jax.ShapeDtypeStruct((r, c), F32)] * 3,
        compiler_params=_cparams(("arbitrary",)), name=name,
    )(w, g, m, v)


def sum_layers(recvs, out_rows, out_cols, rows, name):
    depth = len(recvs)
    n, _, c_in = recvs[0].shape

    def body(*refs):
        o_ref = refs[depth]
        for l in range(depth):
            @pl.when(pl.program_id(0) == l)
            def _():
                acc = refs[l][0].astype(F32)
                for j in range(1, n):
                    acc = acc + refs[l][j].astype(F32)
                o_ref[0] = acc[:, :out_cols]

    return pl.pallas_call(
        body, grid=(depth, out_rows // rows),
        in_specs=[pl.BlockSpec((n, rows, c_in), lambda ll, i, l=l: (0, jnp.where(ll == l, i, 0), 0))
                  for l in range(depth)],
        out_specs=pl.BlockSpec((1, rows, out_cols), lambda ll, i: (ll, i, 0)),
        out_shape=jax.ShapeDtypeStruct((depth, out_rows, out_cols), F32),
        compiler_params=_cparams(("arbitrary", "arbitrary")), name=name,
    )(*recvs)


def _heads(t, n):
    return t.reshape(t.shape[0], n, HEAD).transpose(1, 0, 2)


def _unheads(t):
    return t.transpose(1, 0, 2).reshape(t.shape[1], t.shape[0] * HEAD)


def _mkv_heads(mkv):
    mw = N_MEMH * HEAD
    return _heads(mkv[:, :mw], N_MEMH), _heads(mkv[:, mw:], N_MEMH)


def _mix_args(sv, tabs, sw):
    mk, mv = _mkv_heads(sv["mkv"])
    ct, st = tabs
    conv_args = [(sv["a"], HALO, True), (sv["gate"], HALO, True), (sw["conv_w"], PARAM, True),
                 (sw["conv_b"], PARAM, True), (sw["conv_ln_g"], PARAM, True), (sw["conv_ln_b"], PARAM, True)]
    swa_args = [(sv["q"], TILE, True), (sv["k"], HALO, True), (sv["v"], HALO, True), (ct, HALO, False),
                (st, HALO, False), (sw["swa_q_norm"], PARAM, True), (sw["swa_k_norm"], PARAM, True),
                (sw["swa_sinks"], PARAM, True)]
    mem_args = [(sv["qm"], TILE, True), (mk, PARAM, True), (mv, PARAM, True), (sw["mem_q_norm"], PARAM, True),
                (sw["mem_k_norm"], PARAM, True)]
    return conv_args, swa_args, mem_args


def _proj_args(h1, sw, bw):
    return [(h1, TILE, True), (sw["mix_norm"], PARAM, True), (bw["w_in"], PARAM, True)]


def _join_args(sv, bw):
    return [(sv["h1"], TILE, True), (sv["y_conv"], TILE, True), (sv["y_swa"], TILE, True), (sv["y_mem"], TILE, True),
            (bw["w_out"], PARAM, True)]


def layer_fwd(x, mem, tabs, bw, sent, sw, tm, l, depth):
    seq, dm = x.shape
    tag = f"_l{l}"
    sv = dict(x=x)
    sv["h1"], got1 = ffn_fwd(x, sw["ffn1_norm"], bw["ffn1_w1"], bw["ffn1_w3"], bw["ffn1_w2"], tm, "ffn1_fwd" + tag,
                             gather_items(sent, l, GATHER_LATE))
    bw = {**bw, **kernel_layouts(got1, GATHER_LATE)}
    items2 = gather_items(sent, l + 1, GATHER_EARLY) if l + 1 < depth else ()
    split_outs = [((seq, CONV_CH), F32), ((seq, CONV_CH), F32), ((N_Q, seq, HEAD), F32), ((N_KV, seq, HEAD), F32),
                  ((N_KV, seq, HEAD), F32), ((N_MEMH, seq, HEAD), F32)]
    sv["a"], sv["gate"], sv["q"], sv["k"], sv["v"], sv["qm"] = seq_fwd(
        f_proj_split, _proj_args(sv["h1"], sw, bw), split_outs, tm, tm, "proj_in_fwd" + tag)
    ml = mem.shape[0]
    (sv["mkv"],) = seq_fwd(f_proj_in, [(mem, TILE, False), (sw["mem_norm"], PARAM, True), (bw["w_mem_kv"], PARAM, True)],
                           [((ml, bw["w_mem_kv"].shape[1]), F32)], ml, ml, "mem_kv_fwd" + tag)
    conv_args, swa_args, mem_args = _mix_args(sv, tabs, sw)
    (sv["y_conv"],) = seq_fwd(f_conv, conv_args, [((seq, CONV_CH), F32)], tm, 32, "conv_fwd" + tag)
    (sv["y_swa"],) = seq_fwd(f_swa, swa_args, [((N_Q, seq, HEAD), F32)], tm, BLOCK, "swa_fwd" + tag)
    (sv["y_mem"],) = seq_fwd(f_mem, mem_args, [((N_MEMH, seq, HEAD), F32)], tm, tm, "mem_attn_fwd" + tag)
    (sv["h2"],) = seq_fwd(f_out_join, _join_args(sv, bw), [((seq, dm), F32)], tm, tm, "out_proj_fwd" + tag)
    sv["h3"], got2 = ffn_fwd(sv["h2"], sw["ffn2_norm"], bw["ffn2_w1"], bw["ffn2_w3"], bw["ffn2_w2"], tm,
                             "ffn2_fwd" + tag, items2)
    (xo,) = seq_fwd(f_rms, [(sv["h3"], TILE, True), (sw["final_norm"], PARAM, True)], [((seq, dm), F32)], tm, tm,
                    "final_norm_fwd" + tag)
    return xo, sv, bw, kernel_layouts(got2, GATHER_EARLY)


def layer_bwd(dxo, sv, mem, tabs, bw, sw, tm, l, items=()):
    tag = f"_l{l}"
    gb, gs = {}, {}
    dh3, gs["final_norm"] = seq_bwd(f_rms, [(sv["h3"], TILE, True), (sw["final_norm"], PARAM, True)], [dxo], tm, tm,
                                    "final_norm_bwd" + tag)
    parts2, gb["ffn2_w1"], gb["ffn2_w3"], gb["ffn2_w2"], got = ffn_bwd_chunks(
        sv["h2"], sw["ffn2_norm"], dh3, bw["ffn2_w1"], bw["ffn2_w3"], bw["ffn2_w2"], tm, "ffn2_bwd" + tag, items)
    dh2, gs["ffn2_norm"] = ffn_bwd_norm(sv["h2"], sw["ffn2_norm"], parts2, dh3, tm, "ffn2_norm_bwd" + tag)
    dh1_a, dy_conv, dy_swa, dy_mem, gb["w_out"] = seq_bwd(f_out_join, _join_args(sv, bw), [dh2], tm // 2, tm // 2,
                                                           "out_proj_bwd" + tag)
    conv_args, swa_args, mem_args = _mix_args(sv, tabs, sw)
    da, dgate, gs["conv_w"], gs["conv_b"], gs["conv_ln_g"], gs["conv_ln_b"] = seq_bwd(
        f_conv, conv_args, [dy_conv], tm, 32, "conv_bwd" + tag)
    dq, dk, dv, gs["swa_q_norm"], gs["swa_k_norm"], gs["swa_sinks"] = seq_bwd(
        f_swa, swa_args, [dy_swa], tm, BLOCK, "swa_bwd" + tag)
    dqm, dmk, dmv, gs["mem_q_norm"], gs["mem_k_norm"] = seq_bwd(f_mem, mem_args, [dy_mem], tm, tm, "mem_attn_bwd" + tag)
    dmkv = jnp.concatenate([_unheads(dmk), _unheads(dmv)], axis=-1)
    ml = mem.shape[0]
    gs["mem_norm"], gb["w_mem_kv"] = seq_bwd(
        f_proj_in, [(mem, TILE, False), (sw["mem_norm"], PARAM, True), (bw["w_mem_kv"], PARAM, True)], [dmkv], ml, ml,
        "mem_kv_bwd" + tag)
    dh1_b, gs["mix_norm"], gb["w_in"] = seq_bwd(f_proj_split, _proj_args(sv["h1"], sw, bw),
                                                 [da, dgate, dq, dk, dv, dqm], tm // 2, tm // 2, "proj_in_bwd" + tag)
    dh1 = add2(dh1_a, dh1_b, "add_dh1" + tag)
    parts1, gb["ffn1_w1"], gb["ffn1_w3"], gb["ffn1_w2"], got_late = ffn_bwd_chunks(
        sv["x"], sw["ffn1_norm"], dh1, bw["ffn1_w1"], bw["ffn1_w3"], bw["ffn1_w2"], tm, "ffn1_bwd" + tag,
        scatter_items(gb, GATHER_LATE))
    dx, gs["ffn1_norm"] = ffn_bwd_norm(sv["x"], sw["ffn1_norm"], parts1, dh1, tm, "ffn1_norm_bwd" + tag)
    return dx, gb, gs, got, got_late


FFN_UP = ("ffn1_w1", "ffn1_w3", "ffn2_w1", "ffn2_w3")
FFN_DOWN = ("ffn1_w2", "ffn2_w2")
ROW_SHARDED = ("w_mem_kv", "w_out")
GATHER_EARLY = ("ffn1_w1", "ffn1_w3", "ffn1_w2")
GATHER_LATE = ("w_in", "w_mem_kv", "w_out", "ffn2_w1", "ffn2_w3", "ffn2_w2")
BIG = GATHER_EARLY + GATHER_LATE
SMALL = ("ffn1_norm", "mix_norm", "conv_b", "conv_ln_g", "conv_ln_b", "swa_q_norm", "swa_k_norm", "swa_sinks",
         "mem_norm", "mem_q_norm", "mem_k_norm", "ffn2_norm", "final_norm")
WEIGHTS = ("ffn1_norm", "ffn1_w1", "ffn1_w3", "ffn1_w2", "mix_norm", "w_in", "conv_w", "conv_b", "conv_ln_g",
           "conv_ln_b", "swa_q_norm", "swa_k_norm", "swa_sinks", "mem_norm", "w_mem_kv", "mem_q_norm", "mem_k_norm",
           "w_out", "ffn2_norm", "ffn2_w1", "ffn2_w3", "ffn2_w2", "final_norm")


def _round_up(n, m):
    return -(-n // m) * m


def _row_block(rows, target, mult=8):
    best = rows
    for cand in range(mult, min(rows, target) + 1, mult):
        if rows % cand == 0:
            best = cand
    return best


def _pad_rows(flat, cols, mult):
    n = flat.shape[0]
    rows = _round_up(-(-n // cols), mult)
    return jnp.pad(flat, (0, rows * cols - n)).reshape(rows, cols)


def send_form(w):
    out = {}
    for n in BIG:
        t = w[n].astype(BF)
        if n in FFN_UP:
            t = jnp.pad(t, ((0, 0), (0, 0), (0, _round_up(t.shape[2], LANE) - t.shape[2])))
        elif n in FFN_DOWN:
            t = jnp.pad(t, ((0, 0), (0, _round_up(t.shape[1], LANE) - t.shape[1]), (0, 0)))
        out[n] = t
    return out


def gather_items(sent, l, names):
    return [gather_cols_item(sent[n][l]) if n in FFN_UP else gather_item(sent[n][l]) for n in names]


def kernel_layouts(got, names):
    out = {}
    for n, t in zip(names, got):
        if n in FFN_DOWN:
            t = t.reshape(N_DEV // 2, 2 * t.shape[1], t.shape[2])
        elif n == "w_in":
            t = t.transpose(1, 0, 2).reshape(t.shape[1], N_DEV * t.shape[2])
        elif n in ROW_SHARDED:
            t = t.reshape(N_DEV * t.shape[1], t.shape[2])
        out[n] = t
    return out


def scatter_items(gb, names):
    items = []
    for n in names:
        t = gb[n]
        if n in FFN_UP:
            items.append(scatter_cols_item(t))
            continue
        if n in FFN_DOWN:
            t = t.reshape(N_DEV, t.shape[1] // 2, t.shape[2])
        elif n == "w_in":
            t = t.reshape(t.shape[0], N_DEV, t.shape[1] // N_DEV).transpose(1, 0, 2).astype(BF)
        else:
            t = t.reshape(N_DEV, t.shape[0] // N_DEV, t.shape[1]).astype(BF)
        items.append(scatter_item(t))
    return items


def small_layer_params(w, l):
    sw = {n: w[n][l][None, :] for n in SMALL if n != "swa_sinks"}
    sw["swa_sinks"] = jnp.pad(w["swa_sinks"][l], (0, LANE - N_Q))[None, :]
    sw["conv_w"] = jnp.pad(w["conv_w_full"][l], ((0, 1), (0, 0)))
    return sw


def kernel(x, mem, positions, ffn1_norm, ffn1_w1, ffn1_w3, ffn1_w2, mix_norm, w_in, conv_w, conv_b, conv_ln_g, conv_ln_b, swa_q_norm, swa_k_norm, swa_sinks, mem_norm, w_mem_kv, mem_q_norm, mem_k_norm, w_out, ffn2_norm, ffn2_w1, ffn2_w3, ffn2_w2, final_norm, loss_target, m_ffn1_norm, m_ffn1_w1, m_ffn1_w3, m_ffn1_w2, m_mix_norm, m_w_in, m_conv_w, m_conv_b, m_conv_ln_g, m_conv_ln_b, m_swa_q_norm, m_swa_k_norm, m_swa_sinks, m_mem_norm, m_w_mem_kv, m_mem_q_norm, m_mem_k_norm, m_w_out, m_ffn2_norm, m_ffn2_w1, m_ffn2_w3, m_ffn2_w2, m_final_norm, v_ffn1_norm, v_ffn1_w1, v_ffn1_w3, v_ffn1_w2, v_mix_norm, v_w_in, v_conv_w, v_conv_b, v_conv_ln_g, v_conv_ln_b, v_swa_q_norm, v_swa_k_norm, v_swa_sinks, v_mem_norm, v_w_mem_kv, v_mem_q_norm, v_mem_k_norm, v_w_out, v_ffn2_norm, v_ffn2_w1, v_ffn2_w3, v_ffn2_w2, v_final_norm):
    loc = locals()
    w = {n: loc[n] for n in WEIGHTS}
    m = {n: loc["m_" + n] for n in WEIGHTS}
    v = {n: loc["v_" + n] for n in WEIGHTS}
    depth = ffn1_norm.shape[0]
    seq = x.shape[1]
    tm = min(512, seq)
    me = 4 * lax.axis_index("x") + 2 * lax.axis_index("y") + lax.axis_index("c")
    xs, mems, target = x[0], mem[0], loss_target[0]

    sent = send_form(w)
    cw = conv_w.shape[2]
    conv_rows = _pad_rows(conv_w.reshape(-1), LANE, 8)
    got = comm_call(gather_items(sent, 0, GATHER_EARLY) + [gather_item(conv_rows)], "gather_l0")
    bw = kernel_layouts(got[:-1], GATHER_EARLY)
    conv_full = got[-1].reshape(N_DEV, -1)[:, :conv_w.size].reshape((N_DEV,) + conv_w.shape)
    small = {n: w[n] for n in SMALL}
    small["conv_w_full"] = conv_full.transpose(1, 2, 0, 3).reshape(depth, CONV_W, N_DEV * cw)

    inv = ROPE_THETA ** (-jnp.arange(0, HEAD, 2, dtype=F32) / HEAD)
    tabs = rope_tables(positions[0].reshape(-1, 1), jnp.concatenate([inv, inv])[None, :], tm)
    saved, bws = [], []
    h = xs
    for l in range(depth):
        h, sv, bw_all, bw = layer_fwd(h, mems, tabs, bw, sent, small_layer_params(small, l), tm, l, depth)
        saved.append(sv)
        bws.append(bw_all)
    loss_sum, dh = loss_and_grad(h, target, tm)
    loss = lax.psum(loss_sum[0, 0], ("x", "y", "c"))

    recv = [dict() for _ in range(depth)]
    gss = [None] * depth
    items = ()
    for l in reversed(range(depth)):
        dh, gb, gss[l], got, got_late = layer_bwd(dh, saved[l], mems, tabs, bws[l], small_layer_params(small, l), tm,
                                                  l, items)
        if items:
            recv[l + 1].update(zip(GATHER_EARLY, got))
        recv[l].update(zip(GATHER_LATE, got_late))
        items = scatter_items(gb, GATHER_EARLY)
    recv[0].update(zip(GATHER_EARLY, comm_call(items, "grad_exchange_l0")))

    g_all = {}
    for n in BIG:
        a, b = w[n].shape[1:]
        blocks = [recv[l][n] for l in range(depth)]
        if n in FFN_DOWN:
            rows = _row_block(a, 256, BF16_ROWS)
        else:
            rows = _row_block(a, 256)
        g_all[n] = sum_layers(blocks, a, b, rows, "grad_sum_" + n)

    small_names = SMALL + ("conv_w",)
    sizes = {n: (w[n].shape[1] if n != "conv_w" else CONV_W * N_DEV * cw) for n in small_names}
    flat = []
    for l in range(depth):
        for n in small_names:
            t = gss[l][n]
            if n == "swa_sinks":
                t = t[:, :N_Q]
            elif n == "conv_w":
                t = t[:CONV_W]
            flat.append(t.reshape(-1))
    flat = _pad_rows(jnp.concatenate(flat), LANE, 8)
    (small_got,) = comm_call([gather_item(flat)], "small_grad_gather")
    summed = sum_layers([small_got], flat.shape[0], LANE, flat.shape[0], "small_grad_sum").reshape(-1)
    g_small = {n: [] for n in small_names}
    o = 0
    for l in range(depth):
        for n in small_names:
            g_small[n].append(summed[o:o + sizes[n]])
            o += sizes[n]
    for n in SMALL:
        g_all[n] = jnp.stack(g_small[n])
    gcw = jnp.stack(g_small["conv_w"]).reshape(depth, CONV_W, N_DEV, cw)
    g_all["conv_w"] = lax.dynamic_slice(gcw, (0, 0, me, 0), (depth, CONV_W, 1, cw)).reshape(depth, CONV_W, cw)

    outs = {"delta": {}, "new_m": {}, "new_v": {}}
    for n in BIG:
        shape = w[n].shape
        two_d = (shape[0] * shape[1], shape[2])
        res = adamw(w[n].reshape(two_d), g_all[n].reshape(two_d), m[n].reshape(two_d), v[n].reshape(two_d),
                    _row_block(two_d[0], 512), "adamw_" + n)
        for key, arr in zip(("delta", "new_m", "new_v"), res):
            outs[key][n] = arr.reshape(shape)

    def packed_small(t):
        return _pad_rows(jnp.concatenate([t[n].reshape(-1) for n in small_names]), LANE, 8)

    ps = [packed_small(t) for t in (w, g_all, m, v)]
    res = adamw(*ps, ps[0].shape[0], "adamw_small")
    for key, arr in zip(("delta", "new_m", "new_v"), res):
        fl, o = arr.reshape(-1), 0
        for n in small_names:
            outs[key][n] = fl[o:o + w[n].size].reshape(w[n].shape)
            o += w[n].size

    return (loss, dh[None], *[g_all[n] for n in WEIGHTS], *[outs["delta"][n] for n in WEIGHTS],
            *[outs["new_m"][n] for n in WEIGHTS], *[outs["new_v"][n] for n in WEIGHTS])
```

```python
import jax
import jax.numpy as jnp
from jax import lax
from jax.experimental import pallas as pl
from jax.experimental.pallas import tpu as pltpu

F32 = jnp.float32
BF = jnp.bfloat16
EPS = 1e-6
HEAD = 64
N_Q, N_KV, N_MEMH = 6, 2, 4
CONV_CH, CONV_W = 384, 31
BLOCK = 128
ROPE_THETA = 10000.0
N_DEV = 8
V7X_VMEM_LIMIT = 56 * 1024 * 1024
LANE = 128
BF16_ROWS = 16

ADAM_LR, ADAM_B1, ADAM_B2, ADAM_EPS, ADAM_WD, ADAM_STEP = 0.001, 0.9, 0.999, 1e-08, 0.01, 10

TILE, HALO, PARAM = "tile", "halo", "param"
MESH = pl.DeviceIdType.MESH
ANY = pl.BlockSpec(memory_space=pl.ANY)


def _cparams(sem=None):
    kw = dict(vmem_limit_bytes=V7X_VMEM_LIMIT)
    if sem is not None:
        kw["dimension_semantics"] = sem
    return pltpu.CompilerParams(**kw)


def _dot(a, b, dims):
    return lax.dot_general(a, b, (dims, ((), ())), preferred_element_type=F32)


_NN, _NT, _TN = ((1,), (0,)), ((1,), (1,)), ((0,), (0,))


@jax.custom_vjp
def mm(a, b):
    return _dot(a.astype(BF), b.astype(BF), _NN)


def _mm_fwd(a, b):
    return mm(a, b), (a, b)


def _mm_bwd(res, g):
    a, b = res
    gb = g.astype(BF)
    return _dot(gb, b.astype(BF), _NT), _dot(a.astype(BF), gb, _TN)


mm.defvjp(_mm_fwd, _mm_bwd)


@jax.custom_vjp
def mm_nt(a, b):
    return _dot(a.astype(BF), b.astype(BF), _NT)


def _mm_nt_fwd(a, b):
    return mm_nt(a, b), (a, b)


def _mm_nt_bwd(res, g):
    a, b = res
    gb = g.astype(BF)
    return _dot(gb, b.astype(BF), _NN), _dot(gb, a.astype(BF), _TN)


mm_nt.defvjp(_mm_nt_fwd, _mm_nt_bwd)


def rms(x, g):
    return x * lax.rsqrt(jnp.mean(x * x, axis=-1, keepdims=True) + EPS) * g


def _swap_matrix():
    i = lax.broadcasted_iota(jnp.int32, (HEAD, HEAD), 0)
    j = lax.broadcasted_iota(jnp.int32, (HEAD, HEAD), 1)
    src = jnp.where(j < HEAD // 2, j + HEAD // 2, j - HEAD // 2)
    return (i == src).astype(F32)


def rope(x, c, s, swap):
    xs = jnp.dot(x, swap, precision=lax.Precision.HIGHEST, preferred_element_type=F32)
    return x * c + xs * s


SUBLANES = 8


def _row_shifts(t, rows):
    return [t] + [t[b:b + rows + 24] for b in range(1, SUBLANES)]


CONV_ROWS = 64


def _window(shifts, offset, start, rows):
    base = offset - offset % SUBLANES + start
    return shifts[offset % SUBLANES][base:base + rows]


def _row_blocks(rows):
    return [(r, min(CONV_ROWS, rows - r)) for r in range(0, rows, CONV_ROWS)]


def _conv_taps(gs, w, tm, first_offset=2, step=1):
    wr = [w[j:j + 1, :] for j in range(CONV_W)]
    blocks = []
    for r, n in _row_blocks(tm):
        acc = wr[0] * _window(gs, first_offset, r, n)
        for j in range(1, CONV_W):
            acc = acc + wr[j] * _window(gs, first_offset + step * j, r, n)
        blocks.append(acc)
    return jnp.concatenate(blocks, axis=0)


@jax.custom_vjp
def causal_dw_conv(g, w):
    tm = g.shape[0] - 32
    return _conv_taps(_row_shifts(g, tm), w, tm)


def _conv_fwd(g, w):
    tm = g.shape[0] - 32
    gs = _row_shifts(g, tm)
    return _conv_taps(gs, w, tm), (gs, w)


def _conv_bwd(res, dc):
    gs, w = res
    tm, ch = dc.shape
    z = jnp.zeros((32, ch), F32)
    ds = _row_shifts(jnp.concatenate([z, dc, z], axis=0), tm + 32)
    dg = _conv_taps(ds, w, tm + 32, first_offset=30, step=-1)
    rows = lax.broadcasted_iota(jnp.int32, (32, 1), 0)
    dcb = [dc[r:r + n] for r, n in _row_blocks(tm)]
    dw = jnp.zeros((32, ch), F32)
    for j in range(CONV_W):
        part = jnp.zeros((SUBLANES, ch), F32)
        for (r, n), d in zip(_row_blocks(tm), dcb):
            prod = d * _window(gs, 2 + j, r, n)
            for s in range(0, n, SUBLANES):
                part = part + prod[s:s + SUBLANES]
        dw = dw + jnp.where(rows == j, jnp.sum(part, axis=0, keepdims=True), 0.0)
    return dg, dw


causal_dw_conv.defvjp(_conv_fwd, _conv_bwd)


def f_rms(first, x, g):
    return (rms(x, g),)


def f_proj_in(first, h, g, w):
    return (mm(rms(h, g), w),)


def _to_heads(t, n):
    return jnp.stack([t[:, h * HEAD:(h + 1) * HEAD] for h in range(n)])


def f_proj_split(first, h, g, w):
    p = mm(rms(h, g), w)
    o, outs = 0, []
    for width, n in ((CONV_CH, 0), (CONV_CH, 0), (N_Q * HEAD, N_Q), (N_KV * HEAD, N_KV), (N_KV * HEAD, N_KV),
                     (N_MEMH * HEAD, N_MEMH)):
        t = p[:, o:o + width]
        outs.append(_to_heads(t, n) if n else t)
        o += width
    return tuple(outs)


def f_out_join(first, h, y_conv, y_swa, y_mem, w):
    y = jnp.concatenate([y_conv] + [y_swa[i] for i in range(N_Q)] + [y_mem[i] for i in range(N_MEMH)], axis=1)
    return (h + mm(y, w),)


def f_conv(first, a, gate, w, b, lg, lb):
    keep = 1.0 - first
    av = jnp.concatenate([a[0] * keep, a[1]], axis=0)
    gv = jnp.concatenate([gate[0], gate[1]], axis=0)
    glu = av * jax.nn.sigmoid(gv)
    c = causal_dw_conv(glu, w) + b
    mu = jnp.mean(c, axis=-1, keepdims=True)
    var = jnp.mean(jnp.square(c - mu), axis=-1, keepdims=True)
    z = (c - mu) * lax.rsqrt(var + EPS) * lg + lb
    return (z * jax.nn.sigmoid(z),)


def _softmax_with_extra(s, extra):
    m = jnp.max(s, axis=-1, keepdims=True)
    if extra is not None:
        m = jnp.maximum(m, extra)
    m = lax.stop_gradient(m)
    e = jnp.exp(s - m)
    den = jnp.sum(e, axis=-1, keepdims=True)
    if extra is not None:
        den = den + jnp.exp(extra - m)
    return e / den


def f_swa(first, q, k, v, ct, st, qn, kn, sinks):
    tm = q.shape[1]
    nb = tm // BLOCK
    g = N_Q // N_KV
    swap = _swap_matrix()
    c_all = jnp.concatenate([ct[0], ct[1]], axis=0)
    s_all = jnp.concatenate([st[0], st[1]], axis=0)
    qi = lax.broadcasted_iota(jnp.int32, (g * BLOCK, 2 * BLOCK), 0)
    kj = lax.broadcasted_iota(jnp.int32, (g * BLOCK, 2 * BLOCK), 1)
    qpos = jnp.where(qi >= 2 * BLOCK, qi - 2 * BLOCK, jnp.where(qi >= BLOCK, qi - BLOCK, qi)) + BLOCK
    rel = qpos - kj
    band = (rel >= 0) & (rel < BLOCK)
    band_first = band & ((kj >= BLOCK) | (first < 0.5))
    lane = lax.broadcasted_iota(jnp.int32, (g * BLOCK, LANE), 1)
    hrow = lax.broadcasted_iota(jnp.int32, (g * BLOCK, LANE), 0)
    head_in_group = jnp.where(hrow >= 2 * BLOCK, 2, jnp.where(hrow >= BLOCK, 1, 0))
    qr = [rope(rms(q[h], qn), ct[1], st[1], swap) for h in range(N_Q)]
    outs = [[None] * nb for _ in range(N_Q)]
    for hk in range(N_KV):
        kk = rope(rms(jnp.concatenate([k[0][hk], k[1][hk]], axis=0), kn), c_all, s_all, swap)
        vv = jnp.concatenate([v[0][hk], v[1][hk]], axis=0)
        sel = (lane == head_in_group + hk * g).astype(F32)
        sink_col = jnp.sum(sel * sinks, axis=1, keepdims=True)
        for j in range(nb):
            keys = kk[j * BLOCK:(j + 2) * BLOCK]
            vals = vv[j * BLOCK:(j + 2) * BLOCK]
            qs = jnp.concatenate([qr[hk * g + gg][j * BLOCK:(j + 1) * BLOCK] for gg in range(g)], axis=0)
            s = mm_nt(qs, keys) * (HEAD ** -0.5)
            s = jnp.where(band_first if j == 0 else band, s, -1e30)
            o = mm(_softmax_with_extra(s, sink_col), vals)
            for gg in range(g):
                outs[hk * g + gg][j] = o[gg * BLOCK:(gg + 1) * BLOCK]
    return (jnp.stack([jnp.concatenate(outs[h], axis=0) for h in range(N_Q)]),)


def f_mem(first, qm, mk, mv, qn, kn):
    outs = []
    for h in range(N_MEMH):
        qh = rms(qm[h], qn)
        kh = rms(mk[h], kn)
        s = mm_nt(qh, kh) * (HEAD ** -0.5)
        outs.append(mm(_softmax_with_extra(s, None), mv[h]))
    return (jnp.stack(outs),)


def _seq_len(arr):
    return arr.shape[0] if arr.ndim == 2 else arr.shape[1]


def _tile_spec(arr, rows, imap):
    if arr.ndim == 2:
        return pl.BlockSpec((rows, arr.shape[1]), lambda i: (imap(i), 0))
    return pl.BlockSpec((arr.shape[0], rows, arr.shape[2]), lambda i: (0, imap(i), 0))


def _full_spec(arr):
    nd = arr.ndim
    return pl.BlockSpec(arr.shape, lambda i: (0,) * nd)


def _in_specs(args, tm, hl, nt):
    specs = []
    ratio = tm // hl
    cur = lambda i: jnp.minimum(i, nt - 1)
    prev = lambda i: jnp.maximum(jnp.minimum(i, nt - 1) * ratio - 1, 0)
    for arr, kind, _ in args:
        if kind == TILE:
            specs.append(_tile_spec(arr, tm, cur))
        elif kind == HALO:
            specs.append(_tile_spec(arr, hl, prev))
            specs.append(_tile_spec(arr, tm, cur))
        else:
            specs.append(_full_spec(arr))
    return specs


def _operands(args):
    ops = []
    for arr, kind, _ in args:
        ops.append(arr)
        if kind == HALO:
            ops.append(arr)
    return ops


def _load_values(args, refs):
    vals, k = [], 0
    for arr, kind, _ in args:
        if kind == HALO:
            vals.append((refs[k][...].astype(F32), refs[k + 1][...].astype(F32)))
            k += 2
        else:
            vals.append(refs[k][...].astype(F32))
            k += 1
    return vals


def seq_fwd(f, args, outs, tm, hl, name):
    out_shape = [jax.ShapeDtypeStruct(s, d) for s, d in outs]
    nt = _seq_len(out_shape[0]) // tm
    n_in = len(_operands(args))

    def body(*refs):
        first = (pl.program_id(0) == 0).astype(F32)
        res = f(first, *_load_values(args, refs[:n_in]))
        for r, o in zip(refs[n_in:], res):
            r[...] = o.astype(r.dtype)

    cur = lambda i: i
    return pl.pallas_call(
        body, grid=(nt,), in_specs=_in_specs(args, tm, hl, nt),
        out_specs=[_tile_spec(o, tm, cur) for o in out_shape], out_shape=out_shape,
        compiler_params=_cparams(("arbitrary",)), name=name,
    )(*_operands(args))


def seq_bwd(f, args, douts, tm, hl, name):
    seq = _seq_len(douts[0])
    nt = seq // tm
    lag = any(kind == HALO and diff for _, kind, diff in args)
    steps = nt + 1 if lag else nt
    n_in = len(_operands(args))
    n_do = len(douts)
    dargs = [(arr, kind) for arr, kind, diff in args if diff]

    def body(*refs):
        in_refs = refs[:n_in]
        do_refs = refs[n_in:n_in + n_do]
        g_refs = refs[n_in + n_do:n_in + n_do + len(dargs)]
        carries = refs[n_in + n_do + len(dargs):]
        i = pl.program_id(0)
        first = (i == 0).astype(F32)

        def compute():
            vals = _load_values(args, in_refs)
            dvals = [v for v, (_, _, diff) in zip(vals, args) if diff]

            def fd(*dv):
                it = iter(dv)
                return f(first, *[next(it) if diff else v for v, (_, _, diff) in zip(vals, args)])

            _, vjp = jax.vjp(fd, *dvals)
            grads = vjp(tuple(r[...].astype(F32) for r in do_refs))
            c = 0
            for gref, gval, (arr, kind) in zip(g_refs, grads, dargs):
                if kind == TILE:
                    gref[...] = gval.astype(gref.dtype)
                elif kind == PARAM:
                    @pl.when(i == 0)
                    def _():
                        gref[...] = gval

                    @pl.when(i > 0)
                    def _():
                        gref[...] += gval
                else:
                    carry = carries[c]
                    c += 1
                    g_prev, g_cur = gval

                    @pl.when(i > 0)
                    def _():
                        gref[...] = carry[...]
                        if arr.ndim == 2:
                            gref[tm - hl:tm, :] += g_prev
                        else:
                            gref[:, tm - hl:tm, :] += g_prev

                    carry[...] = g_cur

        if lag:
            pl.when(i < nt)(compute)

            @pl.when(i == nt)
            def _():
                c = 0
                for gref, (arr, kind) in zip(g_refs, dargs):
                    if kind == HALO:
                        gref[...] = carries[c][...]
                        c += 1
        else:
            compute()

    cur = lambda i: jnp.minimum(i, nt - 1)
    lagged = lambda i: jnp.maximum(i - 1, 0)
    out_shape, out_specs, scratch = [], [], []
    for arr, kind in dargs:
        out_shape.append(jax.ShapeDtypeStruct(arr.shape, F32))
        if kind == PARAM:
            out_specs.append(_full_spec(arr))
        else:
            out_specs.append(_tile_spec(arr, tm, lagged if kind == HALO else cur))
            if kind == HALO:
                blk = (tm, arr.shape[1]) if arr.ndim == 2 else (arr.shape[0], tm, arr.shape[2])
                scratch.append(pltpu.VMEM(blk, F32))
    in_specs = _in_specs(args, tm, hl, nt) + [_tile_spec(d, tm, cur) for d in douts]
    return pl.pallas_call(
        body, grid=(steps,), in_specs=in_specs, out_specs=out_specs, out_shape=out_shape,
        scratch_shapes=scratch, compiler_params=_cparams(("arbitrary",)), name=name,
    )(*_operands(args), *douts)


class Item:
    def __init__(self, operand, out_shape, src, dst, two_level=False):
        self.operand, self.out_shape, self.src, self.dst, self.two_level = operand, out_shape, src, dst, two_level


def _dev(p):
    return 4 * p[0] + 2 * p[1] + p[2]


def gather_item(shard):
    return Item(shard, jax.ShapeDtypeStruct((N_DEV,) + shard.shape, shard.dtype),
                lambda r, peer: r, lambda r, s: r.at[_dev(s)], two_level=True)


def gather_cols_item(shard):
    d, w = shard.shape
    return Item(shard, jax.ShapeDtypeStruct((N_DEV // 2, d, 2 * w), shard.dtype),
                lambda r, peer: r, lambda r, s: r.at[2 * s[0] + s[1], :, pl.ds(s[2] * w, w)], two_level=True)


def scatter_item(blocks):
    return Item(blocks, jax.ShapeDtypeStruct(blocks.shape, blocks.dtype),
                lambda r, peer: r.at[_dev(peer)], lambda r, s: r.at[_dev(s)])


def scatter_cols_item(full):
    n, d, w2 = full.shape
    w = w2 // 2
    return Item(full, jax.ShapeDtypeStruct((N_DEV, d, w), full.dtype),
                lambda r, peer: r.at[2 * peer[0] + peer[1], :, pl.ds(peer[2] * w, w)], lambda r, s: r.at[_dev(s)])


def _comm_sems(items):
    n = len(items) * (N_DEV - 1)
    return [pltpu.SemaphoreType.DMA((n,)), pltpu.SemaphoreType.DMA((n,)), pltpu.SemaphoreType.DMA((len(items),))]


PASSED_ON = (3, 5, 7)


def _comm_copies(items, in_refs, out_refs, sems, x, y, c):
    send_sems, recv_sems, local_sems = sems
    me = (x, y, c)
    sibling = (x, y, 1 - c)
    local, first, passed = [], [], []

    def remote(t, k, src, dst, to):
        n = t * (N_DEV - 1) + k - 1
        return lambda: pltpu.make_async_remote_copy(
            src_ref=src(), dst_ref=dst(), send_sem=send_sems.at[n], recv_sem=recv_sems.at[n],
            device_id=(to[0], to[1], jnp.int32(to[2])), device_id_type=MESH)

    for t, it in enumerate(items):
        local.append(lambda t=t, it=it: pltpu.make_async_copy(it.src(in_refs[t], me), it.dst(out_refs[t], me),
                                                               local_sems.at[t]))
        for k in range(1, N_DEV):
            peer = (1 - x if k & 4 else x, 1 - y if k & 2 else y, 1 - c if k & 1 else c)
            if it.two_level and k in PASSED_ON:
                origin = (peer[0], peer[1], c)
                landed = lambda t=t, it=it, origin=origin: it.dst(out_refs[t], origin)
                passed.append((t, k, remote(t, k, landed, landed, sibling)))
            else:
                first.append((t, k, remote(t, k, lambda t=t, it=it, peer=peer: it.src(in_refs[t], peer),
                                           lambda t=t, it=it: it.dst(out_refs[t], me), peer)))
    return local, first, passed


def _on_my_core(fn):
    x, y, c = lax.axis_index("x"), lax.axis_index("y"), lax.axis_index("c")
    for cv in (0, 1):
        @pl.when(c == cv)
        def _():
            fn(x, y, cv)


def comm_start(items, in_refs, out_refs, sems):
    def go(x, y, c):
        local, first, _ = _comm_copies(items, in_refs, out_refs, sems, x, y, c)
        for make in local:
            make().start()
        for _, _, make in first:
            make().start()

    _on_my_core(go)


def comm_pass_on(items, in_refs, out_refs, sems):
    def go(x, y, c):
        _, first, passed = _comm_copies(items, in_refs, out_refs, sems, x, y, c)
        arrived = {(t, k): make for t, k, make in first}
        for t, k, make in passed:
            arrived[(t, k - 1)]().wait_recv()
            make().start()

    _on_my_core(go)


def comm_wait(items, in_refs, out_refs, sems):
    def go(x, y, c):
        local, first, passed = _comm_copies(items, in_refs, out_refs, sems, x, y, c)
        waited = {(t, k - 1) for t, k, _ in passed}
        for t, k, make in first:
            cp = make()
            if (t, k) not in waited:
                cp.wait_recv()
            cp.wait_send()
        for _, _, make in passed:
            cp = make()
            cp.wait_recv()
            cp.wait_send()
        for make in local:
            make().wait()

    _on_my_core(go)


def comm_call(items, name):
    n = len(items)

    def body(*refs):
        in_refs, out_refs, sems = refs[:n], refs[n:2 * n], refs[2 * n:]
        comm_start(items, in_refs, out_refs, sems)
        comm_pass_on(items, in_refs, out_refs, sems)
        comm_wait(items, in_refs, out_refs, sems)

    return pl.pallas_call(
        body, in_specs=[ANY] * n, out_specs=[ANY] * n, out_shape=[it.out_shape for it in items],
        scratch_shapes=_comm_sems(items), name=name,
    )(*[it.operand for it in items])


def ffn_fwd(x, g, w1, w3, w2, tm, name, items=()):
    seq, dm = x.shape
    nc, _, fc = w1.shape
    nt = seq // tm
    n = len(items)

    def body(*refs):
        x_ref, g_ref, w1_ref, w3_ref, w2_ref = refs[:5]
        c_in, h_ref, c_out = refs[5:5 + n], refs[5 + n], refs[6 + n:6 + 2 * n]
        xn_s, acc_s = refs[6 + 2 * n:8 + 2 * n]
        sems = refs[8 + 2 * n:]
        i, c = pl.program_id(0), pl.program_id(1)

        if n:
            @pl.when((i == 0) & (c == 0))
            def _():
                comm_start(items, c_in, c_out, sems)

        @pl.when(c == 0)
        def _():
            xn_s[...] = rms(x_ref[...], g_ref[...]).astype(BF)
            acc_s[...] = jnp.zeros_like(acc_s)

        xn = xn_s[...]
        a = _dot(xn, w1_ref[0], _NN)
        b = _dot(xn, w3_ref[0], _NN)
        hid = (a * jax.nn.sigmoid(a)) * b
        acc_s[...] += _dot(hid.astype(BF), w2_ref[0], _NN)

        @pl.when(c == nc - 1)
        def _():
            h_ref[...] = x_ref[...] + 0.5 * acc_s[...]

        if n:
            step = i * nc + c
            total = nt * nc

            @pl.when(step == max(1, (13 * total) // 16))
            def _():
                comm_pass_on(items, c_in, c_out, sems)

            @pl.when(step == total - 1)
            def _():
                comm_wait(items, c_in, c_out, sems)

    res = pl.pallas_call(
        body, grid=(nt, nc),
        in_specs=[pl.BlockSpec((tm, dm), lambda i, c: (i, 0)), pl.BlockSpec((1, dm), lambda i, c: (0, 0)),
                  pl.BlockSpec((1, dm, fc), lambda i, c: (c, 0, 0)), pl.BlockSpec((1, dm, fc), lambda i, c: (c, 0, 0)),
                  pl.BlockSpec((1, fc, dm), lambda i, c: (c, 0, 0))] + [ANY] * n,
        out_specs=[pl.BlockSpec((tm, dm), lambda i, c: (i, 0))] + [ANY] * n,
        out_shape=[jax.ShapeDtypeStruct((seq, dm), F32)] + [it.out_shape for it in items],
        scratch_shapes=[pltpu.VMEM((tm, dm), BF), pltpu.VMEM((tm, dm), F32)] + (_comm_sems(items) if n else []),
        compiler_params=_cparams(("arbitrary", "arbitrary")), name=name,
    )(x, g, w1, w3, w2, *[it.operand for it in items])
    return res[0], list(res[1:])


def ffn_bwd_chunks(x, g, dh, w1, w3, w2, tm, name, items=()):
    seq, dm = x.shape
    nc, _, fc = w1.shape
    nt = seq // tm
    n = len(items)

    def body(*refs):
        x_ref, g_ref, dh_ref, w1_ref, w3_ref, w2_ref = refs[:6]
        c_in = refs[6:6 + n]
        dxn_ref, dw1_ref, dw3_ref, dw2_ref = refs[6 + n:10 + n]
        c_out = refs[10 + n:10 + 2 * n]
        a1_s, a3_s, a2_s = refs[10 + 2 * n:13 + 2 * n]
        sems = refs[13 + 2 * n:]
        c, i = pl.program_id(0), pl.program_id(1)

        if n:
            @pl.when((i == 0) & (c == 0))
            def _():
                comm_start(items, c_in, c_out, sems)

        xn = rms(x_ref[...], g_ref[...]).astype(BF)
        dy = (0.5 * dh_ref[...]).astype(BF)
        w1v, w3v, w2v = w1_ref[0], w3_ref[0], w2_ref[0]
        a = _dot(xn, w1v, _NN)
        b = _dot(xn, w3v, _NN)
        sig = jax.nn.sigmoid(a)
        sa = a * sig
        dhid = _dot(dy, w2v, _NT)
        db = (dhid * sa).astype(BF)
        da = (dhid * b * (sig * (1.0 + a * (1.0 - sig)))).astype(BF)
        dxn_ref[0] = _dot(da, w1v, _NT) + _dot(db, w3v, _NT)
        g1 = _dot(xn, da, _TN)
        g3 = _dot(xn, db, _TN)
        g2 = _dot((sa * b).astype(BF), dy, _TN)

        @pl.when(i == 0)
        def _():
            a1_s[...] = g1
            a3_s[...] = g3
            a2_s[...] = g2

        @pl.when(i > 0)
        def _():
            a1_s[...] += g1
            a3_s[...] += g3
            a2_s[...] += g2

        @pl.when(i == nt - 1)
        def _():
            dw1_ref[0] = a1_s[...].astype(BF)
            dw3_ref[0] = a3_s[...].astype(BF)
            dw2_ref[0] = a2_s[...].astype(BF)

        if n:
            @pl.when((i == nt - 1) & (c == nc - 1))
            def _():
                comm_wait(items, c_in, c_out, sems)

    res = pl.pallas_call(
        body, grid=(nc, nt),
        in_specs=[pl.BlockSpec((tm, dm), lambda c, i: (i, 0)), pl.BlockSpec((1, dm), lambda c, i: (0, 0)),
                  pl.BlockSpec((tm, dm), lambda c, i: (i, 0)),
                  pl.BlockSpec((1, dm, fc), lambda c, i: (c, 0, 0)), pl.BlockSpec((1, dm, fc), lambda c, i: (c, 0, 0)),
                  pl.BlockSpec((1, fc, dm), lambda c, i: (c, 0, 0))] + [ANY] * n,
        out_specs=[pl.BlockSpec((1, tm, dm), lambda c, i: (c, i, 0)),
                   pl.BlockSpec((1, dm, fc), lambda c, i: (c, 0, 0)), pl.BlockSpec((1, dm, fc), lambda c, i: (c, 0, 0)),
                   pl.BlockSpec((1, fc, dm), lambda c, i: (c, 0, 0))] + [ANY] * n,
        out_shape=[jax.ShapeDtypeStruct((nc, seq, dm), F32), jax.ShapeDtypeStruct((nc, dm, fc), BF),
                   jax.ShapeDtypeStruct((nc, dm, fc), BF), jax.ShapeDtypeStruct((nc, fc, dm), BF)]
                  + [it.out_shape for it in items],
        scratch_shapes=[pltpu.VMEM((dm, fc), F32), pltpu.VMEM((dm, fc), F32), pltpu.VMEM((fc, dm), F32)]
                       + (_comm_sems(items) if n else []),
        compiler_params=_cparams(("arbitrary", "arbitrary")), name=name,
    )(x, g, dh, w1, w3, w2, *[it.operand for it in items])
    return res[0], res[1], res[2], res[3], list(res[4:])


def ffn_bwd_norm(x, g, parts, dh, tm, name):
    seq, dm = x.shape
    nc = parts.shape[0]
    nt = seq // tm

    def body(x_ref, g_ref, p_ref, dh_ref, dx_ref, dg_ref):
        i = pl.program_id(0)
        dxn = p_ref[0]
        for c in range(1, nc):
            dxn = dxn + p_ref[c]
        _, vjp = jax.vjp(rms, x_ref[...], g_ref[...])
        dx, dg = vjp(dxn)
        dx_ref[...] = dx + dh_ref[...]

        @pl.when(i == 0)
        def _():
            dg_ref[...] = dg

        @pl.when(i > 0)
        def _():
            dg_ref[...] += dg

    return pl.pallas_call(
        body, grid=(nt,),
        in_specs=[pl.BlockSpec((tm, dm), lambda i: (i, 0)), pl.BlockSpec((1, dm), lambda i: (0, 0)),
                  pl.BlockSpec((nc, tm, dm), lambda i: (0, i, 0)), pl.BlockSpec((tm, dm), lambda i: (i, 0))],
        out_specs=[pl.BlockSpec((tm, dm), lambda i: (i, 0)), pl.BlockSpec((1, dm), lambda i: (0, 0))],
        out_shape=[jax.ShapeDtypeStruct((seq, dm), F32), jax.ShapeDtypeStruct((1, dm), F32)],
        compiler_params=_cparams(("arbitrary",)), name=name,
    )(x, g, parts, dh)


def rope_tables(pos_col, inv_freq, tm):
    seq = pos_col.shape[0]

    def body(p_ref, f_ref, c_ref, s_ref):
        ang = p_ref[...].astype(F32) * f_ref[...]
        lane = lax.broadcasted_iota(jnp.int32, ang.shape, 1)
        c_ref[...] = jnp.cos(ang)
        s_ref[...] = jnp.where(lane < HEAD // 2, -jnp.sin(ang), jnp.sin(ang))

    return pl.pallas_call(
        body, grid=(seq // tm,),
        in_specs=[pl.BlockSpec((tm, 1), lambda i: (i, 0)), pl.BlockSpec((1, HEAD), lambda i: (0, 0))],
        out_specs=[pl.BlockSpec((tm, HEAD), lambda i: (i, 0))] * 2,
        out_shape=[jax.ShapeDtypeStruct((seq, HEAD), F32)] * 2,
        compiler_params=_cparams(("arbitrary",)), name="rope_tables",
    )(pos_col, inv_freq)


def loss_and_grad(y, target, tm):
    seq, dm = y.shape

    def body(y_ref, t_ref, l_ref, dy_ref):
        i = pl.program_id(0)
        err = y_ref[...] - t_ref[...]
        dy_ref[...] = err * (1.0 / dm)
        part = 0.5 * jnp.sum(jnp.mean(err * err, axis=-1, keepdims=True), axis=0, keepdims=True)
        part = jnp.broadcast_to(part, (1, LANE))

        @pl.when(i == 0)
        def _():
            l_ref[...] = part

        @pl.when(i > 0)
        def _():
            l_ref[...] += part

    return pl.pallas_call(
        body, grid=(seq // tm,),
        in_specs=[pl.BlockSpec((tm, dm), lambda i: (i, 0))] * 2,
        out_specs=[pl.BlockSpec((1, LANE), lambda i: (0, 0)), pl.BlockSpec((tm, dm), lambda i: (i, 0))],
        out_shape=[jax.ShapeDtypeStruct((1, LANE), F32), jax.ShapeDtypeStruct((seq, dm), F32)],
        compiler_params=_cparams(("arbitrary",)), name="loss_and_grad",
    )(y, target)


def add2(a, b, name):
    r, c = a.shape
    rows = min(r, 512)

    def body(a_ref, b_ref, o_ref):
        o_ref[...] = a_ref[...] + b_ref[...]

    spec = pl.BlockSpec((rows, c), lambda i: (i, 0))
    return pl.pallas_call(body, grid=(r // rows,), in_specs=[spec, spec], out_specs=spec,
                          out_shape=jax.ShapeDtypeStruct((r, c), F32), compiler_params=_cparams(("arbitrary",)),
                          name=name)(a, b)


def adamw(w, g, m, v, rows, name):
    r, c = w.shape

    def body(w_ref, g_ref, m_ref, v_ref, d_ref, nm_ref, nv_ref):
        gv = g_ref[...]
        nm = ADAM_B1 * m_ref[...] + (1.0 - ADAM_B1) * gv
        nv = ADAM_B2 * v_ref[...] + (1.0 - ADAM_B2) * (gv * gv)
        m_hat = nm / (1.0 - ADAM_B1 ** ADAM_STEP)
        v_hat = nv / (1.0 - ADAM_B2 ** ADAM_STEP)
        d_ref[...] = -ADAM_LR * (m_hat / (jnp.sqrt(v_hat) + ADAM_EPS) + ADAM_WD * w_ref[...])
        nm_ref[...] = nm
        nv_ref[...] = nv

    spec = pl.BlockSpec((rows, c), lambda i: (i, 0))
    return pl.pallas_call(
        body, grid=(r // rows,), in_specs=[spec] * 4, out_specs=[spec] * 3,
        out_shape=[jax.ShapeDtypeStruct((r, c), F32)] * 3,
        compiler_params=_cparams(("arbitrary",)), name=name,
    )(w, g, m, v)


def sum_layers(recvs, out_rows, out_cols, rows, name):
    depth = len(recvs)
    n, _, c_in = recvs[0].shape

    def body(*refs):
        o_ref = refs[depth]
        for l in range(depth):
            @pl.when(pl.program_id(0) == l)
            def _():
                acc = refs[l][0].astype(F32)
                for j in range(1, n):
                    acc = acc + refs[l][j].astype(F32)
                o_ref[0] = acc[:, :out_cols]

    return pl.pallas_call(
        body, grid=(depth, out_rows // rows),
        in_specs=[pl.BlockSpec((n, rows, c_in), lambda ll, i, l=l: (0, jnp.where(ll == l, i, 0), 0))
                  for l in range(depth)],
        out_specs=pl.BlockSpec((1, rows, out_cols), lambda ll, i: (ll, i, 0)),
        out_shape=jax.ShapeDtypeStruct((depth, out_rows, out_cols), F32),
        compiler_params=_cparams(("arbitrary", "arbitrary")), name=name,
    )(*recvs)


def _heads(t, n):
    return t.reshape(t.shape[0], n, HEAD).transpose(1, 0, 2)


def _unheads(t):
    return t.transpose(1, 0, 2).reshape(t.shape[1], t.shape[0] * HEAD)


def _mkv_heads(mkv):
    mw = N_MEMH * HEAD
    return _heads(mkv[:, :mw], N_MEMH), _heads(mkv[:, mw:], N_MEMH)


def _mix_args(sv, tabs, sw):
    mk, mv = _mkv_heads(sv["mkv"])
    ct, st = tabs
    conv_args = [(sv["a"], HALO, True), (sv["gate"], HALO, True), (sw["conv_w"], PARAM, True),
                 (sw["conv_b"], PARAM, True), (sw["conv_ln_g"], PARAM, True), (sw["conv_ln_b"], PARAM, True)]
    swa_args = [(sv["q"], TILE, True), (sv["k"], HALO, True), (sv["v"], HALO, True), (ct, HALO, False),
                (st, HALO, False), (sw["swa_q_norm"], PARAM, True), (sw["swa_k_norm"], PARAM, True),
                (sw["swa_sinks"], PARAM, True)]
    mem_args = [(sv["qm"], TILE, True), (mk, PARAM, True), (mv, PARAM, True), (sw["mem_q_norm"], PARAM, True),
                (sw["mem_k_norm"], PARAM, True)]
    return conv_args, swa_args, mem_args


def _proj_args(h1, sw, bw):
    return [(h1, TILE, True), (sw["mix_norm"], PARAM, True), (bw["w_in"], PARAM, True)]


def _join_args(sv, bw):
    return [(sv["h1"], TILE, True), (sv["y_conv"], TILE, True), (sv["y_swa"], TILE, True), (sv["y_mem"], TILE, True),
            (bw["w_out"], PARAM, True)]


def layer_fwd(x, mem, tabs, bw, sent, sw, tm, l, depth):
    seq, dm = x.shape
    tag = f"_l{l}"
    sv = dict(x=x)
    sv["h1"], got1 = ffn_fwd(x, sw["ffn1_norm"], bw["ffn1_w1"], bw["ffn1_w3"], bw["ffn1_w2"], tm, "ffn1_fwd" + tag,
                             gather_items(sent, l, GATHER_LATE))
    bw = {**bw, **kernel_layouts(got1, GATHER_LATE)}
    items2 = gather_items(sent, l + 1, GATHER_EARLY) if l + 1 < depth else ()
    split_outs = [((seq, CONV_CH), F32), ((seq, CONV_CH), F32), ((N_Q, seq, HEAD), F32), ((N_KV, seq, HEAD), F32),
                  ((N_KV, seq, HEAD), F32), ((N_MEMH, seq, HEAD), F32)]
    sv["a"], sv["gate"], sv["q"], sv["k"], sv["v"], sv["qm"] = seq_fwd(
        f_proj_split, _proj_args(sv["h1"], sw, bw), split_outs, tm, tm, "proj_in_fwd" + tag)
    ml = mem.shape[0]
    (sv["mkv"],) = seq_fwd(f_proj_in, [(mem, TILE, False), (sw["mem_norm"], PARAM, True), (bw["w_mem_kv"], PARAM, True)],
                           [((ml, bw["w_mem_kv"].shape[1]), F32)], ml, ml, "mem_kv_fwd" + tag)
    conv_args, swa_args, mem_args = _mix_args(sv, tabs, sw)
    (sv["y_conv"],) = seq_fwd(f_conv, conv_args, [((seq, CONV_CH), F32)], tm, 32, "conv_fwd" + tag)
    (sv["y_swa"],) = seq_fwd(f_swa, swa_args, [((N_Q, seq, HEAD), F32)], tm, BLOCK, "swa_fwd" + tag)
    (sv["y_mem"],) = seq_fwd(f_mem, mem_args, [((N_MEMH, seq, HEAD), F32)], tm, tm, "mem_attn_fwd" + tag)
    (sv["h2"],) = seq_fwd(f_out_join, _join_args(sv, bw), [((seq, dm), F32)], tm, tm, "out_proj_fwd" + tag)
    sv["h3"], got2 = ffn_fwd(sv["h2"], sw["ffn2_norm"], bw["ffn2_w1"], bw["ffn2_w3"], bw["ffn2_w2"], tm,
                             "ffn2_fwd" + tag, items2)
    (xo,) = seq_fwd(f_rms, [(sv["h3"], TILE, True), (sw["final_norm"], PARAM, True)], [((seq, dm), F32)], tm, tm,
                    "final_norm_fwd" + tag)
    return xo, sv, bw, kernel_layouts(got2, GATHER_EARLY)


def layer_bwd(dxo, sv, mem, tabs, bw, sw, tm, l, items=()):
    tag = f"_l{l}"
    gb, gs = {}, {}
    dh3, gs["final_norm"] = seq_bwd(f_rms, [(sv["h3"], TILE, True), (sw["final_norm"], PARAM, True)], [dxo], tm, tm,
                                    "final_norm_bwd" + tag)
    parts2, gb["ffn2_w1"], gb["ffn2_w3"], gb["ffn2_w2"], got = ffn_bwd_chunks(
        sv["h2"], sw["ffn2_norm"], dh3, bw["ffn2_w1"], bw["ffn2_w3"], bw["ffn2_w2"], tm, "ffn2_bwd" + tag, items)
    dh2, gs["ffn2_norm"] = ffn_bwd_norm(sv["h2"], sw["ffn2_norm"], parts2, dh3, tm, "ffn2_norm_bwd" + tag)
    dh1_a, dy_conv, dy_swa, dy_mem, gb["w_out"] = seq_bwd(f_out_join, _join_args(sv, bw), [dh2], tm // 2, tm // 2,
                                                           "out_proj_bwd" + tag)
    conv_args, swa_args, mem_args = _mix_args(sv, tabs, sw)
    da, dgate, gs["conv_w"], gs["conv_b"], gs["conv_ln_g"], gs["conv_ln_b"] = seq_bwd(
        f_conv, conv_args, [dy_conv], tm, 32, "conv_bwd" + tag)
    dq, dk, dv, gs["swa_q_norm"], gs["swa_k_norm"], gs["swa_sinks"] = seq_bwd(
        f_swa, swa_args, [dy_swa], tm, BLOCK, "swa_bwd" + tag)
    dqm, dmk, dmv, gs["mem_q_norm"], gs["mem_k_norm"] = seq_bwd(f_mem, mem_args, [dy_mem], tm, tm, "mem_attn_bwd" + tag)
    dmkv = jnp.concatenate([_unheads(dmk), _unheads(dmv)], axis=-1)
    ml = mem.shape[0]
    gs["mem_norm"], gb["w_mem_kv"] = seq_bwd(
        f_proj_in, [(mem, TILE, False), (sw["mem_norm"], PARAM, True), (bw["w_mem_kv"], PARAM, True)], [dmkv], ml, ml,
        "mem_kv_bwd" + tag)
    dh1_b, gs["mix_norm"], gb["w_in"] = seq_bwd(f_proj_split, _proj_args(sv["h1"], sw, bw),
                                                 [da, dgate, dq, dk, dv, dqm], tm // 2, tm // 2, "proj_in_bwd" + tag)
    dh1 = add2(dh1_a, dh1_b, "add_dh1" + tag)
    parts1, gb["ffn1_w1"], gb["ffn1_w3"], gb["ffn1_w2"], got_late = ffn_bwd_chunks(
        sv["x"], sw["ffn1_norm"], dh1, bw["ffn1_w1"], bw["ffn1_w3"], bw["ffn1_w2"], tm, "ffn1_bwd" + tag,
        scatter_items(gb, GATHER_LATE))
    dx, gs["ffn1_norm"] = ffn_bwd_norm(sv["x"], sw["ffn1_norm"], parts1, dh1, tm, "ffn1_norm_bwd" + tag)
    return dx, gb, gs, got, got_late


FFN_UP = ("ffn1_w1", "ffn1_w3", "ffn2_w1", "ffn2_w3")
FFN_DOWN = ("ffn1_w2", "ffn2_w2")
ROW_SHARDED = ("w_mem_kv", "w_out")
GATHER_EARLY = ("ffn1_w1", "ffn1_w3", "ffn1_w2")
GATHER_LATE = ("w_in", "w_mem_kv", "w_out", "ffn2_w1", "ffn2_w3", "ffn2_w2")
BIG = GATHER_EARLY + GATHER_LATE
SMALL = ("ffn1_norm", "mix_norm", "conv_b", "conv_ln_g", "conv_ln_b", "swa_q_norm", "swa_k_norm", "swa_sinks",
         "mem_norm", "mem_q_norm", "mem_k_norm", "ffn2_norm", "final_norm")
WEIGHTS = ("ffn1_norm", "ffn1_w1", "ffn1_w3", "ffn1_w2", "mix_norm", "w_in", "conv_w", "conv_b", "conv_ln_g",
           "conv_ln_b", "swa_q_norm", "swa_k_norm", "swa_sinks", "mem_norm", "w_mem_kv", "mem_q_norm", "mem_k_norm",
           "w_out", "ffn2_norm", "ffn2_w1", "ffn2_w3", "ffn2_w2", "final_norm")


def _round_up(n, m):
    return -(-n // m) * m


def _row_block(rows, target, mult=8):
    best = rows
    for cand in range(mult, min(rows, target) + 1, mult):
        if rows % cand == 0:
            best = cand
    return best


def _pad_rows(flat, cols, mult):
    n = flat.shape[0]
    rows = _round_up(-(-n // cols), mult)
    return jnp.pad(flat, (0, rows * cols - n)).reshape(rows, cols)


def send_form(w):
    out = {}
    for n in BIG:
        t = w[n].astype(BF)
        if n in FFN_UP:
            t = jnp.pad(t, ((0, 0), (0, 0), (0, _round_up(t.shape[2], LANE) - t.shape[2])))
        elif n in FFN_DOWN:
            t = jnp.pad(t, ((0, 0), (0, _round_up(t.shape[1], LANE) - t.shape[1]), (0, 0)))
        out[n] = t
    return out


def gather_items(sent, l, names):
    return [gather_cols_item(sent[n][l]) if n in FFN_UP else gather_item(sent[n][l]) for n in names]


def kernel_layouts(got, names):
    out = {}
    for n, t in zip(names, got):
        if n in FFN_DOWN:
            t = t.reshape(N_DEV // 2, 2 * t.shape[1], t.shape[2])
        elif n == "w_in":
            t = t.transpose(1, 0, 2).reshape(t.shape[1], N_DEV * t.shape[2])
        elif n in ROW_SHARDED:
            t = t.reshape(N_DEV * t.shape[1], t.shape[2])
        out[n] = t
    return out


def scatter_items(gb, names):
    items = []
    for n in names:
        t = gb[n]
        if n in FFN_UP:
            items.append(scatter_cols_item(t))
            continue
        if n in FFN_DOWN:
            t = t.reshape(N_DEV, t.shape[1] // 2, t.shape[2])
        elif n == "w_in":
            t = t.reshape(t.shape[0], N_DEV, t.shape[1] // N_DEV).transpose(1, 0, 2).astype(BF)
        else:
            t = t.reshape(N_DEV, t.shape[0] // N_DEV, t.shape[1]).astype(BF)
        items.append(scatter_item(t))
    return items


def small_layer_params(w, l):
    sw = {n: w[n][l][None, :] for n in SMALL if n != "swa_sinks"}
    sw["swa_sinks"] = jnp.pad(w["swa_sinks"][l], (0, LANE - N_Q))[None, :]
    sw["conv_w"] = jnp.pad(w["conv_w_full"][l], ((0, 1), (0, 0)))
    return sw


def kernel(x, mem, positions, ffn1_norm, ffn1_w1, ffn1_w3, ffn1_w2, mix_norm, w_in, conv_w, conv_b, conv_ln_g, conv_ln_b, swa_q_norm, swa_k_norm, swa_sinks, mem_norm, w_mem_kv, mem_q_norm, mem_k_norm, w_out, ffn2_norm, ffn2_w1, ffn2_w3, ffn2_w2, final_norm, loss_target, m_ffn1_norm, m_ffn1_w1, m_ffn1_w3, m_ffn1_w2, m_mix_norm, m_w_in, m_conv_w, m_conv_b, m_conv_ln_g, m_conv_ln_b, m_swa_q_norm, m_swa_k_norm, m_swa_sinks, m_mem_norm, m_w_mem_kv, m_mem_q_norm, m_mem_k_norm, m_w_out, m_ffn2_norm, m_ffn2_w1, m_ffn2_w3, m_ffn2_w2, m_final_norm, v_ffn1_norm, v_ffn1_w1, v_ffn1_w3, v_ffn1_w2, v_mix_norm, v_w_in, v_conv_w, v_conv_b, v_conv_ln_g, v_conv_ln_b, v_swa_q_norm, v_swa_k_norm, v_swa_sinks, v_mem_norm, v_w_mem_kv, v_mem_q_norm, v_mem_k_norm, v_w_out, v_ffn2_norm, v_ffn2_w1, v_ffn2_w3, v_ffn2_w2, v_final_norm):
    loc = locals()
    w = {n: loc[n] for n in WEIGHTS}
    m = {n: loc["m_" + n] for n in WEIGHTS}
    v = {n: loc["v_" + n] for n in WEIGHTS}
    depth = ffn1_norm.shape[0]
    seq = x.shape[1]
    tm = min(512, seq)
    me = 4 * lax.axis_index("x") + 2 * lax.axis_index("y") + lax.axis_index("c")
    xs, mems, target = x[0], mem[0], loss_target[0]

    sent = send_form(w)
    cw = conv_w.shape[2]
    conv_rows = _pad_rows(conv_w.reshape(-1), LANE, 8)
    got = comm_call(gather_items(sent, 0, GATHER_EARLY) + [gather_item(conv_rows)], "gather_l0")
    bw = kernel_layouts(got[:-1], GATHER_EARLY)
    conv_full = got[-1].reshape(N_DEV, -1)[:, :conv_w.size].reshape((N_DEV,) + conv_w.shape)
    small = {n: w[n] for n in SMALL}
    small["conv_w_full"] = conv_full.transpose(1, 2, 0, 3).reshape(depth, CONV_W, N_DEV * cw)

    inv = ROPE_THETA ** (-jnp.arange(0, HEAD, 2, dtype=F32) / HEAD)
    tabs = rope_tables(positions[0].reshape(-1, 1), jnp.concatenate([inv, inv])[None, :], tm)
    saved, bws = [], []
    h = xs
    for l in range(depth):
        h, sv, bw_all, bw = layer_fwd(h, mems, tabs, bw, sent, small_layer_params(small, l), tm, l, depth)
        saved.append(sv)
        bws.append(bw_all)
    loss_sum, dh = loss_and_grad(h, target, tm)
    loss = lax.psum(loss_sum[0, 0], ("x", "y", "c"))

    recv = [dict() for _ in range(depth)]
    gss = [None] * depth
    items = ()
    for l in reversed(range(depth)):
        dh, gb, gss[l], got, got_late = layer_bwd(dh, saved[l], mems, tabs, bws[l], small_layer_params(small, l), tm,
                                                  l, items)
        if items:
            recv[l + 1].update(zip(GATHER_EARLY, got))
        recv[l].update(zip(GATHER_LATE, got_late))
        items = scatter_items(gb, GATHER_EARLY)
    recv[0].update(zip(GATHER_EARLY, comm_call(items, "grad_exchange_l0")))

    g_all = {}
    for n in BIG:
        a, b = w[n].shape[1:]
        blocks = [recv[l][n] for l in range(depth)]
        if n in FFN_DOWN:
            rows = _row_block(a, 256, BF16_ROWS)
        else:
            rows = _row_block(a, 256)
        g_all[n] = sum_layers(blocks, a, b, rows, "grad_sum_" + n)

    small_names = SMALL + ("conv_w",)
    sizes = {n: (w[n].shape[1] if n != "conv_w" else CONV_W * N_DEV * cw) for n in small_names}
    flat = []
    for l in range(depth):
        for n in small_names:
            t = gss[l][n]
            if n == "swa_sinks":
                t = t[:, :N_Q]
            elif n == "conv_w":
                t = t[:CONV_W]
            flat.append(t.reshape(-1))
    flat = _pad_rows(jnp.concatenate(flat), LANE, 8)
    (small_got,) = comm_call([gather_item(flat)], "small_grad_gather")
    summed = sum_layers([small_got], flat.shape[0], LANE, flat.shape[0], "small_grad_sum").reshape(-1)
    g_small = {n: [] for n in small_names}
    o = 0
    for l in range(depth):
        for n in small_names:
            g_small[n].append(summed[o:o + sizes[n]])
            o += sizes[n]
    for n in SMALL:
        g_all[n] = jnp.stack(g_small[n])
    gcw = jnp.stack(g_small["conv_w"]).reshape(depth, CONV_W, N_DEV, cw)
    g_all["conv_w"] = lax.dynamic_slice(gcw, (0, 0, me, 0), (depth, CONV_W, 1, cw)).reshape(depth, CONV_W, cw)

    outs = {"delta": {}, "new_m": {}, "new_v": {}}
    for n in BIG:
        shape = w[n].shape
        two_d = (shape[0] * shape[1], shape[2])
        res = adamw(w[n].reshape(two_d), g_all[n].reshape(two_d), m[n].reshape(two_d), v[n].reshape(two_d),
                    _row_block(two_d[0], 512), "adamw_" + n)
        for key, arr in zip(("delta", "new_m", "new_v"), res):
            outs[key][n] = arr.reshape(shape)

    def packed_small(t):
        return _pad_rows(jnp.concatenate([t[n].reshape(-1) for n in small_names]), LANE, 8)

    ps = [packed_small(t) for t in (w, g_all, m, v)]
    res = adamw(*ps, ps[0].shape[0], "adamw_small")
    for key, arr in zip(("delta", "new_m", "new_v"), res):
        fl, o = arr.reshape(-1), 0
        for n in small_names:
            outs[key][n] = fl[o:o + w[n].size].reshape(w[n].shape)
            o += w[n].size

    return (loss, dh[None], *[g_all[n] for n in WEIGHTS], *[outs["delta"][n] for n in WEIGHTS],
            *[outs["new_m"][n] for n in WEIGHTS], *[outs["new_v"][n] for n in WEIGHTS])
```

```python
import jax
import jax.numpy as jnp
from jax import lax
from jax.experimental import pallas as pl
from jax.experimental.pallas import tpu as pltpu

F32 = jnp.float32
BF = jnp.bfloat16
EPS = 1e-6
HEAD = 64
N_Q, N_KV, N_MEMH = 6, 2, 4
CONV_CH, CONV_W = 384, 31
BLOCK = 128
ROPE_THETA = 10000.0
N_DEV = 8
V7X_VMEM_LIMIT = 56 * 1024 * 1024
LANE = 128
BF16_ROWS = 16

ADAM_LR, ADAM_B1, ADAM_B2, ADAM_EPS, ADAM_WD, ADAM_STEP = 0.001, 0.9, 0.999, 1e-08, 0.01, 10

TILE, HALO, PARAM = "tile", "halo", "param"
MESH = pl.DeviceIdType.MESH
ANY = pl.BlockSpec(memory_space=pl.ANY)


def _cparams(sem=None):
    kw = dict(vmem_limit_bytes=V7X_VMEM_LIMIT)
    if sem is not None:
        kw["dimension_semantics"] = sem
    return pltpu.CompilerParams(**kw)


def _dot(a, b, dims):
    return lax.dot_general(a, b, (dims, ((), ())), preferred_element_type=F32)


_NN, _NT, _TN = ((1,), (0,)), ((1,), (1,)), ((0,), (0,))


@jax.custom_vjp
def mm(a, b):
    return _dot(a.astype(BF), b.astype(BF), _NN)


def _mm_fwd(a, b):
    return mm(a, b), (a, b)


def _mm_bwd(res, g):
    a, b = res
    gb = g.astype(BF)
    return _dot(gb, b.astype(BF), _NT), _dot(a.astype(BF), gb, _TN)


mm.defvjp(_mm_fwd, _mm_bwd)


@jax.custom_vjp
def mm_nt(a, b):
    return _dot(a.astype(BF), b.astype(BF), _NT)


def _mm_nt_fwd(a, b):
    return mm_nt(a, b), (a, b)


def _mm_nt_bwd(res, g):
    a, b = res
    gb = g.astype(BF)
    return _dot(gb, b.astype(BF), _NN), _dot(gb, a.astype(BF), _TN)


mm_nt.defvjp(_mm_nt_fwd, _mm_nt_bwd)


def rms(x, g):
    return x * lax.rsqrt(jnp.mean(x * x, axis=-1, keepdims=True) + EPS) * g


def _swap_matrix():
    i = lax.broadcasted_iota(jnp.int32, (HEAD, HEAD), 0)
    j = lax.broadcasted_iota(jnp.int32, (HEAD, HEAD), 1)
    src = jnp.where(j < HEAD // 2, j + HEAD // 2, j - HEAD // 2)
    return (i == src).astype(F32)


def rope(x, c, s, swap):
    xs = jnp.dot(x, swap, precision=lax.Precision.HIGHEST, preferred_element_type=F32)
    return x * c + xs * s


SUBLANES = 8


def _row_shifts(t, rows):
    return [t] + [t[b:b + rows + 24] for b in range(1, SUBLANES)]


CONV_ROWS = 64


def _window(shifts, offset, start, rows):
    base = offset - offset % SUBLANES + start
    return shifts[offset % SUBLANES][base:base + rows]


def _row_blocks(rows):
    return [(r, min(CONV_ROWS, rows - r)) for r in range(0, rows, CONV_ROWS)]


def _conv_taps(gs, w, tm, first_offset=2, step=1):
    wr = [w[j:j + 1, :] for j in range(CONV_W)]
    blocks = []
    for r, n in _row_blocks(tm):
        acc = wr[0] * _window(gs, first_offset, r, n)
        for j in range(1, CONV_W):
            acc = acc + wr[j] * _window(gs, first_offset + step * j, r, n)
        blocks.append(acc)
    return jnp.concatenate(blocks, axis=0)


@jax.custom_vjp
def causal_dw_conv(g, w):
    tm = g.shape[0] - 32
    return _conv_taps(_row_shifts(g, tm), w, tm)


def _conv_fwd(g, w):
    tm = g.shape[0] - 32
    gs = _row_shifts(g, tm)
    return _conv_taps(gs, w, tm), (gs, w)


def _conv_bwd(res, dc):
    gs, w = res
    tm, ch = dc.shape
    z = jnp.zeros((32, ch), F32)
    ds = _row_shifts(jnp.concatenate([z, dc, z], axis=0), tm + 32)
    dg = _conv_taps(ds, w, tm + 32, first_offset=30, step=-1)
    rows = lax.broadcasted_iota(jnp.int32, (32, 1), 0)
    dcb = [dc[r:r + n] for r, n in _row_blocks(tm)]
    dw = jnp.zeros((32, ch), F32)
    for j in range(CONV_W):
        part = jnp.zeros((SUBLANES, ch), F32)
        for (r, n), d in zip(_row_blocks(tm), dcb):
            prod = d * _window(gs, 2 + j, r, n)
            for s in range(0, n, SUBLANES):
                part = part + prod[s:s + SUBLANES]
        dw = dw + jnp.where(rows == j, jnp.sum(part, axis=0, keepdims=True), 0.0)
    return dg, dw


causal_dw_conv.defvjp(_conv_fwd, _conv_bwd)


def f_rms(first, x, g):
    return (rms(x, g),)


def f_proj_in(first, h, g, w):
    return (mm(rms(h, g), w),)


def _to_heads(t, n):
    return jnp.stack([t[:, h * HEAD:(h + 1) * HEAD] for h in range(n)])


def f_proj_split(first, h, g, w):
    p = mm(rms(h, g), w)
    o, outs = 0, []
    for width, n in ((CONV_CH, 0), (CONV_CH, 0), (N_Q * HEAD, N_Q), (N_KV * HEAD, N_KV), (N_KV * HEAD, N_KV),
                     (N_MEMH * HEAD, N_MEMH)):
        t = p[:, o:o + width]
        outs.append(_to_heads(t, n) if n else t)
        o += width
    return tuple(outs)


def f_out_join(first, h, y_conv, y_swa, y_mem, w):
    y = jnp.concatenate([y_conv] + [y_swa[i] for i in range(N_Q)] + [y_mem[i] for i in range(N_MEMH)], axis=1)
    return (h + mm(y, w),)


def f_conv(first, a, gate, w, b, lg, lb):
    keep = 1.0 - first
    av = jnp.concatenate([a[0] * keep, a[1]], axis=0)
    gv = jnp.concatenate([gate[0], gate[1]], axis=0)
    glu = av * jax.nn.sigmoid(gv)
    c = causal_dw_conv(glu, w) + b
    mu = jnp.mean(c, axis=-1, keepdims=True)
    var = jnp.mean(jnp.square(c - mu), axis=-1, keepdims=True)
    z = (c - mu) * lax.rsqrt(var + EPS) * lg + lb
    return (z * jax.nn.sigmoid(z),)


def _softmax_with_extra(s, extra):
    m = jnp.max(s, axis=-1, keepdims=True)
    if extra is not None:
        m = jnp.maximum(m, extra)
    m = lax.stop_gradient(m)
    e = jnp.exp(s - m)
    den = jnp.sum(e, axis=-1, keepdims=True)
    if extra is not None:
        den = den + jnp.exp(extra - m)
    return e / den


def f_swa(first, q, k, v, ct, st, qn, kn, sinks):
    tm = q.shape[1]
    nb = tm // BLOCK
    g = N_Q // N_KV
    swap = _swap_matrix()
    c_all = jnp.concatenate([ct[0], ct[1]], axis=0)
    s_all = jnp.concatenate([st[0], st[1]], axis=0)
    qi = lax.broadcasted_iota(jnp.int32, (g * BLOCK, 2 * BLOCK), 0)
    kj = lax.broadcasted_iota(jnp.int32, (g * BLOCK, 2 * BLOCK), 1)
    qpos = jnp.where(qi >= 2 * BLOCK, qi - 2 * BLOCK, jnp.where(qi >= BLOCK, qi - BLOCK, qi)) + BLOCK
    rel = qpos - kj
    band = (rel >= 0) & (rel < BLOCK)
    band_first = band & ((kj >= BLOCK) | (first < 0.5))
    lane = lax.broadcasted_iota(jnp.int32, (g * BLOCK, LANE), 1)
    hrow = lax.broadcasted_iota(jnp.int32, (g * BLOCK, LANE), 0)
    head_in_group = jnp.where(hrow >= 2 * BLOCK, 2, jnp.where(hrow >= BLOCK, 1, 0))
    qr = [rope(rms(q[h], qn), ct[1], st[1], swap) for h in range(N_Q)]
    outs = [[None] * nb for _ in range(N_Q)]
    for hk in range(N_KV):
        kk = rope(rms(jnp.concatenate([k[0][hk], k[1][hk]], axis=0), kn), c_all, s_all, swap)
        vv = jnp.concatenate([v[0][hk], v[1][hk]], axis=0)
        sel = (lane == head_in_group + hk * g).astype(F32)
        sink_col = jnp.sum(sel * sinks, axis=1, keepdims=True)
        for j in range(nb):
            keys = kk[j * BLOCK:(j + 2) * BLOCK]
            vals = vv[j * BLOCK:(j + 2) * BLOCK]
            qs = jnp.concatenate([qr[hk * g + gg][j * BLOCK:(j + 1) * BLOCK] for gg in range(g)], axis=0)
            s = mm_nt(qs, keys) * (HEAD ** -0.5)
            s = jnp.where(band_first if j == 0 else band, s, -1e30)
            o = mm(_softmax_with_extra(s, sink_col), vals)
            for gg in range(g):
                outs[hk * g + gg][j] = o[gg * BLOCK:(gg + 1) * BLOCK]
    return (jnp.stack([jnp.concatenate(outs[h], axis=0) for h in range(N_Q)]),)


def f_mem(first, qm, mk, mv, qn, kn):
    outs = []
    for h in range(N_MEMH):
        qh = rms(qm[h], qn)
        kh = rms(mk[h], kn)
        s = mm_nt(qh, kh) * (HEAD ** -0.5)
        outs.append(mm(_softmax_with_extra(s, None), mv[h]))
    return (jnp.stack(outs),)


def _seq_len(arr):
    return arr.shape[0] if arr.ndim == 2 else arr.shape[1]


def _tile_spec(arr, rows, imap):
    if arr.ndim == 2:
        return pl.BlockSpec((rows, arr.shape[1]), lambda i: (imap(i), 0))
    return pl.BlockSpec((arr.shape[0], rows, arr.shape[2]), lambda i: (0, imap(i), 0))


def _full_spec(arr):
    nd = arr.ndim
    return pl.BlockSpec(arr.shape, lambda i: (0,) * nd)


def _in_specs(args, tm, hl, nt):
    specs = []
    ratio = tm // hl
    cur = lambda i: jnp.minimum(i, nt - 1)
    prev = lambda i: jnp.maximum(jnp.minimum(i, nt - 1) * ratio - 1, 0)
    for arr, kind, _ in args:
        if kind == TILE:
            specs.append(_tile_spec(arr, tm, cur))
        elif kind == HALO:
            specs.append(_tile_spec(arr, hl, prev))
            specs.append(_tile_spec(arr, tm, cur))
        else:
            specs.append(_full_spec(arr))
    return specs


def _operands(args):
    ops = []
    for arr, kind, _ in args:
        ops.append(arr)
        if kind == HALO:
            ops.append(arr)
    return ops


def _load_values(args, refs):
    vals, k = [], 0
    for arr, kind, _ in args:
        if kind == HALO:
            vals.append((refs[k][...].astype(F32), refs[k + 1][...].astype(F32)))
            k += 2
        else:
            vals.append(refs[k][...].astype(F32))
            k += 1
    return vals


def seq_fwd(f, args, outs, tm, hl, name):
    out_shape = [jax.ShapeDtypeStruct(s, d) for s, d in outs]
    nt = _seq_len(out_shape[0]) // tm
    n_in = len(_operands(args))

    def body(*refs):
        first = (pl.program_id(0) == 0).astype(F32)
        res = f(first, *_load_values(args, refs[:n_in]))
        for r, o in zip(refs[n_in:], res):
            r[...] = o.astype(r.dtype)

    cur = lambda i: i
    return pl.pallas_call(
        body, grid=(nt,), in_specs=_in_specs(args, tm, hl, nt),
        out_specs=[_tile_spec(o, tm, cur) for o in out_shape], out_shape=out_shape,
        compiler_params=_cparams(("arbitrary",)), name=name,
    )(*_operands(args))


def seq_bwd(f, args, douts, tm, hl, name, add_to_first=None):
    seq = _seq_len(douts[0])
    nt = seq // tm
    lag = any(kind == HALO and diff for _, kind, diff in args)
    steps = nt + 1 if lag else nt
    n_in = len(_operands(args))
    n_do = len(douts)
    dargs = [(arr, kind) for arr, kind, diff in args if diff]
    extra = [] if add_to_first is None else [add_to_first]
    assert not extra or (dargs[0][1] == TILE and not lag)

    def body(*refs):
        in_refs = refs[:n_in]
        do_refs = refs[n_in:n_in + n_do]
        add_refs = refs[n_in + n_do:n_in + n_do + len(extra)]
        refs = refs[len(extra):]
        g_refs = refs[n_in + n_do:n_in + n_do + len(dargs)]
        carries = refs[n_in + n_do + len(dargs):]
        i = pl.program_id(0)
        first = (i == 0).astype(F32)

        def compute():
            vals = _load_values(args, in_refs)
            dvals = [v for v, (_, _, diff) in zip(vals, args) if diff]

            def fd(*dv):
                it = iter(dv)
                return f(first, *[next(it) if diff else v for v, (_, _, diff) in zip(vals, args)])

            _, vjp = jax.vjp(fd, *dvals)
            grads = vjp(tuple(r[...].astype(F32) for r in do_refs))
            c = 0
            for pos, (gref, gval, (arr, kind)) in enumerate(zip(g_refs, grads, dargs)):
                if kind == TILE:
                    if pos == 0 and extra:
                        gval = gval + add_refs[0][...]
                    gref[...] = gval.astype(gref.dtype)
                elif kind == PARAM:
                    @pl.when(i == 0)
                    def _():
                        gref[...] = gval

                    @pl.when(i > 0)
                    def _():
                        gref[...] += gval
                else:
                    carry = carries[c]
                    c += 1
                    g_prev, g_cur = gval

                    @pl.when(i > 0)
                    def _():
                        gref[...] = carry[...]
                        if arr.ndim == 2:
                            gref[tm - hl:tm, :] += g_prev
                        else:
                            gref[:, tm - hl:tm, :] += g_prev

                    carry[...] = g_cur

        if lag:
            pl.when(i < nt)(compute)

            @pl.when(i == nt)
            def _():
                c = 0
                for gref, (arr, kind) in zip(g_refs, dargs):
                    if kind == HALO:
                        gref[...] = carries[c][...]
                        c += 1
        else:
            compute()

    cur = lambda i: jnp.minimum(i, nt - 1)
    lagged = lambda i: jnp.maximum(i - 1, 0)
    out_shape, out_specs, scratch = [], [], []
    for arr, kind in dargs:
        out_shape.append(jax.ShapeDtypeStruct(arr.shape, F32))
        if kind == PARAM:
            out_specs.append(_full_spec(arr))
        else:
            out_specs.append(_tile_spec(arr, tm, lagged if kind == HALO else cur))
            if kind == HALO:
                blk = (tm, arr.shape[1]) if arr.ndim == 2 else (arr.shape[0], tm, arr.shape[2])
                scratch.append(pltpu.VMEM(blk, F32))
    in_specs = _in_specs(args, tm, hl, nt) + [_tile_spec(d, tm, cur) for d in list(douts) + extra]
    return pl.pallas_call(
        body, grid=(steps,), in_specs=in_specs, out_specs=out_specs, out_shape=out_shape,
        scratch_shapes=scratch, compiler_params=_cparams(("arbitrary",)), name=name,
    )(*_operands(args), *douts, *extra)


class Item:
    def __init__(self, operand, out_shape, src, dst, two_level=False):
        self.operand, self.out_shape, self.src, self.dst, self.two_level = operand, out_shape, src, dst, two_level


def _dev(p):
    return 4 * p[0] + 2 * p[1] + p[2]


def gather_item(shard):
    return Item(shard, jax.ShapeDtypeStruct((N_DEV,) + shard.shape, shard.dtype),
                lambda r, peer: r, lambda r, s: r.at[_dev(s)], two_level=True)


def gather_cols_item(shard):
    d, w = shard.shape
    return Item(shard, jax.ShapeDtypeStruct((N_DEV // 2, d, 2 * w), shard.dtype),
                lambda r, peer: r, lambda r, s: r.at[2 * s[0] + s[1], :, pl.ds(s[2] * w, w)], two_level=True)


def scatter_item(blocks):
    return Item(blocks, jax.ShapeDtypeStruct(blocks.shape, blocks.dtype),
                lambda r, peer: r.at[_dev(peer)], lambda r, s: r.at[_dev(s)])


def scatter_cols_item(full):
    n, d, w2 = full.shape
    w = w2 // 2
    return Item(full, jax.ShapeDtypeStruct((N_DEV, d, w), full.dtype),
                lambda r, peer: r.at[2 * peer[0] + peer[1], :, pl.ds(peer[2] * w, w)], lambda r, s: r.at[_dev(s)])


def _comm_sems(items):
    n = len(items) * (N_DEV - 1)
    return [pltpu.SemaphoreType.DMA((n,)), pltpu.SemaphoreType.DMA((n,)), pltpu.SemaphoreType.DMA((len(items),))]


PASSED_ON = (3, 5, 7)


def _comm_copies(items, in_refs, out_refs, sems, x, y, c):
    send_sems, recv_sems, local_sems = sems
    me = (x, y, c)
    sibling = (x, y, 1 - c)
    local, first, passed = [], [], []

    def remote(t, k, src, dst, to):
        n = t * (N_DEV - 1) + k - 1
        return lambda: pltpu.make_async_remote_copy(
            src_ref=src(), dst_ref=dst(), send_sem=send_sems.at[n], recv_sem=recv_sems.at[n],
            device_id=(to[0], to[1], jnp.int32(to[2])), device_id_type=MESH)

    for t, it in enumerate(items):
        local.append(lambda t=t, it=it: pltpu.make_async_copy(it.src(in_refs[t], me), it.dst(out_refs[t], me),
                                                               local_sems.at[t]))
        for k in range(1, N_DEV):
            peer = (1 - x if k & 4 else x, 1 - y if k & 2 else y, 1 - c if k & 1 else c)
            if it.two_level and k in PASSED_ON:
                origin = (peer[0], peer[1], c)
                landed = lambda t=t, it=it, origin=origin: it.dst(out_refs[t], origin)
                passed.append((t, k, remote(t, k, landed, landed, sibling)))
            else:
                first.append((t, k, remote(t, k, lambda t=t, it=it, peer=peer: it.src(in_refs[t], peer),
                                           lambda t=t, it=it: it.dst(out_refs[t], me), peer)))
    return local, first, passed


def _on_my_core(fn):
    x, y, c = lax.axis_index("x"), lax.axis_index("y"), lax.axis_index("c")
    for cv in (0, 1):
        @pl.when(c == cv)
        def _():
            fn(x, y, cv)


def comm_start(items, in_refs, out_refs, sems):
    def go(x, y, c):
        local, first, _ = _comm_copies(items, in_refs, out_refs, sems, x, y, c)
        for make in local:
            make().start()
        for _, _, make in first:
            make().start()

    _on_my_core(go)


def comm_pass_on(items, in_refs, out_refs, sems):
    def go(x, y, c):
        _, first, passed = _comm_copies(items, in_refs, out_refs, sems, x, y, c)
        arrived = {(t, k): make for t, k, make in first}
        for t, k, make in passed:
            arrived[(t, k - 1)]().wait_recv()
            make().start()

    _on_my_core(go)


def comm_wait(items, in_refs, out_refs, sems):
    def go(x, y, c):
        local, first, passed = _comm_copies(items, in_refs, out_refs, sems, x, y, c)
        waited = {(t, k - 1) for t, k, _ in passed}
        for t, k, make in first:
            cp = make()
            if (t, k) not in waited:
                cp.wait_recv()
            cp.wait_send()
        for _, _, make in passed:
            cp = make()
            cp.wait_recv()
            cp.wait_send()
        for make in local:
            make().wait()

    _on_my_core(go)


def comm_call(items, name):
    n = len(items)

    def body(*refs):
        in_refs, out_refs, sems = refs[:n], refs[n:2 * n], refs[2 * n:]
        comm_start(items, in_refs, out_refs, sems)
        comm_pass_on(items, in_refs, out_refs, sems)
        comm_wait(items, in_refs, out_refs, sems)

    return pl.pallas_call(
        body, in_specs=[ANY] * n, out_specs=[ANY] * n, out_shape=[it.out_shape for it in items],
        scratch_shapes=_comm_sems(items), name=name,
    )(*[it.operand for it in items])


def ffn_fwd(x, g, w1, w3, w2, tm, name, items=()):
    seq, dm = x.shape
    nc, _, fc = w1.shape
    nt = seq // tm
    n = len(items)

    def body(*refs):
        x_ref, g_ref, w1_ref, w3_ref, w2_ref = refs[:5]
        c_in, (h_ref, a_ref, b_ref), c_out = refs[5:5 + n], refs[5 + n:8 + n], refs[8 + n:8 + 2 * n]
        xn_s, acc_s = refs[8 + 2 * n:10 + 2 * n]
        sems = refs[10 + 2 * n:]
        i, c = pl.program_id(0), pl.program_id(1)

        if n:
            @pl.when((i == 0) & (c == 0))
            def _():
                comm_start(items, c_in, c_out, sems)

        @pl.when(c == 0)
        def _():
            xn_s[...] = rms(x_ref[...], g_ref[...]).astype(BF)
            acc_s[...] = jnp.zeros_like(acc_s)

        xn = xn_s[...]
        a = _dot(xn, w1_ref[0], _NN)
        b = _dot(xn, w3_ref[0], _NN)
        a_ref[...] = a.astype(BF)
        b_ref[...] = b.astype(BF)
        hid = (a * jax.nn.sigmoid(a)) * b
        acc_s[...] += _dot(hid.astype(BF), w2_ref[0], _NN)

        @pl.when(c == nc - 1)
        def _():
            h_ref[...] = x_ref[...] + 0.5 * acc_s[...]

        if n:
            step = i * nc + c
            total = nt * nc

            @pl.when(step == max(1, (13 * total) // 16))
            def _():
                comm_pass_on(items, c_in, c_out, sems)

            @pl.when(step == total - 1)
            def _():
                comm_wait(items, c_in, c_out, sems)

    res = pl.pallas_call(
        body, grid=(nt, nc),
        in_specs=[pl.BlockSpec((tm, dm), lambda i, c: (i, 0)), pl.BlockSpec((1, dm), lambda i, c: (0, 0)),
                  pl.BlockSpec((1, dm, fc), lambda i, c: (c, 0, 0)), pl.BlockSpec((1, dm, fc), lambda i, c: (c, 0, 0)),
                  pl.BlockSpec((1, fc, dm), lambda i, c: (c, 0, 0))] + [ANY] * n,
        out_specs=[pl.BlockSpec((tm, dm), lambda i, c: (i, 0)), pl.BlockSpec((tm, fc), lambda i, c: (i, c)),
                   pl.BlockSpec((tm, fc), lambda i, c: (i, c))] + [ANY] * n,
        out_shape=[jax.ShapeDtypeStruct((seq, dm), F32), jax.ShapeDtypeStruct((seq, nc * fc), BF),
                   jax.ShapeDtypeStruct((seq, nc * fc), BF)] + [it.out_shape for it in items],
        scratch_shapes=[pltpu.VMEM((tm, dm), BF), pltpu.VMEM((tm, dm), F32)] + (_comm_sems(items) if n else []),
        compiler_params=_cparams(("arbitrary", "arbitrary")), name=name,
    )(x, g, w1, w3, w2, *[it.operand for it in items])
    return res[0], (res[1], res[2]), list(res[3:])


def ffn_bwd_chunks(x, g, dh, ups, w1, w3, w2, tm, name, items=()):
    seq, dm = x.shape
    nc, _, fc = w1.shape
    nt = seq // tm
    n = len(items)

    def body(*refs):
        x_ref, g_ref, dh_ref, a_ref, b_ref, w1_ref, w3_ref, w2_ref = refs[:8]
        c_in = refs[8:8 + n]
        dxn_ref, dw1_ref, dw3_ref, dw2_ref = refs[8 + n:12 + n]
        c_out = refs[12 + n:12 + 2 * n]
        a1_s, a3_s, a2_s = refs[12 + 2 * n:15 + 2 * n]
        sems = refs[15 + 2 * n:]
        c, i = pl.program_id(0), pl.program_id(1)

        if n:
            @pl.when((i == 0) & (c == 0))
            def _():
                comm_start(items, c_in, c_out, sems)

        @pl.when(i == 0)
        def _():
            a1_s[...] = jnp.zeros_like(a1_s)
            a3_s[...] = jnp.zeros_like(a3_s)
            a2_s[...] = jnp.zeros_like(a2_s)

        xn = rms(x_ref[...], g_ref[...]).astype(BF)
        dy = (0.5 * dh_ref[...]).astype(BF)
        w1v, w3v, w2v = w1_ref[0], w3_ref[0], w2_ref[0]
        a = a_ref[...].astype(F32)
        b = b_ref[...].astype(F32)
        sig = jax.nn.sigmoid(a)
        sa = a * sig
        dhid = _dot(dy, w2v, _NT)
        db = (dhid * sa).astype(BF)
        da = (dhid * b * (sig * (1.0 + a * (1.0 - sig)))).astype(BF)
        dxn_ref[0] = _dot(da, w1v, _NT) + _dot(db, w3v, _NT)
        a1_s[...] += _dot(xn, da, _TN)
        a3_s[...] += _dot(xn, db, _TN)
        a2_s[...] += _dot((sa * b).astype(BF), dy, _TN)

        @pl.when(i == nt - 1)
        def _():
            dw1_ref[0] = a1_s[...].astype(BF)
            dw3_ref[0] = a3_s[...].astype(BF)
            dw2_ref[0] = a2_s[...].astype(BF)

        if n:
            @pl.when((i == nt - 1) & (c == nc - 1))
            def _():
                comm_wait(items, c_in, c_out, sems)

    res = pl.pallas_call(
        body, grid=(nc, nt),
        in_specs=[pl.BlockSpec((tm, dm), lambda c, i: (i, 0)), pl.BlockSpec((1, dm), lambda c, i: (0, 0)),
                  pl.BlockSpec((tm, dm), lambda c, i: (i, 0)),
                  pl.BlockSpec((tm, fc), lambda c, i: (i, c)), pl.BlockSpec((tm, fc), lambda c, i: (i, c)),
                  pl.BlockSpec((1, dm, fc), lambda c, i: (c, 0, 0)), pl.BlockSpec((1, dm, fc), lambda c, i: (c, 0, 0)),
                  pl.BlockSpec((1, fc, dm), lambda c, i: (c, 0, 0))] + [ANY] * n,
        out_specs=[pl.BlockSpec((1, tm, dm), lambda c, i: (c, i, 0)),
                   pl.BlockSpec((1, dm, fc), lambda c, i: (c, 0, 0)), pl.BlockSpec((1, dm, fc), lambda c, i: (c, 0, 0)),
                   pl.BlockSpec((1, fc, dm), lambda c, i: (c, 0, 0))] + [ANY] * n,
        out_shape=[jax.ShapeDtypeStruct((nc, seq, dm), F32), jax.ShapeDtypeStruct((nc, dm, fc), BF),
                   jax.ShapeDtypeStruct((nc, dm, fc), BF), jax.ShapeDtypeStruct((nc, fc, dm), BF)]
                  + [it.out_shape for it in items],
        scratch_shapes=[pltpu.VMEM((dm, fc), F32), pltpu.VMEM((dm, fc), F32), pltpu.VMEM((fc, dm), F32)]
                       + (_comm_sems(items) if n else []),
        compiler_params=_cparams(("arbitrary", "arbitrary")), name=name,
    )(x, g, dh, ups[0], ups[1], w1, w3, w2, *[it.operand for it in items])
    return res[0], res[1], res[2], res[3], list(res[4:])


def ffn_bwd_norm(x, g, parts, dh, tm, name):
    seq, dm = x.shape
    nc = parts.shape[0]
    nt = seq // tm

    def body(x_ref, g_ref, p_ref, dh_ref, dx_ref, dg_ref):
        i = pl.program_id(0)
        dxn = p_ref[0]
        for c in range(1, nc):
            dxn = dxn + p_ref[c]
        _, vjp = jax.vjp(rms, x_ref[...], g_ref[...])
        dx, dg = vjp(dxn)
        dx_ref[...] = dx + dh_ref[...]

        @pl.when(i == 0)
        def _():
            dg_ref[...] = dg

        @pl.when(i > 0)
        def _():
            dg_ref[...] += dg

    return pl.pallas_call(
        body, grid=(nt,),
        in_specs=[pl.BlockSpec((tm, dm), lambda i: (i, 0)), pl.BlockSpec((1, dm), lambda i: (0, 0)),
                  pl.BlockSpec((nc, tm, dm), lambda i: (0, i, 0)), pl.BlockSpec((tm, dm), lambda i: (i, 0))],
        out_specs=[pl.BlockSpec((tm, dm), lambda i: (i, 0)), pl.BlockSpec((1, dm), lambda i: (0, 0))],
        out_shape=[jax.ShapeDtypeStruct((seq, dm), F32), jax.ShapeDtypeStruct((1, dm), F32)],
        compiler_params=_cparams(("arbitrary",)), name=name,
    )(x, g, parts, dh)


def rope_tables(pos_col, inv_freq, tm):
    seq = pos_col.shape[0]

    def body(p_ref, f_ref, c_ref, s_ref):
        ang = p_ref[...].astype(F32) * f_ref[...]
        lane = lax.broadcasted_iota(jnp.int32, ang.shape, 1)
        c_ref[...] = jnp.cos(ang)
        s_ref[...] = jnp.where(lane < HEAD // 2, -jnp.sin(ang), jnp.sin(ang))

    return pl.pallas_call(
        body, grid=(seq // tm,),
        in_specs=[pl.BlockSpec((tm, 1), lambda i: (i, 0)), pl.BlockSpec((1, HEAD), lambda i: (0, 0))],
        out_specs=[pl.BlockSpec((tm, HEAD), lambda i: (i, 0))] * 2,
        out_shape=[jax.ShapeDtypeStruct((seq, HEAD), F32)] * 2,
        compiler_params=_cparams(("arbitrary",)), name="rope_tables",
    )(pos_col, inv_freq)


def loss_and_grad(y, target, tm):
    seq, dm = y.shape

    def body(y_ref, t_ref, l_ref, dy_ref):
        i = pl.program_id(0)
        err = y_ref[...] - t_ref[...]
        dy_ref[...] = err * (1.0 / dm)
        part = 0.5 * jnp.sum(jnp.mean(err * err, axis=-1, keepdims=True), axis=0, keepdims=True)
        part = jnp.broadcast_to(part, (1, LANE))

        @pl.when(i == 0)
        def _():
            l_ref[...] = part

        @pl.when(i > 0)
        def _():
            l_ref[...] += part

    return pl.pallas_call(
        body, grid=(seq // tm,),
        in_specs=[pl.BlockSpec((tm, dm), lambda i: (i, 0))] * 2,
        out_specs=[pl.BlockSpec((1, LANE), lambda i: (0, 0)), pl.BlockSpec((tm, dm), lambda i: (i, 0))],
        out_shape=[jax.ShapeDtypeStruct((1, LANE), F32), jax.ShapeDtypeStruct((seq, dm), F32)],
        compiler_params=_cparams(("arbitrary",)), name="loss_and_grad",
    )(y, target)


def adamw(w, g, m, v, rows, name):
    r, c = w.shape

    def body(w_ref, g_ref, m_ref, v_ref, d_ref, nm_ref, nv_ref):
        gv = g_ref[...]
        nm = ADAM_B1 * m_ref[...] + (1.0 - ADAM_B1) * gv
        nv = ADAM_B2 * v_ref[...] + (1.0 - ADAM_B2) * (gv * gv)
        m_hat = nm / (1.0 - ADAM_B1 ** ADAM_STEP)
        v_hat = nv / (1.0 - ADAM_B2 ** ADAM_STEP)
        d_ref[...] = -ADAM_LR * (m_hat / (jnp.sqrt(v_hat) + ADAM_EPS) + ADAM_WD * w_ref[...])
        nm_ref[...] = nm
        nv_ref[...] = nv

    spec = pl.BlockSpec((rows, c), lambda i: (i, 0))
    return pl.pallas_call(
        body, grid=(r // rows,), in_specs=[spec] * 4, out_specs=[spec] * 3,
        out_shape=[jax.ShapeDtypeStruct((r, c), F32)] * 3,
        compiler_params=_cparams(("arbitrary",)), name=name,
    )(w, g, m, v)


def sum_layers(recvs, out_rows, out_cols, rows, name):
    depth = len(recvs)
    n, _, c_in = recvs[0].shape

    def body(*refs):
        o_ref = refs[depth]
        for l in range(depth):
            @pl.when(pl.program_id(0) == l)
            def _():
                acc = refs[l][0].astype(F32)
                for j in range(1, n):
                    acc = acc + refs[l][j].astype(F32)
                o_ref[0] = acc[:, :out_cols]

    return pl.pallas_call(
        body, grid=(depth, out_rows // rows),
        in_specs=[pl.BlockSpec((n, rows, c_in), lambda ll, i, l=l: (0, jnp.where(ll == l, i, 0), 0))
                  for l in range(depth)],
        out_specs=pl.BlockSpec((1, rows, out_cols), lambda ll, i: (ll, i, 0)),
        out_shape=jax.ShapeDtypeStruct((depth, out_rows, out_cols), F32),
        compiler_params=_cparams(("arbitrary", "arbitrary")), name=name,
    )(*recvs)


def _heads(t, n):
    return t.reshape(t.shape[0], n, HEAD).transpose(1, 0, 2)


def _unheads(t):
    return t.transpose(1, 0, 2).reshape(t.shape[1], t.shape[0] * HEAD)


def _mkv_heads(mkv):
    mw = N_MEMH * HEAD
    return _heads(mkv[:, :mw], N_MEMH), _heads(mkv[:, mw:], N_MEMH)


def _mix_args(sv, tabs, sw):
    mk, mv = _mkv_heads(sv["mkv"])
    ct, st = tabs
    conv_args = [(sv["a"], HALO, True), (sv["gate"], HALO, True), (sw["conv_w"], PARAM, True),
                 (sw["conv_b"], PARAM, True), (sw["conv_ln_g"], PARAM, True), (sw["conv_ln_b"], PARAM, True)]
    swa_args = [(sv["q"], TILE, True), (sv["k"], HALO, True), (sv["v"], HALO, True), (ct, HALO, False),
                (st, HALO, False), (sw["swa_q_norm"], PARAM, True), (sw["swa_k_norm"], PARAM, True),
                (sw["swa_sinks"], PARAM, True)]
    mem_args = [(sv["qm"], TILE, True), (mk, PARAM, True), (mv, PARAM, True), (sw["mem_q_norm"], PARAM, True),
                (sw["mem_k_norm"], PARAM, True)]
    return conv_args, swa_args, mem_args


def _proj_args(h1, sw, bw):
    return [(h1, TILE, True), (sw["mix_norm"], PARAM, True), (bw["w_in"], PARAM, True)]


def _join_args(sv, bw):
    return [(sv["h1"], TILE, True), (sv["y_conv"], TILE, True), (sv["y_swa"], TILE, True), (sv["y_mem"], TILE, True),
            (bw["w_out"], PARAM, True)]


def layer_fwd(x, mem, tabs, bw, sent, sw, tm, l, depth):
    seq, dm = x.shape
    tag = f"_l{l}"
    sv = dict(x=x)
    sv["h1"], sv["ups1"], got1 = ffn_fwd(x, sw["ffn1_norm"], bw["ffn1_w1"], bw["ffn1_w3"], bw["ffn1_w2"], tm,
                                         "ffn1_fwd" + tag, gather_items(sent, l, GATHER_LATE))
    bw = {**bw, **kernel_layouts(got1, GATHER_LATE)}
    items2 = gather_items(sent, l + 1, GATHER_EARLY) if l + 1 < depth else ()
    split_outs = [((seq, CONV_CH), F32), ((seq, CONV_CH), F32), ((N_Q, seq, HEAD), F32), ((N_KV, seq, HEAD), F32),
                  ((N_KV, seq, HEAD), F32), ((N_MEMH, seq, HEAD), F32)]
    sv["a"], sv["gate"], sv["q"], sv["k"], sv["v"], sv["qm"] = seq_fwd(
        f_proj_split, _proj_args(sv["h1"], sw, bw), split_outs, tm, tm, "proj_in_fwd" + tag)
    ml = mem.shape[0]
    (sv["mkv"],) = seq_fwd(f_proj_in, [(mem, TILE, False), (sw["mem_norm"], PARAM, True), (bw["w_mem_kv"], PARAM, True)],
                           [((ml, bw["w_mem_kv"].shape[1]), F32)], ml, ml, "mem_kv_fwd" + tag)
    conv_args, swa_args, mem_args = _mix_args(sv, tabs, sw)
    (sv["y_conv"],) = seq_fwd(f_conv, conv_args, [((seq, CONV_CH), F32)], tm, 32, "conv_fwd" + tag)
    (sv["y_swa"],) = seq_fwd(f_swa, swa_args, [((N_Q, seq, HEAD), F32)], tm, BLOCK, "swa_fwd" + tag)
    (sv["y_mem"],) = seq_fwd(f_mem, mem_args, [((N_MEMH, seq, HEAD), F32)], tm, tm, "mem_attn_fwd" + tag)
    (sv["h2"],) = seq_fwd(f_out_join, _join_args(sv, bw), [((seq, dm), F32)], tm, tm, "out_proj_fwd" + tag)
    sv["h3"], sv["ups2"], got2 = ffn_fwd(sv["h2"], sw["ffn2_norm"], bw["ffn2_w1"], bw["ffn2_w3"], bw["ffn2_w2"], tm,
                                         "ffn2_fwd" + tag, items2)
    (xo,) = seq_fwd(f_rms, [(sv["h3"], TILE, True), (sw["final_norm"], PARAM, True)], [((seq, dm), F32)], tm, tm,
                    "final_norm_fwd" + tag)
    return xo, sv, bw, kernel_layouts(got2, GATHER_EARLY)


def layer_bwd(dxo, sv, mem, tabs, bw, sw, tm, l, items=()):
    tag = f"_l{l}"
    gb, gs = {}, {}
    dh3, gs["final_norm"] = seq_bwd(f_rms, [(sv["h3"], TILE, True), (sw["final_norm"], PARAM, True)], [dxo], tm, tm,
                                    "final_norm_bwd" + tag)
    parts2, gb["ffn2_w1"], gb["ffn2_w3"], gb["ffn2_w2"], got = ffn_bwd_chunks(
        sv["h2"], sw["ffn2_norm"], dh3, sv["ups2"], bw["ffn2_w1"], bw["ffn2_w3"], bw["ffn2_w2"], tm, "ffn2_bwd" + tag,
        items)
    dh2, gs["ffn2_norm"] = ffn_bwd_norm(sv["h2"], sw["ffn2_norm"], parts2, dh3, tm, "ffn2_norm_bwd" + tag)
    dh1_a, dy_conv, dy_swa, dy_mem, gb["w_out"] = seq_bwd(f_out_join, _join_args(sv, bw), [dh2], tm // 2, tm // 2,
                                                           "out_proj_bwd" + tag)
    conv_args, swa_args, mem_args = _mix_args(sv, tabs, sw)
    da, dgate, gs["conv_w"], gs["conv_b"], gs["conv_ln_g"], gs["conv_ln_b"] = seq_bwd(
        f_conv, conv_args, [dy_conv], tm, 32, "conv_bwd" + tag)
    dq, dk, dv, gs["swa_q_norm"], gs["swa_k_norm"], gs["swa_sinks"] = seq_bwd(
        f_swa, swa_args, [dy_swa], tm, BLOCK, "swa_bwd" + tag)
    dqm, dmk, dmv, gs["mem_q_norm"], gs["mem_k_norm"] = seq_bwd(f_mem, mem_args, [dy_mem], tm, tm, "mem_attn_bwd" + tag)
    dmkv = jnp.concatenate([_unheads(dmk), _unheads(dmv)], axis=-1)
    ml = mem.shape[0]
    gs["mem_norm"], gb["w_mem_kv"] = seq_bwd(
        f_proj_in, [(mem, TILE, False), (sw["mem_norm"], PARAM, True), (bw["w_mem_kv"], PARAM, True)], [dmkv], ml, ml,
        "mem_kv_bwd" + tag)
    dh1, gs["mix_norm"], gb["w_in"] = seq_bwd(f_proj_split, _proj_args(sv["h1"], sw, bw),
                                               [da, dgate, dq, dk, dv, dqm], tm // 2, tm // 2, "proj_in_bwd" + tag,
                                               add_to_first=dh1_a)
    parts1, gb["ffn1_w1"], gb["ffn1_w3"], gb["ffn1_w2"], got_late = ffn_bwd_chunks(
        sv["x"], sw["ffn1_norm"], dh1, sv["ups1"], bw["ffn1_w1"], bw["ffn1_w3"], bw["ffn1_w2"], tm, "ffn1_bwd" + tag,
        scatter_items(gb, GATHER_LATE))
    dx, gs["ffn1_norm"] = ffn_bwd_norm(sv["x"], sw["ffn1_norm"], parts1, dh1, tm, "ffn1_norm_bwd" + tag)
    return dx, gb, gs, got, got_late


FFN_UP = ("ffn1_w1", "ffn1_w3", "ffn2_w1", "ffn2_w3")
FFN_DOWN = ("ffn1_w2", "ffn2_w2")
ROW_SHARDED = ("w_mem_kv", "w_out")
GATHER_EARLY = ("ffn1_w1", "ffn1_w3", "ffn1_w2")
GATHER_LATE = ("w_in", "w_mem_kv", "w_out", "ffn2_w1", "ffn2_w3", "ffn2_w2")
BIG = GATHER_EARLY + GATHER_LATE
SMALL = ("ffn1_norm", "mix_norm", "conv_b", "conv_ln_g", "conv_ln_b", "swa_q_norm", "swa_k_norm", "swa_sinks",
         "mem_norm", "mem_q_norm", "mem_k_norm", "ffn2_norm", "final_norm")
WEIGHTS = ("ffn1_norm", "ffn1_w1", "ffn1_w3", "ffn1_w2", "mix_norm", "w_in", "conv_w", "conv_b", "conv_ln_g",
           "conv_ln_b", "swa_q_norm", "swa_k_norm", "swa_sinks", "mem_norm", "w_mem_kv", "mem_q_norm", "mem_k_norm",
           "w_out", "ffn2_norm", "ffn2_w1", "ffn2_w3", "ffn2_w2", "final_norm")


def _round_up(n, m):
    return -(-n // m) * m


def _row_block(rows, target, mult=8):
    best = rows
    for cand in range(mult, min(rows, target) + 1, mult):
        if rows % cand == 0:
            best = cand
    return best


def _pad_rows(flat, cols, mult):
    n = flat.shape[0]
    rows = _round_up(-(-n // cols), mult)
    return jnp.pad(flat, (0, rows * cols - n)).reshape(rows, cols)


def send_form(w):
    out = {}
    for n in BIG:
        t = w[n].astype(BF)
        if n in FFN_UP:
            t = jnp.pad(t, ((0, 0), (0, 0), (0, _round_up(t.shape[2], LANE) - t.shape[2])))
        elif n in FFN_DOWN:
            t = jnp.pad(t, ((0, 0), (0, _round_up(t.shape[1], LANE) - t.shape[1]), (0, 0)))
        out[n] = t
    return out


def gather_items(sent, l, names):
    return [gather_cols_item(sent[n][l]) if n in FFN_UP else gather_item(sent[n][l]) for n in names]


def kernel_layouts(got, names):
    out = {}
    for n, t in zip(names, got):
        if n in FFN_DOWN:
            t = t.reshape(N_DEV // 2, 2 * t.shape[1], t.shape[2])
        elif n == "w_in":
            t = t.transpose(1, 0, 2).reshape(t.shape[1], N_DEV * t.shape[2])
        elif n in ROW_SHARDED:
            t = t.reshape(N_DEV * t.shape[1], t.shape[2])
        out[n] = t
    return out


def scatter_items(gb, names):
    items = []
    for n in names:
        t = gb[n]
        if n in FFN_UP:
            items.append(scatter_cols_item(t))
            continue
        if n in FFN_DOWN:
            t = t.reshape(N_DEV, t.shape[1] // 2, t.shape[2])
        elif n == "w_in":
            t = t.reshape(t.shape[0], N_DEV, t.shape[1] // N_DEV).transpose(1, 0, 2).astype(BF)
        else:
            t = t.reshape(N_DEV, t.shape[0] // N_DEV, t.shape[1]).astype(BF)
        items.append(scatter_item(t))
    return items


def small_layer_params(w, l):
    sw = {n: w[n][l][None, :] for n in SMALL if n != "swa_sinks"}
    sw["swa_sinks"] = jnp.pad(w["swa_sinks"][l], (0, LANE - N_Q))[None, :]
    sw["conv_w"] = jnp.pad(w["conv_w_full"][l], ((0, 1), (0, 0)))
    return sw


def kernel(x, mem, positions, ffn1_norm, ffn1_w1, ffn1_w3, ffn1_w2, mix_norm, w_in, conv_w, conv_b, conv_ln_g, conv_ln_b, swa_q_norm, swa_k_norm, swa_sinks, mem_norm, w_mem_kv, mem_q_norm, mem_k_norm, w_out, ffn2_norm, ffn2_w1, ffn2_w3, ffn2_w2, final_norm, loss_target, m_ffn1_norm, m_ffn1_w1, m_ffn1_w3, m_ffn1_w2, m_mix_norm, m_w_in, m_conv_w, m_conv_b, m_conv_ln_g, m_conv_ln_b, m_swa_q_norm, m_swa_k_norm, m_swa_sinks, m_mem_norm, m_w_mem_kv, m_mem_q_norm, m_mem_k_norm, m_w_out, m_ffn2_norm, m_ffn2_w1, m_ffn2_w3, m_ffn2_w2, m_final_norm, v_ffn1_norm, v_ffn1_w1, v_ffn1_w3, v_ffn1_w2, v_mix_norm, v_w_in, v_conv_w, v_conv_b, v_conv_ln_g, v_conv_ln_b, v_swa_q_norm, v_swa_k_norm, v_swa_sinks, v_mem_norm, v_w_mem_kv, v_mem_q_norm, v_mem_k_norm, v_w_out, v_ffn2_norm, v_ffn2_w1, v_ffn2_w3, v_ffn2_w2, v_final_norm):
    loc = locals()
    w = {n: loc[n] for n in WEIGHTS}
    m = {n: loc["m_" + n] for n in WEIGHTS}
    v = {n: loc["v_" + n] for n in WEIGHTS}
    depth = ffn1_norm.shape[0]
    seq = x.shape[1]
    tm = min(512, seq)
    me = 4 * lax.axis_index("x") + 2 * lax.axis_index("y") + lax.axis_index("c")
    xs, mems, target = x[0], mem[0], loss_target[0]

    sent = send_form(w)
    cw = conv_w.shape[2]
    conv_rows = _pad_rows(conv_w.reshape(-1), LANE, 8)
    got = comm_call(gather_items(sent, 0, GATHER_EARLY) + [gather_item(conv_rows)], "gather_l0")
    bw = kernel_layouts(got[:-1], GATHER_EARLY)
    conv_full = got[-1].reshape(N_DEV, -1)[:, :conv_w.size].reshape((N_DEV,) + conv_w.shape)
    small = {n: w[n] for n in SMALL}
    small["conv_w_full"] = conv_full.transpose(1, 2, 0, 3).reshape(depth, CONV_W, N_DEV * cw)

    inv = ROPE_THETA ** (-jnp.arange(0, HEAD, 2, dtype=F32) / HEAD)
    tabs = rope_tables(positions[0].reshape(-1, 1), jnp.concatenate([inv, inv])[None, :], tm)
    saved, bws = [], []
    h = xs
    for l in range(depth):
        h, sv, bw_all, bw = layer_fwd(h, mems, tabs, bw, sent, small_layer_params(small, l), tm, l, depth)
        saved.append(sv)
        bws.append(bw_all)
    loss_sum, dh = loss_and_grad(h, target, tm)
    loss = lax.psum(loss_sum[0, 0], ("x", "y", "c"))

    recv = [dict() for _ in range(depth)]
    gss = [None] * depth
    items = ()
    for l in reversed(range(depth)):
        dh, gb, gss[l], got, got_late = layer_bwd(dh, saved[l], mems, tabs, bws[l], small_layer_params(small, l), tm,
                                                  l, items)
        if items:
            recv[l + 1].update(zip(GATHER_EARLY, got))
        recv[l].update(zip(GATHER_LATE, got_late))
        items = scatter_items(gb, GATHER_EARLY)
    recv[0].update(zip(GATHER_EARLY, comm_call(items, "grad_exchange_l0")))

    g_all = {}
    for n in BIG:
        a, b = w[n].shape[1:]
        blocks = [recv[l][n] for l in range(depth)]
        if n in FFN_DOWN:
            rows = _row_block(a, 256, BF16_ROWS)
        else:
            rows = _row_block(a, 256)
        g_all[n] = sum_layers(blocks, a, b, rows, "grad_sum_" + n)

    small_names = SMALL + ("conv_w",)
    sizes = {n: (w[n].shape[1] if n != "conv_w" else CONV_W * N_DEV * cw) for n in small_names}
    flat = []
    for l in range(depth):
        for n in small_names:
            t = gss[l][n]
            if n == "swa_sinks":
                t = t[:, :N_Q]
            elif n == "conv_w":
                t = t[:CONV_W]
            flat.append(t.reshape(-1))
    flat = _pad_rows(jnp.concatenate(flat), LANE, 8)
    (small_got,) = comm_call([gather_item(flat)], "small_grad_gather")
    summed = sum_layers([small_got], flat.shape[0], LANE, flat.shape[0], "small_grad_sum").reshape(-1)
    g_small = {n: [] for n in small_names}
    o = 0
    for l in range(depth):
        for n in small_names:
            g_small[n].append(summed[o:o + sizes[n]])
            o += sizes[n]
    for n in SMALL:
        g_all[n] = jnp.stack(g_small[n])
    gcw = jnp.stack(g_small["conv_w"]).reshape(depth, CONV_W, N_DEV, cw)
    g_all["conv_w"] = lax.dynamic_slice(gcw, (0, 0, me, 0), (depth, CONV_W, 1, cw)).reshape(depth, CONV_W, cw)

    outs = {"delta": {}, "new_m": {}, "new_v": {}}
    for n in BIG:
        shape = w[n].shape
        two_d = (shape[0] * shape[1], shape[2])
        res = adamw(w[n].reshape(two_d), g_all[n].reshape(two_d), m[n].reshape(two_d), v[n].reshape(two_d),
                    _row_block(two_d[0], 512), "adamw_" + n)
        for key, arr in zip(("delta", "new_m", "new_v"), res):
            outs[key][n] = arr.reshape(shape)

    def packed_small(t):
        return _pad_rows(jnp.concatenate([t[n].reshape(-1) for n in small_names]), LANE, 8)

    ps = [packed_small(t) for t in (w, g_all, m, v)]
    res = adamw(*ps, ps[0].shape[0], "adamw_small")
    for key, arr in zip(("delta", "new_m", "new_v"), res):
        fl, o = arr.reshape(-1), 0
        for n in small_names:
            outs[key][n] = fl[o:o + w[n].size].reshape(w[n].shape)
            o += w[n].size

    return (loss, dh[None], *[g_all[n] for n in WEIGHTS], *[outs["delta"][n] for n in WEIGHTS],
            *[outs["new_m"][n] for n in WEIGHTS], *[outs["new_v"][n] for n in WEIGHTS])
```

```python
import jax
import jax.numpy as jnp
from jax import lax
from jax.experimental import pallas as pl
from jax.experimental.pallas import tpu as pltpu

F32 = jnp.float32
BF = jnp.bfloat16
EPS = 1e-6
HEAD = 64
N_Q, N_KV, N_MEMH = 6, 2, 4
CONV_CH, CONV_W = 384, 31
BLOCK = 128
ROPE_THETA = 10000.0
N_DEV = 8
V7X_VMEM_LIMIT = 56 * 1024 * 1024
LANE = 128
BF16_ROWS = 16

ADAM_LR, ADAM_B1, ADAM_B2, ADAM_EPS, ADAM_WD, ADAM_STEP = 0.001, 0.9, 0.999, 1e-08, 0.01, 10

TILE, HALO, PARAM = "tile", "halo", "param"
MESH = pl.DeviceIdType.MESH
ANY = pl.BlockSpec(memory_space=pl.ANY)


def _cparams(sem=None):
    kw = dict(vmem_limit_bytes=V7X_VMEM_LIMIT)
    if sem is not None:
        kw["dimension_semantics"] = sem
    return pltpu.CompilerParams(**kw)


def _dot(a, b, dims):
    return lax.dot_general(a, b, (dims, ((), ())), preferred_element_type=F32)


_NN, _NT, _TN = ((1,), (0,)), ((1,), (1,)), ((0,), (0,))


@jax.custom_vjp
def mm(a, b):
    return _dot(a.astype(BF), b.astype(BF), _NN)


def _mm_fwd(a, b):
    return mm(a, b), (a, b)


def _mm_bwd(res, g):
    a, b = res
    gb = g.astype(BF)
    return _dot(gb, b.astype(BF), _NT), _dot(a.astype(BF), gb, _TN)


mm.defvjp(_mm_fwd, _mm_bwd)


@jax.custom_vjp
def mm_nt(a, b):
    return _dot(a.astype(BF), b.astype(BF), _NT)


def _mm_nt_fwd(a, b):
    return mm_nt(a, b), (a, b)


def _mm_nt_bwd(res, g):
    a, b = res
    gb = g.astype(BF)
    return _dot(gb, b.astype(BF), _NN), _dot(gb, a.astype(BF), _TN)


mm_nt.defvjp(_mm_nt_fwd, _mm_nt_bwd)


def rms(x, g):
    return x * lax.rsqrt(jnp.mean(x * x, axis=-1, keepdims=True) + EPS) * g


def _swap_halves(x):
    half = HEAD // 2
    return jnp.concatenate([x[:, half:], x[:, :half]], axis=1)


@jax.custom_vjp
def rope(x, c, s):
    return x * c + _swap_halves(x) * s


def _rope_fwd(x, c, s):
    return rope(x, c, s), (c, s)


def _rope_bwd(res, g):
    c, s = res
    return g * c + _swap_halves(g * s), jnp.zeros_like(c), jnp.zeros_like(s)


rope.defvjp(_rope_fwd, _rope_bwd)


SUBLANES = 8


def _row_shifts(t, rows):
    return [t] + [t[b:b + rows + 24] for b in range(1, SUBLANES)]


CONV_ROWS = 64


def _window(shifts, offset, start, rows):
    base = offset - offset % SUBLANES + start
    return shifts[offset % SUBLANES][base:base + rows]


def _row_blocks(rows):
    return [(r, min(CONV_ROWS, rows - r)) for r in range(0, rows, CONV_ROWS)]


def _conv_taps(gs, w, tm, first_offset=2, step=1):
    wr = [w[j:j + 1, :] for j in range(CONV_W)]
    blocks = []
    for r, n in _row_blocks(tm):
        acc = wr[0] * _window(gs, first_offset, r, n)
        for j in range(1, CONV_W):
            acc = acc + wr[j] * _window(gs, first_offset + step * j, r, n)
        blocks.append(acc)
    return jnp.concatenate(blocks, axis=0)


@jax.custom_vjp
def causal_dw_conv(g, w):
    tm = g.shape[0] - 32
    return _conv_taps(_row_shifts(g, tm), w, tm)


def _conv_fwd(g, w):
    tm = g.shape[0] - 32
    gs = _row_shifts(g, tm)
    return _conv_taps(gs, w, tm), (gs, w)


def _conv_bwd(res, dc):
    gs, w = res
    tm, ch = dc.shape
    z = jnp.zeros((32, ch), F32)
    ds = _row_shifts(jnp.concatenate([z, dc, z], axis=0), tm + 32)
    dg = _conv_taps(ds, w, tm + 32, first_offset=30, step=-1)
    rows = lax.broadcasted_iota(jnp.int32, (32, 1), 0)
    dcb = [dc[r:r + n] for r, n in _row_blocks(tm)]
    dw = jnp.zeros((32, ch), F32)
    for j in range(CONV_W):
        part = jnp.zeros((SUBLANES, ch), F32)
        for (r, n), d in zip(_row_blocks(tm), dcb):
            prod = d * _window(gs, 2 + j, r, n)
            for s in range(0, n, SUBLANES):
                part = part + prod[s:s + SUBLANES]
        dw = dw + jnp.where(rows == j, jnp.sum(part, axis=0, keepdims=True), 0.0)
    return dg, dw


causal_dw_conv.defvjp(_conv_fwd, _conv_bwd)


def f_rms(first, x, g):
    return (rms(x, g),)


def f_proj_in(first, h, g, w):
    return (mm(rms(h, g), w),)


def _to_heads(t, n):
    return jnp.stack([t[:, h * HEAD:(h + 1) * HEAD] for h in range(n)])


def f_proj_split(first, h, g, w):
    p = mm(rms(h, g), w)
    o, outs = 0, []
    for width, n in ((CONV_CH, 0), (CONV_CH, 0), (N_Q * HEAD, N_Q), (N_KV * HEAD, N_KV), (N_KV * HEAD, N_KV),
                     (N_MEMH * HEAD, N_MEMH)):
        t = p[:, o:o + width]
        outs.append(_to_heads(t, n) if n else t)
        o += width
    return tuple(outs)


def f_out_join(first, h, y_conv, y_swa, y_mem, w):
    y = jnp.concatenate([y_conv] + [y_swa[i] for i in range(N_Q)] + [y_mem[i] for i in range(N_MEMH)], axis=1)
    return (h + mm(y, w),)


def f_conv(first, a, gate, w, b, lg, lb):
    keep = 1.0 - first
    av = jnp.concatenate([a[0] * keep, a[1]], axis=0)
    gv = jnp.concatenate([gate[0], gate[1]], axis=0)
    glu = av * jax.nn.sigmoid(gv)
    c = causal_dw_conv(glu, w) + b
    mu = jnp.mean(c, axis=-1, keepdims=True)
    var = jnp.mean(jnp.square(c - mu), axis=-1, keepdims=True)
    z = (c - mu) * lax.rsqrt(var + EPS) * lg + lb
    return (z * jax.nn.sigmoid(z),)


def _softmax_with_extra(s, extra):
    m = jnp.max(s, axis=-1, keepdims=True)
    if extra is not None:
        m = jnp.maximum(m, extra)
    m = lax.stop_gradient(m)
    e = jnp.exp(s - m)
    den = jnp.sum(e, axis=-1, keepdims=True)
    if extra is not None:
        den = den + jnp.exp(extra - m)
    return e / den


def f_swa(first, q, k, v, ct, st, qn, kn, sinks):
    tm = q.shape[1]
    nb = tm // BLOCK
    g = N_Q // N_KV
    c_all = jnp.concatenate([ct[0], ct[1]], axis=0)
    s_all = jnp.concatenate([st[0], st[1]], axis=0)
    qi = lax.broadcasted_iota(jnp.int32, (g * BLOCK, 2 * BLOCK), 0)
    kj = lax.broadcasted_iota(jnp.int32, (g * BLOCK, 2 * BLOCK), 1)
    qpos = jnp.where(qi >= 2 * BLOCK, qi - 2 * BLOCK, jnp.where(qi >= BLOCK, qi - BLOCK, qi)) + BLOCK
    rel = qpos - kj
    band = (rel >= 0) & (rel < BLOCK)
    band_first = band & ((kj >= BLOCK) | (first < 0.5))
    lane = lax.broadcasted_iota(jnp.int32, (g * BLOCK, LANE), 1)
    hrow = lax.broadcasted_iota(jnp.int32, (g * BLOCK, LANE), 0)
    head_in_group = jnp.where(hrow >= 2 * BLOCK, 2, jnp.where(hrow >= BLOCK, 1, 0))
    qr = [rope(rms(q[h], qn), ct[1], st[1]) for h in range(N_Q)]
    outs = [[None] * nb for _ in range(N_Q)]
    for hk in range(N_KV):
        kk = rope(rms(jnp.concatenate([k[0][hk], k[1][hk]], axis=0), kn), c_all, s_all)
        vv = jnp.concatenate([v[0][hk], v[1][hk]], axis=0)
        sel = (lane == head_in_group + hk * g).astype(F32)
        sink_col = jnp.sum(sel * sinks, axis=1, keepdims=True)
        for j in range(nb):
            keys = kk[j * BLOCK:(j + 2) * BLOCK]
            vals = vv[j * BLOCK:(j + 2) * BLOCK]
            qs = jnp.concatenate([qr[hk * g + gg][j * BLOCK:(j + 1) * BLOCK] for gg in range(g)], axis=0)
            s = mm_nt(qs, keys) * (HEAD ** -0.5)
            s = jnp.where(band_first if j == 0 else band, s, -1e30)
            o = mm(_softmax_with_extra(s, sink_col), vals)
            for gg in range(g):
                outs[hk * g + gg][j] = o[gg * BLOCK:(gg + 1) * BLOCK]
    return (jnp.stack([jnp.concatenate(outs[h], axis=0) for h in range(N_Q)]),)


def f_mem(first, qm, mk, mv, qn, kn):
    outs = []
    for h in range(N_MEMH):
        qh = rms(qm[h], qn)
        kh = rms(mk[h], kn)
        s = mm_nt(qh, kh) * (HEAD ** -0.5)
        outs.append(mm(_softmax_with_extra(s, None), mv[h]))
    return (jnp.stack(outs),)


def _seq_len(arr):
    return arr.shape[0] if arr.ndim == 2 else arr.shape[1]


def _tile_spec(arr, rows, imap):
    if arr.ndim == 2:
        return pl.BlockSpec((rows, arr.shape[1]), lambda i: (imap(i), 0))
    return pl.BlockSpec((arr.shape[0], rows, arr.shape[2]), lambda i: (0, imap(i), 0))


def _full_spec(arr):
    nd = arr.ndim
    return pl.BlockSpec(arr.shape, lambda i: (0,) * nd)


def _in_specs(args, tm, hl, nt):
    specs = []
    ratio = tm // hl
    cur = lambda i: jnp.minimum(i, nt - 1)
    prev = lambda i: jnp.maximum(jnp.minimum(i, nt - 1) * ratio - 1, 0)
    for arr, kind, _ in args:
        if kind == TILE:
            specs.append(_tile_spec(arr, tm, cur))
        elif kind == HALO:
            specs.append(_tile_spec(arr, hl, prev))
            specs.append(_tile_spec(arr, tm, cur))
        else:
            specs.append(_full_spec(arr))
    return specs


def _operands(args):
    ops = []
    for arr, kind, _ in args:
        ops.append(arr)
        if kind == HALO:
            ops.append(arr)
    return ops


def _load_values(args, refs):
    vals, k = [], 0
    for arr, kind, _ in args:
        if kind == HALO:
            vals.append((refs[k][...].astype(F32), refs[k + 1][...].astype(F32)))
            k += 2
        else:
            vals.append(refs[k][...].astype(F32))
            k += 1
    return vals


def seq_fwd(f, args, outs, tm, hl, name):
    out_shape = [jax.ShapeDtypeStruct(s, d) for s, d in outs]
    nt = _seq_len(out_shape[0]) // tm
    n_in = len(_operands(args))

    def body(*refs):
        first = (pl.program_id(0) == 0).astype(F32)
        res = f(first, *_load_values(args, refs[:n_in]))
        for r, o in zip(refs[n_in:], res):
            r[...] = o.astype(r.dtype)

    cur = lambda i: i
    return pl.pallas_call(
        body, grid=(nt,), in_specs=_in_specs(args, tm, hl, nt),
        out_specs=[_tile_spec(o, tm, cur) for o in out_shape], out_shape=out_shape,
        compiler_params=_cparams(("arbitrary",)), name=name,
    )(*_operands(args))


def seq_bwd(f, args, douts, tm, hl, name, add_to_first=None):
    seq = _seq_len(douts[0])
    nt = seq // tm
    lag = any(kind == HALO and diff for _, kind, diff in args)
    steps = nt + 1 if lag else nt
    n_in = len(_operands(args))
    n_do = len(douts)
    dargs = [(arr, kind) for arr, kind, diff in args if diff]
    extra = [] if add_to_first is None else [add_to_first]
    assert not extra or (dargs[0][1] == TILE and not lag)

    def body(*refs):
        in_refs = refs[:n_in]
        do_refs = refs[n_in:n_in + n_do]
        add_refs = refs[n_in + n_do:n_in + n_do + len(extra)]
        refs = refs[len(extra):]
        g_refs = refs[n_in + n_do:n_in + n_do + len(dargs)]
        carries = refs[n_in + n_do + len(dargs):]
        i = pl.program_id(0)
        first = (i == 0).astype(F32)

        def compute():
            vals = _load_values(args, in_refs)
            dvals = [v for v, (_, _, diff) in zip(vals, args) if diff]

            def fd(*dv):
                it = iter(dv)
                return f(first, *[next(it) if diff else v for v, (_, _, diff) in zip(vals, args)])

            _, vjp = jax.vjp(fd, *dvals)
            grads = vjp(tuple(r[...].astype(F32) for r in do_refs))
            c = 0
            for pos, (gref, gval, (arr, kind)) in enumerate(zip(g_refs, grads, dargs)):
                if kind == TILE:
                    if pos == 0 and extra:
                        gval = gval + add_refs[0][...]
                    gref[...] = gval.astype(gref.dtype)
                elif kind == PARAM:
                    @pl.when(i == 0)
                    def _():
                        gref[...] = gval

                    @pl.when(i > 0)
                    def _():
                        gref[...] += gval
                else:
                    carry = carries[c]
                    c += 1
                    g_prev, g_cur = gval

                    @pl.when(i > 0)
                    def _():
                        gref[...] = carry[...]
                        if arr.ndim == 2:
                            gref[tm - hl:tm, :] += g_prev
                        else:
                            gref[:, tm - hl:tm, :] += g_prev

                    carry[...] = g_cur

        if lag:
            pl.when(i < nt)(compute)

            @pl.when(i == nt)
            def _():
                c = 0
                for gref, (arr, kind) in zip(g_refs, dargs):
                    if kind == HALO:
                        gref[...] = carries[c][...]
                        c += 1
        else:
            compute()

    cur = lambda i: jnp.minimum(i, nt - 1)
    lagged = lambda i: jnp.maximum(i - 1, 0)
    out_shape, out_specs, scratch = [], [], []
    for arr, kind in dargs:
        out_shape.append(jax.ShapeDtypeStruct(arr.shape, F32))
        if kind == PARAM:
            out_specs.append(_full_spec(arr))
        else:
            out_specs.append(_tile_spec(arr, tm, lagged if kind == HALO else cur))
            if kind == HALO:
                blk = (tm, arr.shape[1]) if arr.ndim == 2 else (arr.shape[0], tm, arr.shape[2])
                scratch.append(pltpu.VMEM(blk, F32))
    in_specs = _in_specs(args, tm, hl, nt) + [_tile_spec(d, tm, cur) for d in list(douts) + extra]
    return pl.pallas_call(
        body, grid=(steps,), in_specs=in_specs, out_specs=out_specs, out_shape=out_shape,
        scratch_shapes=scratch, compiler_params=_cparams(("arbitrary",)), name=name,
    )(*_operands(args), *douts, *extra)


class Item:
    def __init__(self, operand, out_shape, src, dst, two_level=False):
        self.operand, self.out_shape, self.src, self.dst, self.two_level = operand, out_shape, src, dst, two_level


def _dev(p):
    return 4 * p[0] + 2 * p[1] + p[2]


def gather_item(shard):
    return Item(shard, jax.ShapeDtypeStruct((N_DEV,) + shard.shape, shard.dtype),
                lambda r, peer: r, lambda r, s: r.at[_dev(s)], two_level=True)


def gather_cols_item(shard):
    d, w = shard.shape
    return Item(shard, jax.ShapeDtypeStruct((N_DEV // 2, d, 2 * w), shard.dtype),
                lambda r, peer: r, lambda r, s: r.at[2 * s[0] + s[1], :, pl.ds(s[2] * w, w)], two_level=True)


def scatter_item(blocks):
    return Item(blocks, jax.ShapeDtypeStruct(blocks.shape, blocks.dtype),
                lambda r, peer: r.at[_dev(peer)], lambda r, s: r.at[_dev(s)])


def scatter_cols_item(full):
    n, d, w2 = full.shape
    w = w2 // 2
    return Item(full, jax.ShapeDtypeStruct((N_DEV, d, w), full.dtype),
                lambda r, peer: r.at[2 * peer[0] + peer[1], :, pl.ds(peer[2] * w, w)], lambda r, s: r.at[_dev(s)])


def _comm_sems(items):
    n = len(items) * (N_DEV - 1)
    return [pltpu.SemaphoreType.DMA((n,)), pltpu.SemaphoreType.DMA((n,)), pltpu.SemaphoreType.DMA((len(items),))]


PASSED_ON = (3, 5, 7)


def _comm_copies(items, in_refs, out_refs, sems, x, y, c):
    send_sems, recv_sems, local_sems = sems
    me = (x, y, c)
    sibling = (x, y, 1 - c)
    local, first, passed = [], [], []

    def remote(t, k, src, dst, to):
        n = t * (N_DEV - 1) + k - 1
        return lambda: pltpu.make_async_remote_copy(
            src_ref=src(), dst_ref=dst(), send_sem=send_sems.at[n], recv_sem=recv_sems.at[n],
            device_id=(to[0], to[1], jnp.int32(to[2])), device_id_type=MESH)

    for t, it in enumerate(items):
        local.append(lambda t=t, it=it: pltpu.make_async_copy(it.src(in_refs[t], me), it.dst(out_refs[t], me),
                                                               local_sems.at[t]))
        for k in range(1, N_DEV):
            peer = (1 - x if k & 4 else x, 1 - y if k & 2 else y, 1 - c if k & 1 else c)
            if it.two_level and k in PASSED_ON:
                origin = (peer[0], peer[1], c)
                landed = lambda t=t, it=it, origin=origin: it.dst(out_refs[t], origin)
                passed.append((t, k, remote(t, k, landed, landed, sibling)))
            else:
                first.append((t, k, remote(t, k, lambda t=t, it=it, peer=peer: it.src(in_refs[t], peer),
                                           lambda t=t, it=it: it.dst(out_refs[t], me), peer)))
    return local, first, passed


def _on_my_core(fn):
    x, y, c = lax.axis_index("x"), lax.axis_index("y"), lax.axis_index("c")
    for cv in (0, 1):
        @pl.when(c == cv)
        def _():
            fn(x, y, cv)


def comm_start(items, in_refs, out_refs, sems):
    def go(x, y, c):
        local, first, _ = _comm_copies(items, in_refs, out_refs, sems, x, y, c)
        for make in local:
            make().start()
        for _, _, make in first:
            make().start()

    _on_my_core(go)


def comm_pass_on(items, in_refs, out_refs, sems):
    def go(x, y, c):
        _, first, passed = _comm_copies(items, in_refs, out_refs, sems, x, y, c)
        arrived = {(t, k): make for t, k, make in first}
        for t, k, make in passed:
            arrived[(t, k - 1)]().wait_recv()
            make().start()

    _on_my_core(go)


def comm_wait(items, in_refs, out_refs, sems):
    def go(x, y, c):
        local, first, passed = _comm_copies(items, in_refs, out_refs, sems, x, y, c)
        waited = {(t, k - 1) for t, k, _ in passed}
        for t, k, make in first:
            cp = make()
            if (t, k) not in waited:
                cp.wait_recv()
            cp.wait_send()
        for _, _, make in passed:
            cp = make()
            cp.wait_recv()
            cp.wait_send()
        for make in local:
            make().wait()

    _on_my_core(go)


def comm_call(items, name):
    n = len(items)

    def body(*refs):
        in_refs, out_refs, sems = refs[:n], refs[n:2 * n], refs[2 * n:]
        comm_start(items, in_refs, out_refs, sems)
        comm_pass_on(items, in_refs, out_refs, sems)
        comm_wait(items, in_refs, out_refs, sems)

    return pl.pallas_call(
        body, in_specs=[ANY] * n, out_specs=[ANY] * n, out_shape=[it.out_shape for it in items],
        scratch_shapes=_comm_sems(items), name=name,
    )(*[it.operand for it in items])


def ffn_fwd(x, g, w1, w3, w2, tm, name, items=()):
    seq, dm = x.shape
    nc, _, fc = w1.shape
    nt = seq // tm
    n = len(items)

    def body(*refs):
        x_ref, g_ref, w1_ref, w3_ref, w2_ref = refs[:5]
        c_in, (h_ref, a_ref, b_ref), c_out = refs[5:5 + n], refs[5 + n:8 + n], refs[8 + n:8 + 2 * n]
        xn_s, acc_s = refs[8 + 2 * n:10 + 2 * n]
        sems = refs[10 + 2 * n:]
        i, c = pl.program_id(0), pl.program_id(1)

        if n:
            @pl.when((i == 0) & (c == 0))
            def _():
                comm_start(items, c_in, c_out, sems)

        @pl.when(c == 0)
        def _():
            xn_s[...] = rms(x_ref[...], g_ref[...]).astype(BF)
            acc_s[...] = jnp.zeros_like(acc_s)

        xn = xn_s[...]
        a = _dot(xn, w1_ref[0], _NN)
        b = _dot(xn, w3_ref[0], _NN)
        a_ref[...] = a.astype(BF)
        b_ref[...] = b.astype(BF)
        hid = (a * jax.nn.sigmoid(a)) * b
        acc_s[...] += _dot(hid.astype(BF), w2_ref[0], _NN)

        @pl.when(c == nc - 1)
        def _():
            h_ref[...] = x_ref[...] + 0.5 * acc_s[...]

        if n:
            step = i * nc + c
            total = nt * nc

            @pl.when(step == max(1, (13 * total) // 16))
            def _():
                comm_pass_on(items, c_in, c_out, sems)

            @pl.when(step == total - 1)
            def _():
                comm_wait(items, c_in, c_out, sems)

    res = pl.pallas_call(
        body, grid=(nt, nc),
        in_specs=[pl.BlockSpec((tm, dm), lambda i, c: (i, 0)), pl.BlockSpec((1, dm), lambda i, c: (0, 0)),
                  pl.BlockSpec((1, dm, fc), lambda i, c: (c, 0, 0)), pl.BlockSpec((1, dm, fc), lambda i, c: (c, 0, 0)),
                  pl.BlockSpec((1, fc, dm), lambda i, c: (c, 0, 0))] + [ANY] * n,
        out_specs=[pl.BlockSpec((tm, dm), lambda i, c: (i, 0)), pl.BlockSpec((tm, fc), lambda i, c: (i, c)),
                   pl.BlockSpec((tm, fc), lambda i, c: (i, c))] + [ANY] * n,
        out_shape=[jax.ShapeDtypeStruct((seq, dm), F32), jax.ShapeDtypeStruct((seq, nc * fc), BF),
                   jax.ShapeDtypeStruct((seq, nc * fc), BF)] + [it.out_shape for it in items],
        scratch_shapes=[pltpu.VMEM((tm, dm), BF), pltpu.VMEM((tm, dm), F32)] + (_comm_sems(items) if n else []),
        compiler_params=_cparams(("arbitrary", "arbitrary")), name=name,
    )(x, g, w1, w3, w2, *[it.operand for it in items])
    return res[0], (res[1], res[2]), list(res[3:])


def ffn_bwd_chunks(x, g, dh, ups, w1, w3, w2, tm, name, items=()):
    seq, dm = x.shape
    nc, _, fc = w1.shape
    nt = seq // tm
    n = len(items)

    def body(*refs):
        x_ref, g_ref, dh_ref, a_ref, b_ref, w1_ref, w3_ref, w2_ref = refs[:8]
        c_in = refs[8:8 + n]
        dxn_ref, dw1_ref, dw3_ref, dw2_ref = refs[8 + n:12 + n]
        c_out = refs[12 + n:12 + 2 * n]
        a1_s, a3_s, a2_s = refs[12 + 2 * n:15 + 2 * n]
        sems = refs[15 + 2 * n:]
        c, i = pl.program_id(0), pl.program_id(1)

        if n:
            @pl.when((i == 0) & (c == 0))
            def _():
                comm_start(items, c_in, c_out, sems)

        @pl.when(i == 0)
        def _():
            a1_s[...] = jnp.zeros_like(a1_s)
            a3_s[...] = jnp.zeros_like(a3_s)
            a2_s[...] = jnp.zeros_like(a2_s)

        xn = rms(x_ref[...], g_ref[...]).astype(BF)
        dy = (0.5 * dh_ref[...]).astype(BF)
        w1v, w3v, w2v = w1_ref[0], w3_ref[0], w2_ref[0]
        a = a_ref[...].astype(F32)
        b = b_ref[...].astype(F32)
        sig = jax.nn.sigmoid(a)
        sa = a * sig
        dhid = _dot(dy, w2v, _NT)
        db = (dhid * sa).astype(BF)
        da = (dhid * b * (sig * (1.0 + a * (1.0 - sig)))).astype(BF)
        dxn_ref[0] = (_dot(da, w1v, _NT) + _dot(db, w3v, _NT)).astype(dxn_ref.dtype)
        a1_s[...] += _dot(xn, da, _TN)
        a3_s[...] += _dot(xn, db, _TN)
        a2_s[...] += _dot((sa * b).astype(BF), dy, _TN)

        @pl.when(i == nt - 1)
        def _():
            dw1_ref[0] = a1_s[...].astype(BF)
            dw3_ref[0] = a3_s[...].astype(BF)
            dw2_ref[0] = a2_s[...].astype(BF)

        if n:
            @pl.when((i == nt - 1) & (c == nc - 1))
            def _():
                comm_wait(items, c_in, c_out, sems)

    res = pl.pallas_call(
        body, grid=(nc, nt),
        in_specs=[pl.BlockSpec((tm, dm), lambda c, i: (i, 0)), pl.BlockSpec((1, dm), lambda c, i: (0, 0)),
                  pl.BlockSpec((tm, dm), lambda c, i: (i, 0)),
                  pl.BlockSpec((tm, fc), lambda c, i: (i, c)), pl.BlockSpec((tm, fc), lambda c, i: (i, c)),
                  pl.BlockSpec((1, dm, fc), lambda c, i: (c, 0, 0)), pl.BlockSpec((1, dm, fc), lambda c, i: (c, 0, 0)),
                  pl.BlockSpec((1, fc, dm), lambda c, i: (c, 0, 0))] + [ANY] * n,
        out_specs=[pl.BlockSpec((1, tm, dm), lambda c, i: (c, i, 0)),
                   pl.BlockSpec((1, dm, fc), lambda c, i: (c, 0, 0)), pl.BlockSpec((1, dm, fc), lambda c, i: (c, 0, 0)),
                   pl.BlockSpec((1, fc, dm), lambda c, i: (c, 0, 0))] + [ANY] * n,
        out_shape=[jax.ShapeDtypeStruct((nc, seq, dm), BF), jax.ShapeDtypeStruct((nc, dm, fc), BF),
                   jax.ShapeDtypeStruct((nc, dm, fc), BF), jax.ShapeDtypeStruct((nc, fc, dm), BF)]
                  + [it.out_shape for it in items],
        scratch_shapes=[pltpu.VMEM((dm, fc), F32), pltpu.VMEM((dm, fc), F32), pltpu.VMEM((fc, dm), F32)]
                       + (_comm_sems(items) if n else []),
        compiler_params=_cparams(("arbitrary", "arbitrary")), name=name,
    )(x, g, dh, ups[0], ups[1], w1, w3, w2, *[it.operand for it in items])
    return res[0], res[1], res[2], res[3], list(res[4:])


def ffn_bwd_norm(x, g, parts, dh, tm, name):
    seq, dm = x.shape
    nc = parts.shape[0]
    nt = seq // tm

    def body(x_ref, g_ref, p_ref, dh_ref, dx_ref, dg_ref):
        i = pl.program_id(0)
        dxn = p_ref[0].astype(F32)
        for c in range(1, nc):
            dxn = dxn + p_ref[c].astype(F32)
        _, vjp = jax.vjp(rms, x_ref[...], g_ref[...])
        dx, dg = vjp(dxn)
        dx_ref[...] = dx + dh_ref[...]

        @pl.when(i == 0)
        def _():
            dg_ref[...] = dg

        @pl.when(i > 0)
        def _():
            dg_ref[...] += dg

    return pl.pallas_call(
        body, grid=(nt,),
        in_specs=[pl.BlockSpec((tm, dm), lambda i: (i, 0)), pl.BlockSpec((1, dm), lambda i: (0, 0)),
                  pl.BlockSpec((nc, tm, dm), lambda i: (0, i, 0)), pl.BlockSpec((tm, dm), lambda i: (i, 0))],
        out_specs=[pl.BlockSpec((tm, dm), lambda i: (i, 0)), pl.BlockSpec((1, dm), lambda i: (0, 0))],
        out_shape=[jax.ShapeDtypeStruct((seq, dm), F32), jax.ShapeDtypeStruct((1, dm), F32)],
        compiler_params=_cparams(("arbitrary",)), name=name,
    )(x, g, parts, dh)


def rope_tables(pos_col, inv_freq, tm):
    seq = pos_col.shape[0]

    def body(p_ref, f_ref, c_ref, s_ref):
        ang = p_ref[...].astype(F32) * f_ref[...]
        lane = lax.broadcasted_iota(jnp.int32, ang.shape, 1)
        c_ref[...] = jnp.cos(ang)
        s_ref[...] = jnp.where(lane < HEAD // 2, -jnp.sin(ang), jnp.sin(ang))

    return pl.pallas_call(
        body, grid=(seq // tm,),
        in_specs=[pl.BlockSpec((tm, 1), lambda i: (i, 0)), pl.BlockSpec((1, HEAD), lambda i: (0, 0))],
        out_specs=[pl.BlockSpec((tm, HEAD), lambda i: (i, 0))] * 2,
        out_shape=[jax.ShapeDtypeStruct((seq, HEAD), F32)] * 2,
        compiler_params=_cparams(("arbitrary",)), name="rope_tables",
    )(pos_col, inv_freq)


def loss_and_grad(y, target, tm):
    seq, dm = y.shape

    def body(y_ref, t_ref, l_ref, dy_ref):
        i = pl.program_id(0)
        err = y_ref[...] - t_ref[...]
        dy_ref[...] = err * (1.0 / dm)
        part = 0.5 * jnp.sum(jnp.mean(err * err, axis=-1, keepdims=True), axis=0, keepdims=True)
        part = jnp.broadcast_to(part, (1, LANE))

        @pl.when(i == 0)
        def _():
            l_ref[...] = part

        @pl.when(i > 0)
        def _():
            l_ref[...] += part

    return pl.pallas_call(
        body, grid=(seq // tm,),
        in_specs=[pl.BlockSpec((tm, dm), lambda i: (i, 0))] * 2,
        out_specs=[pl.BlockSpec((1, LANE), lambda i: (0, 0)), pl.BlockSpec((tm, dm), lambda i: (i, 0))],
        out_shape=[jax.ShapeDtypeStruct((1, LANE), F32), jax.ShapeDtypeStruct((seq, dm), F32)],
        compiler_params=_cparams(("arbitrary",)), name="loss_and_grad",
    )(y, target)


def adamw(w, g, m, v, rows, name):
    r, c = w.shape

    def body(w_ref, g_ref, m_ref, v_ref, d_ref, nm_ref, nv_ref):
        gv = g_ref[...]
        nm = ADAM_B1 * m_ref[...] + (1.0 - ADAM_B1) * gv
        nv = ADAM_B2 * v_ref[...] + (1.0 - ADAM_B2) * (gv * gv)
        m_hat = nm / (1.0 - ADAM_B1 ** ADAM_STEP)
        v_hat = nv / (1.0 - ADAM_B2 ** ADAM_STEP)
        d_ref[...] = -ADAM_LR * (m_hat / (jnp.sqrt(v_hat) + ADAM_EPS) + ADAM_WD * w_ref[...])
        nm_ref[...] = nm
        nv_ref[...] = nv

    spec = pl.BlockSpec((rows, c), lambda i: (i, 0))
    return pl.pallas_call(
        body, grid=(r // rows,), in_specs=[spec] * 4, out_specs=[spec] * 3,
        out_shape=[jax.ShapeDtypeStruct((r, c), F32)] * 3,
        compiler_params=_cparams(("arbitrary",)), name=name,
    )(w, g, m, v)


def sum_layers(recvs, out_rows, out_cols, rows, name):
    depth = len(recvs)
    n, _, c_in = recvs[0].shape

    def body(*refs):
        o_ref = refs[depth]
        for l in range(depth):
            @pl.when(pl.program_id(0) == l)
            def _():
                acc = refs[l][0].astype(F32)
                for j in range(1, n):
                    acc = acc + refs[l][j].astype(F32)
                o_ref[0] = acc[:, :out_cols]

    return pl.pallas_call(
        body, grid=(depth, out_rows // rows),
        in_specs=[pl.BlockSpec((n, rows, c_in), lambda ll, i, l=l: (0, jnp.where(ll == l, i, 0), 0))
                  for l in range(depth)],
        out_specs=pl.BlockSpec((1, rows, out_cols), lambda ll, i: (ll, i, 0)),
        out_shape=jax.ShapeDtypeStruct((depth, out_rows, out_cols), F32),
        compiler_params=_cparams(("arbitrary", "arbitrary")), name=name,
    )(*recvs)


def _heads(t, n):
    return t.reshape(t.shape[0], n, HEAD).transpose(1, 0, 2)


def _unheads(t):
    return t.transpose(1, 0, 2).reshape(t.shape[1], t.shape[0] * HEAD)


def _mkv_heads(mkv):
    mw = N_MEMH * HEAD
    return _heads(mkv[:, :mw], N_MEMH), _heads(mkv[:, mw:], N_MEMH)


def _mix_args(sv, tabs, sw):
    mk, mv = _mkv_heads(sv["mkv"])
    ct, st = tabs
    conv_args = [(sv["a"], HALO, True), (sv["gate"], HALO, True), (sw["conv_w"], PARAM, True),
                 (sw["conv_b"], PARAM, True), (sw["conv_ln_g"], PARAM, True), (sw["conv_ln_b"], PARAM, True)]
    swa_args = [(sv["q"], TILE, True), (sv["k"], HALO, True), (sv["v"], HALO, True), (ct, HALO, False),
                (st, HALO, False), (sw["swa_q_norm"], PARAM, True), (sw["swa_k_norm"], PARAM, True),
                (sw["swa_sinks"], PARAM, True)]
    mem_args = [(sv["qm"], TILE, True), (mk, PARAM, True), (mv, PARAM, True), (sw["mem_q_norm"], PARAM, True),
                (sw["mem_k_norm"], PARAM, True)]
    return conv_args, swa_args, mem_args


def _proj_args(h1, sw, bw):
    return [(h1, TILE, True), (sw["mix_norm"], PARAM, True), (bw["w_in"], PARAM, True)]


def _join_args(sv, bw):
    return [(sv["h1"], TILE, True), (sv["y_conv"], TILE, True), (sv["y_swa"], TILE, True), (sv["y_mem"], TILE, True),
            (bw["w_out"], PARAM, True)]


def layer_fwd(x, mem, tabs, bw, sent, sw, tm, l, depth):
    seq, dm = x.shape
    tag = f"_l{l}"
    sv = dict(x=x)
    sv["h1"], sv["ups1"], got1 = ffn_fwd(x, sw["ffn1_norm"], bw["ffn1_w1"], bw["ffn1_w3"], bw["ffn1_w2"], tm,
                                         "ffn1_fwd" + tag, gather_items(sent, l, GATHER_LATE))
    bw = {**bw, **kernel_layouts(got1, GATHER_LATE)}
    items2 = gather_items(sent, l + 1, GATHER_EARLY) if l + 1 < depth else ()
    split_outs = [((seq, CONV_CH), F32), ((seq, CONV_CH), F32), ((N_Q, seq, HEAD), F32), ((N_KV, seq, HEAD), F32),
                  ((N_KV, seq, HEAD), F32), ((N_MEMH, seq, HEAD), F32)]
    sv["a"], sv["gate"], sv["q"], sv["k"], sv["v"], sv["qm"] = seq_fwd(
        f_proj_split, _proj_args(sv["h1"], sw, bw), split_outs, tm, tm, "proj_in_fwd" + tag)
    ml = mem.shape[0]
    (sv["mkv"],) = seq_fwd(f_proj_in, [(mem, TILE, False), (sw["mem_norm"], PARAM, True), (bw["w_mem_kv"], PARAM, True)],
                           [((ml, bw["w_mem_kv"].shape[1]), F32)], ml, ml, "mem_kv_fwd" + tag)
    conv_args, swa_args, mem_args = _mix_args(sv, tabs, sw)
    (sv["y_conv"],) = seq_fwd(f_conv, conv_args, [((seq, CONV_CH), F32)], tm, 32, "conv_fwd" + tag)
    (sv["y_swa"],) = seq_fwd(f_swa, swa_args, [((N_Q, seq, HEAD), F32)], tm, BLOCK, "swa_fwd" + tag)
    (sv["y_mem"],) = seq_fwd(f_mem, mem_args, [((N_MEMH, seq, HEAD), F32)], tm, tm, "mem_attn_fwd" + tag)
    (sv["h2"],) = seq_fwd(f_out_join, _join_args(sv, bw), [((seq, dm), F32)], tm, tm, "out_proj_fwd" + tag)
    sv["h3"], sv["ups2"], got2 = ffn_fwd(sv["h2"], sw["ffn2_norm"], bw["ffn2_w1"], bw["ffn2_w3"], bw["ffn2_w2"], tm,
                                         "ffn2_fwd" + tag, items2)
    (xo,) = seq_fwd(f_rms, [(sv["h3"], TILE, True), (sw["final_norm"], PARAM, True)], [((seq, dm), F32)], tm, tm,
                    "final_norm_fwd" + tag)
    return xo, sv, bw, kernel_layouts(got2, GATHER_EARLY)


def layer_bwd(dxo, sv, mem, tabs, bw, sw, tm, l, items=()):
    tag = f"_l{l}"
    gb, gs = {}, {}
    dh3, gs["final_norm"] = seq_bwd(f_rms, [(sv["h3"], TILE, True), (sw["final_norm"], PARAM, True)], [dxo], tm, tm,
                                    "final_norm_bwd" + tag)
    parts2, gb["ffn2_w1"], gb["ffn2_w3"], gb["ffn2_w2"], got = ffn_bwd_chunks(
        sv["h2"], sw["ffn2_norm"], dh3, sv["ups2"], bw["ffn2_w1"], bw["ffn2_w3"], bw["ffn2_w2"], tm, "ffn2_bwd" + tag,
        items)
    dh2, gs["ffn2_norm"] = ffn_bwd_norm(sv["h2"], sw["ffn2_norm"], parts2, dh3, tm, "ffn2_norm_bwd" + tag)
    dh1_a, dy_conv, dy_swa, dy_mem, gb["w_out"] = seq_bwd(f_out_join, _join_args(sv, bw), [dh2], tm // 2, tm // 2,
                                                           "out_proj_bwd" + tag)
    conv_args, swa_args, mem_args = _mix_args(sv, tabs, sw)
    da, dgate, gs["conv_w"], gs["conv_b"], gs["conv_ln_g"], gs["conv_ln_b"] = seq_bwd(
        f_conv, conv_args, [dy_conv], tm, 32, "conv_bwd" + tag)
    dq, dk, dv, gs["swa_q_norm"], gs["swa_k_norm"], gs["swa_sinks"] = seq_bwd(
        f_swa, swa_args, [dy_swa], tm, BLOCK, "swa_bwd" + tag)
    dqm, dmk, dmv, gs["mem_q_norm"], gs["mem_k_norm"] = seq_bwd(f_mem, mem_args, [dy_mem], tm, tm, "mem_attn_bwd" + tag)
    dmkv = jnp.concatenate([_unheads(dmk), _unheads(dmv)], axis=-1)
    ml = mem.shape[0]
    gs["mem_norm"], gb["w_mem_kv"] = seq_bwd(
        f_proj_in, [(mem, TILE, False), (sw["mem_norm"], PARAM, True), (bw["w_mem_kv"], PARAM, True)], [dmkv], ml, ml,
        "mem_kv_bwd" + tag)
    dh1, gs["mix_norm"], gb["w_in"] = seq_bwd(f_proj_split, _proj_args(sv["h1"], sw, bw),
                                               [da, dgate, dq, dk, dv, dqm], tm // 2, tm // 2, "proj_in_bwd" + tag,
                                               add_to_first=dh1_a)
    parts1, gb["ffn1_w1"], gb["ffn1_w3"], gb["ffn1_w2"], got_late = ffn_bwd_chunks(
        sv["x"], sw["ffn1_norm"], dh1, sv["ups1"], bw["ffn1_w1"], bw["ffn1_w3"], bw["ffn1_w2"], tm, "ffn1_bwd" + tag,
        scatter_items(gb, GATHER_LATE))
    dx, gs["ffn1_norm"] = ffn_bwd_norm(sv["x"], sw["ffn1_norm"], parts1, dh1, tm, "ffn1_norm_bwd" + tag)
    return dx, gb, gs, got, got_late


FFN_UP = ("ffn1_w1", "ffn1_w3", "ffn2_w1", "ffn2_w3")
FFN_DOWN = ("ffn1_w2", "ffn2_w2")
ROW_SHARDED = ("w_mem_kv", "w_out")
GATHER_EARLY = ("ffn1_w1", "ffn1_w3", "ffn1_w2")
GATHER_LATE = ("w_in", "w_mem_kv", "w_out", "ffn2_w1", "ffn2_w3", "ffn2_w2")
BIG = GATHER_EARLY + GATHER_LATE
SMALL = ("ffn1_norm", "mix_norm", "conv_b", "conv_ln_g", "conv_ln_b", "swa_q_norm", "swa_k_norm", "swa_sinks",
         "mem_norm", "mem_q_norm", "mem_k_norm", "ffn2_norm", "final_norm")
WEIGHTS = ("ffn1_norm", "ffn1_w1", "ffn1_w3", "ffn1_w2", "mix_norm", "w_in", "conv_w", "conv_b", "conv_ln_g",
           "conv_ln_b", "swa_q_norm", "swa_k_norm", "swa_sinks", "mem_norm", "w_mem_kv", "mem_q_norm", "mem_k_norm",
           "w_out", "ffn2_norm", "ffn2_w1", "ffn2_w3", "ffn2_w2", "final_norm")


def _round_up(n, m):
    return -(-n // m) * m


def _row_block(rows, target, mult=8):
    best = rows
    for cand in range(mult, min(rows, target) + 1, mult):
        if rows % cand == 0:
            best = cand
    return best


def _pad_rows(flat, cols, mult):
    n = flat.shape[0]
    rows = _round_up(-(-n // cols), mult)
    return jnp.pad(flat, (0, rows * cols - n)).reshape(rows, cols)


def send_form(w):
    out = {}
    for n in BIG:
        t = w[n].astype(BF)
        if n in FFN_UP:
            t = jnp.pad(t, ((0, 0), (0, 0), (0, _round_up(t.shape[2], LANE) - t.shape[2])))
        elif n in FFN_DOWN:
            t = jnp.pad(t, ((0, 0), (0, _round_up(t.shape[1], LANE) - t.shape[1]), (0, 0)))
        out[n] = t
    return out


def gather_items(sent, l, names):
    return [gather_cols_item(sent[n][l]) if n in FFN_UP else gather_item(sent[n][l]) for n in names]


def kernel_layouts(got, names):
    out = {}
    for n, t in zip(names, got):
        if n in FFN_DOWN:
            t = t.reshape(N_DEV // 2, 2 * t.shape[1], t.shape[2])
        elif n == "w_in":
            t = t.transpose(1, 0, 2).reshape(t.shape[1], N_DEV * t.shape[2])
        elif n in ROW_SHARDED:
            t = t.reshape(N_DEV * t.shape[1], t.shape[2])
        out[n] = t
    return out


def scatter_items(gb, names):
    items = []
    for n in names:
        t = gb[n]
        if n in FFN_UP:
            items.append(scatter_cols_item(t))
            continue
        if n in FFN_DOWN:
            t = t.reshape(N_DEV, t.shape[1] // 2, t.shape[2])
        elif n == "w_in":
            t = t.reshape(t.shape[0], N_DEV, t.shape[1] // N_DEV).transpose(1, 0, 2).astype(BF)
        else:
            t = t.reshape(N_DEV, t.shape[0] // N_DEV, t.shape[1]).astype(BF)
        items.append(scatter_item(t))
    return items


def small_layer_params(w, l):
    sw = {n: w[n][l][None, :] for n in SMALL if n != "swa_sinks"}
    sw["swa_sinks"] = jnp.pad(w["swa_sinks"][l], (0, LANE - N_Q))[None, :]
    sw["conv_w"] = jnp.pad(w["conv_w_full"][l], ((0, 1), (0, 0)))
    return sw


def kernel(x, mem, positions, ffn1_norm, ffn1_w1, ffn1_w3, ffn1_w2, mix_norm, w_in, conv_w, conv_b, conv_ln_g, conv_ln_b, swa_q_norm, swa_k_norm, swa_sinks, mem_norm, w_mem_kv, mem_q_norm, mem_k_norm, w_out, ffn2_norm, ffn2_w1, ffn2_w3, ffn2_w2, final_norm, loss_target, m_ffn1_norm, m_ffn1_w1, m_ffn1_w3, m_ffn1_w2, m_mix_norm, m_w_in, m_conv_w, m_conv_b, m_conv_ln_g, m_conv_ln_b, m_swa_q_norm, m_swa_k_norm, m_swa_sinks, m_mem_norm, m_w_mem_kv, m_mem_q_norm, m_mem_k_norm, m_w_out, m_ffn2_norm, m_ffn2_w1, m_ffn2_w3, m_ffn2_w2, m_final_norm, v_ffn1_norm, v_ffn1_w1, v_ffn1_w3, v_ffn1_w2, v_mix_norm, v_w_in, v_conv_w, v_conv_b, v_conv_ln_g, v_conv_ln_b, v_swa_q_norm, v_swa_k_norm, v_swa_sinks, v_mem_norm, v_w_mem_kv, v_mem_q_norm, v_mem_k_norm, v_w_out, v_ffn2_norm, v_ffn2_w1, v_ffn2_w3, v_ffn2_w2, v_final_norm):
    loc = locals()
    w = {n: loc[n] for n in WEIGHTS}
    m = {n: loc["m_" + n] for n in WEIGHTS}
    v = {n: loc["v_" + n] for n in WEIGHTS}
    depth = ffn1_norm.shape[0]
    seq = x.shape[1]
    tm = min(512, seq)
    me = 4 * lax.axis_index("x") + 2 * lax.axis_index("y") + lax.axis_index("c")
    xs, mems, target = x[0], mem[0], loss_target[0]

    sent = send_form(w)
    cw = conv_w.shape[2]
    conv_rows = _pad_rows(conv_w.reshape(-1), LANE, 8)
    got = comm_call(gather_items(sent, 0, GATHER_EARLY) + [gather_item(conv_rows)], "gather_l0")
    bw = kernel_layouts(got[:-1], GATHER_EARLY)
    conv_full = got[-1].reshape(N_DEV, -1)[:, :conv_w.size].reshape((N_DEV,) + conv_w.shape)
    small = {n: w[n] for n in SMALL}
    small["conv_w_full"] = conv_full.transpose(1, 2, 0, 3).reshape(depth, CONV_W, N_DEV * cw)

    inv = ROPE_THETA ** (-jnp.arange(0, HEAD, 2, dtype=F32) / HEAD)
    tabs = rope_tables(positions[0].reshape(-1, 1), jnp.concatenate([inv, inv])[None, :], tm)
    saved, bws = [], []
    h = xs
    for l in range(depth):
        h, sv, bw_all, bw = layer_fwd(h, mems, tabs, bw, sent, small_layer_params(small, l), tm, l, depth)
        saved.append(sv)
        bws.append(bw_all)
    loss_sum, dh = loss_and_grad(h, target, tm)
    loss = lax.psum(loss_sum[0, 0], ("x", "y", "c"))

    recv = [dict() for _ in range(depth)]
    gss = [None] * depth
    items = ()
    for l in reversed(range(depth)):
        dh, gb, gss[l], got, got_late = layer_bwd(dh, saved[l], mems, tabs, bws[l], small_layer_params(small, l), tm,
                                                  l, items)
        if items:
            recv[l + 1].update(zip(GATHER_EARLY, got))
        recv[l].update(zip(GATHER_LATE, got_late))
        items = scatter_items(gb, GATHER_EARLY)
    recv[0].update(zip(GATHER_EARLY, comm_call(items, "grad_exchange_l0")))

    g_all = {}
    for n in BIG:
        a, b = w[n].shape[1:]
        blocks = [recv[l][n] for l in range(depth)]
        if n in FFN_DOWN:
            rows = _row_block(a, 256, BF16_ROWS)
        else:
            rows = _row_block(a, 256)
        g_all[n] = sum_layers(blocks, a, b, rows, "grad_sum_" + n)

    small_names = SMALL + ("conv_w",)
    sizes = {n: (w[n].shape[1] if n != "conv_w" else CONV_W * N_DEV * cw) for n in small_names}
    flat = []
    for l in range(depth):
        for n in small_names:
            t = gss[l][n]
            if n == "swa_sinks":
                t = t[:, :N_Q]
            elif n == "conv_w":
                t = t[:CONV_W]
            flat.append(t.reshape(-1))
    flat = _pad_rows(jnp.concatenate(flat), LANE, 8)
    (small_got,) = comm_call([gather_item(flat)], "small_grad_gather")
    summed = sum_layers([small_got], flat.shape[0], LANE, flat.shape[0], "small_grad_sum").reshape(-1)
    g_small = {n: [] for n in small_names}
    o = 0
    for l in range(depth):
        for n in small_names:
            g_small[n].append(summed[o:o + sizes[n]])
            o += sizes[n]
    for n in SMALL:
        g_all[n] = jnp.stack(g_small[n])
    gcw = jnp.stack(g_small["conv_w"]).reshape(depth, CONV_W, N_DEV, cw)
    g_all["conv_w"] = lax.dynamic_slice(gcw, (0, 0, me, 0), (depth, CONV_W, 1, cw)).reshape(depth, CONV_W, cw)

    outs = {"delta": {}, "new_m": {}, "new_v": {}}
    for n in BIG:
        shape = w[n].shape
        two_d = (shape[0] * shape[1], shape[2])
        res = adamw(w[n].reshape(two_d), g_all[n].reshape(two_d), m[n].reshape(two_d), v[n].reshape(two_d),
                    _row_block(two_d[0], 512), "adamw_" + n)
        for key, arr in zip(("delta", "new_m", "new_v"), res):
            outs[key][n] = arr.reshape(shape)

    def packed_small(t):
        return _pad_rows(jnp.concatenate([t[n].reshape(-1) for n in small_names]), LANE, 8)

    ps = [packed_small(t) for t in (w, g_all, m, v)]
    res = adamw(*ps, ps[0].shape[0], "adamw_small")
    for key, arr in zip(("delta", "new_m", "new_v"), res):
        fl, o = arr.reshape(-1), 0
        for n in small_names:
            outs[key][n] = fl[o:o + w[n].size].reshape(w[n].shape)
            o += w[n].size

    return (loss, dh[None], *[g_all[n] for n in WEIGHTS], *[outs["delta"][n] for n in WEIGHTS],
            *[outs["new_m"][n] for n in WEIGHTS], *[outs["new_v"][n] for n in WEIGHTS])
```

```python
import jax
import jax.numpy as jnp
from jax import lax
from jax.experimental import pallas as pl
from jax.experimental.pallas import tpu as pltpu

F32 = jnp.float32
BF = jnp.bfloat16
EPS = 1e-6
HEAD = 64
N_Q, N_KV, N_MEMH = 6, 2, 4
CONV_CH, CONV_W = 384, 31
BLOCK = 128
ROPE_THETA = 10000.0
N_DEV = 8
V7X_VMEM_LIMIT = 56 * 1024 * 1024
LANE = 128
BF16_ROWS = 16

ADAM_LR, ADAM_B1, ADAM_B2, ADAM_EPS, ADAM_WD, ADAM_STEP = 0.001, 0.9, 0.999, 1e-08, 0.01, 10

TILE, HALO, PARAM = "tile", "halo", "param"
MESH = pl.DeviceIdType.MESH
ANY = pl.BlockSpec(memory_space=pl.ANY)


def _cparams(sem=None):
    kw = dict(vmem_limit_bytes=V7X_VMEM_LIMIT)
    if sem is not None:
        kw["dimension_semantics"] = sem
    return pltpu.CompilerParams(**kw)


def _dot(a, b, dims):
    return lax.dot_general(a, b, (dims, ((), ())), preferred_element_type=F32)


_NN, _NT, _TN = ((1,), (0,)), ((1,), (1,)), ((0,), (0,))


@jax.custom_vjp
def mm(a, b):
    return _dot(a.astype(BF), b.astype(BF), _NN)


def _mm_fwd(a, b):
    return mm(a, b), (a, b)


def _mm_bwd(res, g):
    a, b = res
    gb = g.astype(BF)
    return _dot(gb, b.astype(BF), _NT), _dot(a.astype(BF), gb, _TN)


mm.defvjp(_mm_fwd, _mm_bwd)


@jax.custom_vjp
def mm_nt(a, b):
    return _dot(a.astype(BF), b.astype(BF), _NT)


def _mm_nt_fwd(a, b):
    return mm_nt(a, b), (a, b)


def _mm_nt_bwd(res, g):
    a, b = res
    gb = g.astype(BF)
    return _dot(gb, b.astype(BF), _NN), _dot(gb, a.astype(BF), _TN)


mm_nt.defvjp(_mm_nt_fwd, _mm_nt_bwd)


def rms(x, g):
    return x * lax.rsqrt(jnp.mean(x * x, axis=-1, keepdims=True) + EPS) * g


def _swap_halves(x):
    half = HEAD // 2
    return jnp.concatenate([x[:, half:], x[:, :half]], axis=1)


@jax.custom_vjp
def rope(x, c, s):
    return x * c + _swap_halves(x) * s


def _rope_fwd(x, c, s):
    return rope(x, c, s), (c, s)


def _rope_bwd(res, g):
    c, s = res
    return g * c + _swap_halves(g * s), jnp.zeros_like(c), jnp.zeros_like(s)


rope.defvjp(_rope_fwd, _rope_bwd)


SUBLANES = 8


def _row_shifts(t, rows):
    return [t] + [t[b:b + rows + 24] for b in range(1, SUBLANES)]


CONV_ROWS = 64


def _window(shifts, offset, start, rows):
    base = offset - offset % SUBLANES + start
    return shifts[offset % SUBLANES][base:base + rows]


def _row_blocks(rows):
    return [(r, min(CONV_ROWS, rows - r)) for r in range(0, rows, CONV_ROWS)]


def _conv_taps(gs, w, tm, first_offset=2, step=1):
    wr = [w[j:j + 1, :] for j in range(CONV_W)]
    blocks = []
    for r, n in _row_blocks(tm):
        acc = wr[0] * _window(gs, first_offset, r, n)
        for j in range(1, CONV_W):
            acc = acc + wr[j] * _window(gs, first_offset + step * j, r, n)
        blocks.append(acc)
    return jnp.concatenate(blocks, axis=0)


@jax.custom_vjp
def causal_dw_conv(g, w):
    tm = g.shape[0] - 32
    return _conv_taps(_row_shifts(g, tm), w, tm)


def _conv_fwd(g, w):
    tm = g.shape[0] - 32
    gs = _row_shifts(g, tm)
    return _conv_taps(gs, w, tm), (gs, w)


def _conv_bwd(res, dc):
    gs, w = res
    tm, ch = dc.shape
    z = jnp.zeros((32, ch), F32)
    ds = _row_shifts(jnp.concatenate([z, dc, z], axis=0), tm + 32)
    dg = _conv_taps(ds, w, tm + 32, first_offset=30, step=-1)
    rows = lax.broadcasted_iota(jnp.int32, (32, 1), 0)
    dcb = [dc[r:r + n] for r, n in _row_blocks(tm)]
    dw = jnp.zeros((32, ch), F32)
    for j in range(CONV_W):
        part = jnp.zeros((SUBLANES, ch), F32)
        for (r, n), d in zip(_row_blocks(tm), dcb):
            prod = d * _window(gs, 2 + j, r, n)
            for s in range(0, n, SUBLANES):
                part = part + prod[s:s + SUBLANES]
        dw = dw + jnp.where(rows == j, jnp.sum(part, axis=0, keepdims=True), 0.0)
    return dg, dw


causal_dw_conv.defvjp(_conv_fwd, _conv_bwd)


def f_proj_in(first, h, g, w):
    return (mm(rms(h, g), w),)


def _to_heads(t, n):
    return jnp.stack([t[:, h * HEAD:(h + 1) * HEAD] for h in range(n)])


def f_proj_split(first, h, g, w):
    p = mm(rms(h, g), w)
    o, outs = 0, []
    for width, n in ((CONV_CH, 0), (CONV_CH, 0), (N_Q * HEAD, N_Q), (N_KV * HEAD, N_KV), (N_KV * HEAD, N_KV),
                     (N_MEMH * HEAD, N_MEMH)):
        t = p[:, o:o + width]
        outs.append(_to_heads(t, n) if n else t)
        o += width
    return tuple(outs)


def f_out_join(first, h, y_conv, y_swa, y_mem, w):
    y = jnp.concatenate([y_conv] + [y_swa[i] for i in range(N_Q)] + [y_mem[i] for i in range(N_MEMH)], axis=1)
    return (h + mm(y, w),)


def f_conv(first, a, gate, w, b, lg, lb):
    keep = 1.0 - first
    av = jnp.concatenate([a[0] * keep, a[1]], axis=0)
    gv = jnp.concatenate([gate[0], gate[1]], axis=0)
    glu = av * jax.nn.sigmoid(gv)
    c = causal_dw_conv(glu, w) + b
    mu = jnp.mean(c, axis=-1, keepdims=True)
    var = jnp.mean(jnp.square(c - mu), axis=-1, keepdims=True)
    z = (c - mu) * lax.rsqrt(var + EPS) * lg + lb
    return (z * jax.nn.sigmoid(z),)


def _softmax_with_extra(s, extra):
    m = jnp.max(s, axis=-1, keepdims=True)
    if extra is not None:
        m = jnp.maximum(m, extra)
    m = lax.stop_gradient(m)
    e = jnp.exp(s - m)
    den = jnp.sum(e, axis=-1, keepdims=True)
    if extra is not None:
        den = den + jnp.exp(extra - m)
    return e / den


def f_swa(first, q, k, v, ct, st, qn, kn, sinks):
    tm = q.shape[1]
    nb = tm // BLOCK
    g = N_Q // N_KV
    c_all = jnp.concatenate([ct[0], ct[1]], axis=0)
    s_all = jnp.concatenate([st[0], st[1]], axis=0)
    qi = lax.broadcasted_iota(jnp.int32, (g * BLOCK, 2 * BLOCK), 0)
    kj = lax.broadcasted_iota(jnp.int32, (g * BLOCK, 2 * BLOCK), 1)
    qpos = jnp.where(qi >= 2 * BLOCK, qi - 2 * BLOCK, jnp.where(qi >= BLOCK, qi - BLOCK, qi)) + BLOCK
    rel = qpos - kj
    band = (rel >= 0) & (rel < BLOCK)
    band_first = band & ((kj >= BLOCK) | (first < 0.5))
    lane = lax.broadcasted_iota(jnp.int32, (g * BLOCK, LANE), 1)
    hrow = lax.broadcasted_iota(jnp.int32, (g * BLOCK, LANE), 0)
    head_in_group = jnp.where(hrow >= 2 * BLOCK, 2, jnp.where(hrow >= BLOCK, 1, 0))
    qr = [rope(rms(q[h], qn), ct[1], st[1]) for h in range(N_Q)]
    outs = [[None] * nb for _ in range(N_Q)]
    for hk in range(N_KV):
        kk = rope(rms(jnp.concatenate([k[0][hk], k[1][hk]], axis=0), kn), c_all, s_all)
        vv = jnp.concatenate([v[0][hk], v[1][hk]], axis=0)
        sel = (lane == head_in_group + hk * g).astype(F32)
        sink_col = jnp.sum(sel * sinks, axis=1, keepdims=True)
        for j in range(nb):
            keys = kk[j * BLOCK:(j + 2) * BLOCK]
            vals = vv[j * BLOCK:(j + 2) * BLOCK]
            qs = jnp.concatenate([qr[hk * g + gg][j * BLOCK:(j + 1) * BLOCK] for gg in range(g)], axis=0)
            s = mm_nt(qs, keys) * (HEAD ** -0.5)
            s = jnp.where(band_first if j == 0 else band, s, -1e30)
            o = mm(_softmax_with_extra(s, sink_col), vals)
            for gg in range(g):
                outs[hk * g + gg][j] = o[gg * BLOCK:(gg + 1) * BLOCK]
    return (jnp.stack([jnp.concatenate(outs[h], axis=0) for h in range(N_Q)]),)


def f_mem(first, qm, mk, mv, qn, kn):
    outs = []
    for h in range(N_MEMH):
        qh = rms(qm[h], qn)
        kh = rms(mk[h], kn)
        s = mm_nt(qh, kh) * (HEAD ** -0.5)
        outs.append(mm(_softmax_with_extra(s, None), mv[h]))
    return (jnp.stack(outs),)


def _seq_len(arr):
    return arr.shape[0] if arr.ndim == 2 else arr.shape[1]


def _tile_spec(arr, rows, imap):
    if arr.ndim == 2:
        return pl.BlockSpec((rows, arr.shape[1]), lambda i: (imap(i), 0))
    return pl.BlockSpec((arr.shape[0], rows, arr.shape[2]), lambda i: (0, imap(i), 0))


def _full_spec(arr):
    nd = arr.ndim
    return pl.BlockSpec(arr.shape, lambda i: (0,) * nd)


def _in_specs(args, tm, hl, nt):
    specs = []
    ratio = tm // hl
    cur = lambda i: jnp.minimum(i, nt - 1)
    prev = lambda i: jnp.maximum(jnp.minimum(i, nt - 1) * ratio - 1, 0)
    for arr, kind, _ in args:
        if kind == TILE:
            specs.append(_tile_spec(arr, tm, cur))
        elif kind == HALO:
            specs.append(_tile_spec(arr, hl, prev))
            specs.append(_tile_spec(arr, tm, cur))
        else:
            specs.append(_full_spec(arr))
    return specs


def _operands(args):
    ops = []
    for arr, kind, _ in args:
        ops.append(arr)
        if kind == HALO:
            ops.append(arr)
    return ops


def _load_values(args, refs):
    vals, k = [], 0
    for arr, kind, _ in args:
        if kind == HALO:
            vals.append((refs[k][...].astype(F32), refs[k + 1][...].astype(F32)))
            k += 2
        else:
            vals.append(refs[k][...].astype(F32))
            k += 1
    return vals


def seq_fwd(f, args, outs, tm, hl, name):
    out_shape = [jax.ShapeDtypeStruct(s, d) for s, d in outs]
    nt = _seq_len(out_shape[0]) // tm
    n_in = len(_operands(args))

    def body(*refs):
        first = (pl.program_id(0) == 0).astype(F32)
        res = f(first, *_load_values(args, refs[:n_in]))
        for r, o in zip(refs[n_in:], res):
            r[...] = o.astype(r.dtype)

    cur = lambda i: i
    return pl.pallas_call(
        body, grid=(nt,), in_specs=_in_specs(args, tm, hl, nt),
        out_specs=[_tile_spec(o, tm, cur) for o in out_shape], out_shape=out_shape,
        compiler_params=_cparams(("arbitrary",)), name=name,
    )(*_operands(args))


def seq_bwd(f, args, douts, tm, hl, name, add_to_first=None):
    seq = _seq_len(douts[0])
    nt = seq // tm
    lag = any(kind == HALO and diff for _, kind, diff in args)
    steps = nt + 1 if lag else nt
    n_in = len(_operands(args))
    n_do = len(douts)
    dargs = [(arr, kind) for arr, kind, diff in args if diff]
    extra = [] if add_to_first is None else [add_to_first]
    assert not extra or (dargs[0][1] == TILE and not lag)

    def body(*refs):
        in_refs = refs[:n_in]
        do_refs = refs[n_in:n_in + n_do]
        add_refs = refs[n_in + n_do:n_in + n_do + len(extra)]
        refs = refs[len(extra):]
        g_refs = refs[n_in + n_do:n_in + n_do + len(dargs)]
        carries = refs[n_in + n_do + len(dargs):]
        i = pl.program_id(0)
        first = (i == 0).astype(F32)

        def compute():
            vals = _load_values(args, in_refs)
            dvals = [v for v, (_, _, diff) in zip(vals, args) if diff]

            def fd(*dv):
                it = iter(dv)
                return f(first, *[next(it) if diff else v for v, (_, _, diff) in zip(vals, args)])

            _, vjp = jax.vjp(fd, *dvals)
            grads = vjp(tuple(r[...].astype(F32) for r in do_refs))
            c = 0
            for pos, (gref, gval, (arr, kind)) in enumerate(zip(g_refs, grads, dargs)):
                if kind == TILE:
                    if pos == 0 and extra:
                        gval = gval + add_refs[0][...]
                    gref[...] = gval.astype(gref.dtype)
                elif kind == PARAM:
                    @pl.when(i == 0)
                    def _():
                        gref[...] = gval

                    @pl.when(i > 0)
                    def _():
                        gref[...] += gval
                else:
                    carry = carries[c]
                    c += 1
                    g_prev, g_cur = gval

                    @pl.when(i > 0)
                    def _():
                        gref[...] = carry[...]
                        if arr.ndim == 2:
                            gref[tm - hl:tm, :] += g_prev
                        else:
                            gref[:, tm - hl:tm, :] += g_prev

                    carry[...] = g_cur

        if lag:
            pl.when(i < nt)(compute)

            @pl.when(i == nt)
            def _():
                c = 0
                for gref, (arr, kind) in zip(g_refs, dargs):
                    if kind == HALO:
                        gref[...] = carries[c][...]
                        c += 1
        else:
            compute()

    cur = lambda i: jnp.minimum(i, nt - 1)
    lagged = lambda i: jnp.maximum(i - 1, 0)
    out_shape, out_specs, scratch = [], [], []
    for arr, kind in dargs:
        out_shape.append(jax.ShapeDtypeStruct(arr.shape, F32))
        if kind == PARAM:
            out_specs.append(_full_spec(arr))
        else:
            out_specs.append(_tile_spec(arr, tm, lagged if kind == HALO else cur))
            if kind == HALO:
                blk = (tm, arr.shape[1]) if arr.ndim == 2 else (arr.shape[0], tm, arr.shape[2])
                scratch.append(pltpu.VMEM(blk, F32))
    in_specs = _in_specs(args, tm, hl, nt) + [_tile_spec(d, tm, cur) for d in list(douts) + extra]
    return pl.pallas_call(
        body, grid=(steps,), in_specs=in_specs, out_specs=out_specs, out_shape=out_shape,
        scratch_shapes=scratch, compiler_params=_cparams(("arbitrary",)), name=name,
    )(*_operands(args), *douts, *extra)


class Item:
    def __init__(self, operand, out_shape, src, dst, two_level=False):
        self.operand, self.out_shape, self.src, self.dst, self.two_level = operand, out_shape, src, dst, two_level


def _dev(p):
    return 4 * p[0] + 2 * p[1] + p[2]


def gather_item(shard):
    return Item(shard, jax.ShapeDtypeStruct((N_DEV,) + shard.shape, shard.dtype),
                lambda r, peer: r, lambda r, s: r.at[_dev(s)], two_level=True)


def gather_cols_item(shard):
    d, w = shard.shape
    return Item(shard, jax.ShapeDtypeStruct((N_DEV // 2, d, 2 * w), shard.dtype),
                lambda r, peer: r, lambda r, s: r.at[2 * s[0] + s[1], :, pl.ds(s[2] * w, w)], two_level=True)


def scatter_item(blocks):
    return Item(blocks, jax.ShapeDtypeStruct(blocks.shape, blocks.dtype),
                lambda r, peer: r.at[_dev(peer)], lambda r, s: r.at[_dev(s)])


def scatter_cols_item(full):
    n, d, w2 = full.shape
    w = w2 // 2
    return Item(full, jax.ShapeDtypeStruct((N_DEV, d, w), full.dtype),
                lambda r, peer: r.at[2 * peer[0] + peer[1], :, pl.ds(peer[2] * w, w)], lambda r, s: r.at[_dev(s)])


def _comm_sems(items):
    n = len(items) * (N_DEV - 1)
    return [pltpu.SemaphoreType.DMA((n,)), pltpu.SemaphoreType.DMA((n,)), pltpu.SemaphoreType.DMA((len(items),))]


PASSED_ON = (3, 5, 7)


def _comm_copies(items, in_refs, out_refs, sems, x, y, c):
    send_sems, recv_sems, local_sems = sems
    me = (x, y, c)
    sibling = (x, y, 1 - c)
    local, first, passed = [], [], []

    def remote(t, k, src, dst, to):
        n = t * (N_DEV - 1) + k - 1
        return lambda: pltpu.make_async_remote_copy(
            src_ref=src(), dst_ref=dst(), send_sem=send_sems.at[n], recv_sem=recv_sems.at[n],
            device_id=(to[0], to[1], jnp.int32(to[2])), device_id_type=MESH)

    for t, it in enumerate(items):
        local.append(lambda t=t, it=it: pltpu.make_async_copy(it.src(in_refs[t], me), it.dst(out_refs[t], me),
                                                               local_sems.at[t]))
        for k in range(1, N_DEV):
            peer = (1 - x if k & 4 else x, 1 - y if k & 2 else y, 1 - c if k & 1 else c)
            if it.two_level and k in PASSED_ON:
                origin = (peer[0], peer[1], c)
                landed = lambda t=t, it=it, origin=origin: it.dst(out_refs[t], origin)
                passed.append((t, k, remote(t, k, landed, landed, sibling)))
            else:
                first.append((t, k, remote(t, k, lambda t=t, it=it, peer=peer: it.src(in_refs[t], peer),
                                           lambda t=t, it=it: it.dst(out_refs[t], me), peer)))
    return local, first, passed


def _on_my_core(fn):
    x, y, c = lax.axis_index("x"), lax.axis_index("y"), lax.axis_index("c")
    for cv in (0, 1):
        @pl.when(c == cv)
        def _():
            fn(x, y, cv)


def comm_start(items, in_refs, out_refs, sems):
    def go(x, y, c):
        local, first, _ = _comm_copies(items, in_refs, out_refs, sems, x, y, c)
        for make in local:
            make().start()
        for _, _, make in first:
            make().start()

    _on_my_core(go)


def comm_pass_on(items, in_refs, out_refs, sems):
    def go(x, y, c):
        _, first, passed = _comm_copies(items, in_refs, out_refs, sems, x, y, c)
        arrived = {(t, k): make for t, k, make in first}
        for t, k, make in passed:
            arrived[(t, k - 1)]().wait_recv()
            make().start()

    _on_my_core(go)


def comm_wait(items, in_refs, out_refs, sems):
    def go(x, y, c):
        local, first, passed = _comm_copies(items, in_refs, out_refs, sems, x, y, c)
        waited = {(t, k - 1) for t, k, _ in passed}
        for t, k, make in first:
            cp = make()
            if (t, k) not in waited:
                cp.wait_recv()
            cp.wait_send()
        for _, _, make in passed:
            cp = make()
            cp.wait_recv()
            cp.wait_send()
        for make in local:
            make().wait()

    _on_my_core(go)


def comm_call(items, name):
    n = len(items)

    def body(*refs):
        in_refs, out_refs, sems = refs[:n], refs[n:2 * n], refs[2 * n:]
        comm_start(items, in_refs, out_refs, sems)
        comm_pass_on(items, in_refs, out_refs, sems)
        comm_wait(items, in_refs, out_refs, sems)

    return pl.pallas_call(
        body, in_specs=[ANY] * n, out_specs=[ANY] * n, out_shape=[it.out_shape for it in items],
        scratch_shapes=_comm_sems(items), name=name,
    )(*[it.operand for it in items])


def ffn_fwd(x, g, w1, w3, w2, tm, name, items=(), out_norm=None):
    seq, dm = x.shape
    nc, _, fc = w1.shape
    nt = seq // tm
    n = len(items)
    e = 0 if out_norm is None else 1

    def body(*refs):
        x_ref, g_ref, w1_ref, w3_ref, w2_ref = refs[:5]
        refs = refs[5:]
        gf_ref, refs = refs[:e], refs[e:]
        c_in, (h_ref, a_ref, b_ref), refs = refs[:n], refs[n:n + 3], refs[n + 3:]
        xo_ref, refs = refs[:e], refs[e:]
        c_out, (xn_s, acc_s), sems = refs[:n], refs[n:n + 2], refs[n + 2:]
        i, c = pl.program_id(0), pl.program_id(1)

        if n:
            @pl.when((i == 0) & (c == 0))
            def _():
                comm_start(items, c_in, c_out, sems)

        @pl.when(c == 0)
        def _():
            xn_s[...] = rms(x_ref[...], g_ref[...]).astype(BF)
            acc_s[...] = jnp.zeros_like(acc_s)

        xn = xn_s[...]
        a = _dot(xn, w1_ref[0], _NN)
        b = _dot(xn, w3_ref[0], _NN)
        a_ref[...] = a.astype(BF)
        b_ref[...] = b.astype(BF)
        hid = (a * jax.nn.sigmoid(a)) * b
        acc_s[...] += _dot(hid.astype(BF), w2_ref[0], _NN)

        @pl.when(c == nc - 1)
        def _():
            h = x_ref[...] + 0.5 * acc_s[...]
            h_ref[...] = h
            if e:
                xo_ref[0][...] = rms(h, gf_ref[0][...])

        if n:
            step = i * nc + c
            total = nt * nc

            @pl.when(step == max(1, (13 * total) // 16))
            def _():
                comm_pass_on(items, c_in, c_out, sems)

            @pl.when(step == total - 1)
            def _():
                comm_wait(items, c_in, c_out, sems)

    res = pl.pallas_call(
        body, grid=(nt, nc),
        in_specs=[pl.BlockSpec((tm, dm), lambda i, c: (i, 0)), pl.BlockSpec((1, dm), lambda i, c: (0, 0)),
                  pl.BlockSpec((1, dm, fc), lambda i, c: (c, 0, 0)), pl.BlockSpec((1, dm, fc), lambda i, c: (c, 0, 0)),
                  pl.BlockSpec((1, fc, dm), lambda i, c: (c, 0, 0))]
                 + [pl.BlockSpec((1, dm), lambda i, c: (0, 0))] * e + [ANY] * n,
        out_specs=[pl.BlockSpec((tm, dm), lambda i, c: (i, 0)), pl.BlockSpec((tm, fc), lambda i, c: (i, c)),
                   pl.BlockSpec((tm, fc), lambda i, c: (i, c))]
                  + [pl.BlockSpec((tm, dm), lambda i, c: (i, 0))] * e + [ANY] * n,
        out_shape=[jax.ShapeDtypeStruct((seq, dm), F32), jax.ShapeDtypeStruct((seq, nc * fc), BF),
                   jax.ShapeDtypeStruct((seq, nc * fc), BF)] + [jax.ShapeDtypeStruct((seq, dm), F32)] * e
                  + [it.out_shape for it in items],
        scratch_shapes=[pltpu.VMEM((tm, dm), BF), pltpu.VMEM((tm, dm), F32)] + (_comm_sems(items) if n else []),
        compiler_params=_cparams(("arbitrary", "arbitrary")), name=name,
    )(x, g, w1, w3, w2, *([out_norm] * e), *[it.operand for it in items])
    return res[0], (res[1], res[2]), (res[3] if e else None), list(res[3 + e:])


def ffn_bwd_chunks(x, g, dh, ups, w1, w3, w2, tm, name, items=()):
    seq, dm = x.shape
    nc, _, fc = w1.shape
    nt = seq // tm
    n = len(items)

    def body(*refs):
        x_ref, g_ref, dh_ref, a_ref, b_ref, w1_ref, w3_ref, w2_ref = refs[:8]
        c_in = refs[8:8 + n]
        dxn_ref, dw1_ref, dw3_ref, dw2_ref = refs[8 + n:12 + n]
        c_out = refs[12 + n:12 + 2 * n]
        a1_s, a3_s, a2_s = refs[12 + 2 * n:15 + 2 * n]
        sems = refs[15 + 2 * n:]
        c, i = pl.program_id(0), pl.program_id(1)

        if n:
            @pl.when((i == 0) & (c == 0))
            def _():
                comm_start(items, c_in, c_out, sems)

        @pl.when(i == 0)
        def _():
            a1_s[...] = jnp.zeros_like(a1_s)
            a3_s[...] = jnp.zeros_like(a3_s)
            a2_s[...] = jnp.zeros_like(a2_s)

        xn = rms(x_ref[...], g_ref[...]).astype(BF)
        dy = (0.5 * dh_ref[...]).astype(BF)
        w1v, w3v, w2v = w1_ref[0], w3_ref[0], w2_ref[0]
        a = a_ref[...].astype(F32)
        b = b_ref[...].astype(F32)
        sig = jax.nn.sigmoid(a)
        sa = a * sig
        dhid = _dot(dy, w2v, _NT)
        db = (dhid * sa).astype(BF)
        da = (dhid * b * (sig * (1.0 + a * (1.0 - sig)))).astype(BF)
        dxn_ref[0] = (_dot(da, w1v, _NT) + _dot(db, w3v, _NT)).astype(dxn_ref.dtype)
        a1_s[...] += _dot(xn, da, _TN)
        a3_s[...] += _dot(xn, db, _TN)
        a2_s[...] += _dot((sa * b).astype(BF), dy, _TN)

        @pl.when(i == nt - 1)
        def _():
            dw1_ref[0] = a1_s[...].astype(BF)
            dw3_ref[0] = a3_s[...].astype(BF)
            dw2_ref[0] = a2_s[...].astype(BF)

        if n:
            @pl.when((i == nt - 1) & (c == nc - 1))
            def _():
                comm_wait(items, c_in, c_out, sems)

    res = pl.pallas_call(
        body, grid=(nc, nt),
        in_specs=[pl.BlockSpec((tm, dm), lambda c, i: (i, 0)), pl.BlockSpec((1, dm), lambda c, i: (0, 0)),
                  pl.BlockSpec((tm, dm), lambda c, i: (i, 0)),
                  pl.BlockSpec((tm, fc), lambda c, i: (i, c)), pl.BlockSpec((tm, fc), lambda c, i: (i, c)),
                  pl.BlockSpec((1, dm, fc), lambda c, i: (c, 0, 0)), pl.BlockSpec((1, dm, fc), lambda c, i: (c, 0, 0)),
                  pl.BlockSpec((1, fc, dm), lambda c, i: (c, 0, 0))] + [ANY] * n,
        out_specs=[pl.BlockSpec((1, tm, dm), lambda c, i: (c, i, 0)),
                   pl.BlockSpec((1, dm, fc), lambda c, i: (c, 0, 0)), pl.BlockSpec((1, dm, fc), lambda c, i: (c, 0, 0)),
                   pl.BlockSpec((1, fc, dm), lambda c, i: (c, 0, 0))] + [ANY] * n,
        out_shape=[jax.ShapeDtypeStruct((nc, seq, dm), BF), jax.ShapeDtypeStruct((nc, dm, fc), BF),
                   jax.ShapeDtypeStruct((nc, dm, fc), BF), jax.ShapeDtypeStruct((nc, fc, dm), BF)]
                  + [it.out_shape for it in items],
        scratch_shapes=[pltpu.VMEM((dm, fc), F32), pltpu.VMEM((dm, fc), F32), pltpu.VMEM((fc, dm), F32)]
                       + (_comm_sems(items) if n else []),
        compiler_params=_cparams(("arbitrary", "arbitrary")), name=name,
    )(x, g, dh, ups[0], ups[1], w1, w3, w2, *[it.operand for it in items])
    return res[0], res[1], res[2], res[3], list(res[4:])


def _accumulate(ref, val, i):
    @pl.when(i == 0)
    def _():
        ref[...] = val

    @pl.when(i > 0)
    def _():
        ref[...] += val


def _through_norm_below(dx, below_refs):
    h3_ref, gf_ref = below_refs
    _, vjp = jax.vjp(rms, h3_ref[...], gf_ref[...])
    return vjp(dx)


def ffn_bwd_norm(x, g, parts, dh, tm, name, below=None):
    seq, dm = x.shape
    nc = parts.shape[0]
    nt = seq // tm
    nb = 0 if below is None else 2

    def body(*refs):
        x_ref, g_ref, p_ref, dh_ref = refs[:4]
        dx_ref, dg_ref = refs[4 + nb:6 + nb]
        i = pl.program_id(0)
        dxn = p_ref[0].astype(F32)
        for c in range(1, nc):
            dxn = dxn + p_ref[c].astype(F32)
        _, vjp = jax.vjp(rms, x_ref[...], g_ref[...])
        dx, dg = vjp(dxn)
        dx = dx + dh_ref[...]
        if nb:
            dx, dgf = _through_norm_below(dx, refs[4:6])
            _accumulate(refs[6 + nb], dgf, i)
        dx_ref[...] = dx
        _accumulate(dg_ref, dg, i)

    tile = pl.BlockSpec((tm, dm), lambda i: (i, 0))
    row = pl.BlockSpec((1, dm), lambda i: (0, 0))
    return pl.pallas_call(
        body, grid=(nt,),
        in_specs=[tile, row, pl.BlockSpec((nc, tm, dm), lambda i: (0, i, 0)), tile] + [tile, row][:nb],
        out_specs=[tile, row] + [row] * (nb // 2),
        out_shape=[jax.ShapeDtypeStruct((seq, dm), F32)] + [jax.ShapeDtypeStruct((1, dm), F32)] * (1 + nb // 2),
        compiler_params=_cparams(("arbitrary",)), name=name,
    )(x, g, parts, dh, *(below or ()))


def rope_tables(pos_col, inv_freq, tm):
    seq = pos_col.shape[0]

    def body(p_ref, f_ref, c_ref, s_ref):
        ang = p_ref[...].astype(F32) * f_ref[...]
        lane = lax.broadcasted_iota(jnp.int32, ang.shape, 1)
        c_ref[...] = jnp.cos(ang)
        s_ref[...] = jnp.where(lane < HEAD // 2, -jnp.sin(ang), jnp.sin(ang))

    return pl.pallas_call(
        body, grid=(seq // tm,),
        in_specs=[pl.BlockSpec((tm, 1), lambda i: (i, 0)), pl.BlockSpec((1, HEAD), lambda i: (0, 0))],
        out_specs=[pl.BlockSpec((tm, HEAD), lambda i: (i, 0))] * 2,
        out_shape=[jax.ShapeDtypeStruct((seq, HEAD), F32)] * 2,
        compiler_params=_cparams(("arbitrary",)), name="rope_tables",
    )(pos_col, inv_freq)


def loss_and_grad(h3, gf, target, tm):
    seq, dm = h3.shape

    def body(h_ref, g_ref, t_ref, l_ref, dh_ref, dg_ref):
        i = pl.program_id(0)
        y, vjp = jax.vjp(rms, h_ref[...], g_ref[...])
        err = y - t_ref[...]
        dh, dg = vjp(err * (1.0 / dm))
        dh_ref[...] = dh
        _accumulate(dg_ref, dg, i)
        part = 0.5 * jnp.sum(jnp.mean(err * err, axis=-1, keepdims=True), axis=0, keepdims=True)
        _accumulate(l_ref, jnp.broadcast_to(part, (1, LANE)), i)

    tile = pl.BlockSpec((tm, dm), lambda i: (i, 0))
    row = pl.BlockSpec((1, dm), lambda i: (0, 0))
    return pl.pallas_call(
        body, grid=(seq // tm,),
        in_specs=[tile, row, tile],
        out_specs=[pl.BlockSpec((1, LANE), lambda i: (0, 0)), tile, row],
        out_shape=[jax.ShapeDtypeStruct((1, LANE), F32), jax.ShapeDtypeStruct((seq, dm), F32),
                   jax.ShapeDtypeStruct((1, dm), F32)],
        compiler_params=_cparams(("arbitrary",)), name="loss_and_grad",
    )(h3, gf, target)


def adamw(w, g, m, v, rows, name):
    r, c = w.shape

    def body(w_ref, g_ref, m_ref, v_ref, d_ref, nm_ref, nv_ref):
        gv = g_ref[...]
        nm = ADAM_B1 * m_ref[...] + (1.0 - ADAM_B1) * gv
        nv = ADAM_B2 * v_ref[...] + (1.0 - ADAM_B2) * (gv * gv)
        m_hat = nm / (1.0 - ADAM_B1 ** ADAM_STEP)
        v_hat = nv / (1.0 - ADAM_B2 ** ADAM_STEP)
        d_ref[...] = -ADAM_LR * (m_hat / (jnp.sqrt(v_hat) + ADAM_EPS) + ADAM_WD * w_ref[...])
        nm_ref[...] = nm
        nv_ref[...] = nv

    spec = pl.BlockSpec((rows, c), lambda i: (i, 0))
    return pl.pallas_call(
        body, grid=(r // rows,), in_specs=[spec] * 4, out_specs=[spec] * 3,
        out_shape=[jax.ShapeDtypeStruct((r, c), F32)] * 3,
        compiler_params=_cparams(("arbitrary",)), name=name,
    )(w, g, m, v)


def sum_layers(recvs, out_rows, out_cols, rows, name):
    depth = len(recvs)
    n, _, c_in = recvs[0].shape

    def body(*refs):
        o_ref = refs[depth]
        for l in range(depth):
            @pl.when(pl.program_id(0) == l)
            def _():
                acc = refs[l][0].astype(F32)
                for j in range(1, n):
                    acc = acc + refs[l][j].astype(F32)
                o_ref[0] = acc[:, :out_cols]

    return pl.pallas_call(
        body, grid=(depth, out_rows // rows),
        in_specs=[pl.BlockSpec((n, rows, c_in), lambda ll, i, l=l: (0, jnp.where(ll == l, i, 0), 0))
                  for l in range(depth)],
        out_specs=pl.BlockSpec((1, rows, out_cols), lambda ll, i: (ll, i, 0)),
        out_shape=jax.ShapeDtypeStruct((depth, out_rows, out_cols), F32),
        compiler_params=_cparams(("arbitrary", "arbitrary")), name=name,
    )(*recvs)


def _heads(t, n):
    return t.reshape(t.shape[0], n, HEAD).transpose(1, 0, 2)


def _unheads(t):
    return t.transpose(1, 0, 2).reshape(t.shape[1], t.shape[0] * HEAD)


def _mkv_heads(mkv):
    mw = N_MEMH * HEAD
    return _heads(mkv[:, :mw], N_MEMH), _heads(mkv[:, mw:], N_MEMH)


def _mix_args(sv, tabs, sw):
    mk, mv = _mkv_heads(sv["mkv"])
    ct, st = tabs
    conv_args = [(sv["a"], HALO, True), (sv["gate"], HALO, True), (sw["conv_w"], PARAM, True),
                 (sw["conv_b"], PARAM, True), (sw["conv_ln_g"], PARAM, True), (sw["conv_ln_b"], PARAM, True)]
    swa_args = [(sv["q"], TILE, True), (sv["k"], HALO, True), (sv["v"], HALO, True), (ct, HALO, False),
                (st, HALO, False), (sw["swa_q_norm"], PARAM, True), (sw["swa_k_norm"], PARAM, True),
                (sw["swa_sinks"], PARAM, True)]
    mem_args = [(sv["qm"], TILE, True), (mk, PARAM, True), (mv, PARAM, True), (sw["mem_q_norm"], PARAM, True),
                (sw["mem_k_norm"], PARAM, True)]
    return conv_args, swa_args, mem_args


def _proj_args(h1, sw, bw):
    return [(h1, TILE, True), (sw["mix_norm"], PARAM, True), (bw["w_in"], PARAM, True)]


def _join_args(sv, bw):
    return [(sv["h1"], TILE, True), (sv["y_conv"], TILE, True), (sv["y_swa"], TILE, True), (sv["y_mem"], TILE, True),
            (bw["w_out"], PARAM, True)]


def layer_fwd(x, mem, tabs, bw, sent, sw, tm, l, depth):
    seq, dm = x.shape
    tag = f"_l{l}"
    sv = dict(x=x)
    sv["h1"], sv["ups1"], _, got1 = ffn_fwd(x, sw["ffn1_norm"], bw["ffn1_w1"], bw["ffn1_w3"], bw["ffn1_w2"], tm,
                                            "ffn1_fwd" + tag, gather_items(sent, l, GATHER_LATE))
    bw = {**bw, **kernel_layouts(got1, GATHER_LATE)}
    items2 = gather_items(sent, l + 1, GATHER_EARLY) if l + 1 < depth else ()
    split_outs = [((seq, CONV_CH), F32), ((seq, CONV_CH), F32), ((N_Q, seq, HEAD), F32), ((N_KV, seq, HEAD), F32),
                  ((N_KV, seq, HEAD), F32), ((N_MEMH, seq, HEAD), F32)]
    sv["a"], sv["gate"], sv["q"], sv["k"], sv["v"], sv["qm"] = seq_fwd(
        f_proj_split, _proj_args(sv["h1"], sw, bw), split_outs, tm, tm, "proj_in_fwd" + tag)
    ml = mem.shape[0]
    (sv["mkv"],) = seq_fwd(f_proj_in, [(mem, TILE, False), (sw["mem_norm"], PARAM, True), (bw["w_mem_kv"], PARAM, True)],
                           [((ml, bw["w_mem_kv"].shape[1]), F32)], ml, ml, "mem_kv_fwd" + tag)
    conv_args, swa_args, mem_args = _mix_args(sv, tabs, sw)
    (sv["y_conv"],) = seq_fwd(f_conv, conv_args, [((seq, CONV_CH), F32)], tm, 32, "conv_fwd" + tag)
    (sv["y_swa"],) = seq_fwd(f_swa, swa_args, [((N_Q, seq, HEAD), F32)], tm, BLOCK, "swa_fwd" + tag)
    (sv["y_mem"],) = seq_fwd(f_mem, mem_args, [((N_MEMH, seq, HEAD), F32)], tm, tm, "mem_attn_fwd" + tag)
    (sv["h2"],) = seq_fwd(f_out_join, _join_args(sv, bw), [((seq, dm), F32)], tm, tm, "out_proj_fwd" + tag)
    sv["h3"], sv["ups2"], xo, got2 = ffn_fwd(sv["h2"], sw["ffn2_norm"], bw["ffn2_w1"], bw["ffn2_w3"], bw["ffn2_w2"],
                                             tm, "ffn2_fwd" + tag, items2,
                                             out_norm=sw["final_norm"] if l + 1 < depth else None)
    return xo, sv, bw, kernel_layouts(got2, GATHER_EARLY)


def layer_bwd(dh3, sv, mem, tabs, bw, sw, tm, l, items=(), below=None):
    tag = f"_l{l}"
    gb, gs = {}, {}
    parts2, gb["ffn2_w1"], gb["ffn2_w3"], gb["ffn2_w2"], got = ffn_bwd_chunks(
        sv["h2"], sw["ffn2_norm"], dh3, sv["ups2"], bw["ffn2_w1"], bw["ffn2_w3"], bw["ffn2_w2"], tm, "ffn2_bwd" + tag,
        items)
    dh2, gs["ffn2_norm"] = ffn_bwd_norm(sv["h2"], sw["ffn2_norm"], parts2, dh3, tm, "ffn2_norm_bwd" + tag)
    dh1_a, dy_conv, dy_swa, dy_mem, gb["w_out"] = seq_bwd(f_out_join, _join_args(sv, bw), [dh2], tm, tm,
                                                           "out_proj_bwd" + tag)
    conv_args, swa_args, mem_args = _mix_args(sv, tabs, sw)
    da, dgate, gs["conv_w"], gs["conv_b"], gs["conv_ln_g"], gs["conv_ln_b"] = seq_bwd(
        f_conv, conv_args, [dy_conv], tm, 32, "conv_bwd" + tag)
    dq, dk, dv, gs["swa_q_norm"], gs["swa_k_norm"], gs["swa_sinks"] = seq_bwd(
        f_swa, swa_args, [dy_swa], tm, BLOCK, "swa_bwd" + tag)
    dqm, dmk, dmv, gs["mem_q_norm"], gs["mem_k_norm"] = seq_bwd(f_mem, mem_args, [dy_mem], tm, tm, "mem_attn_bwd" + tag)
    dmkv = jnp.concatenate([_unheads(dmk), _unheads(dmv)], axis=-1)
    ml = mem.shape[0]
    gs["mem_norm"], gb["w_mem_kv"] = seq_bwd(
        f_proj_in, [(mem, TILE, False), (sw["mem_norm"], PARAM, True), (bw["w_mem_kv"], PARAM, True)], [dmkv], ml, ml,
        "mem_kv_bwd" + tag)
    dh1, gs["mix_norm"], gb["w_in"] = seq_bwd(f_proj_split, _proj_args(sv["h1"], sw, bw),
                                               [da, dgate, dq, dk, dv, dqm], tm // 2, tm // 2, "proj_in_bwd" + tag,
                                               add_to_first=dh1_a)
    parts1, gb["ffn1_w1"], gb["ffn1_w3"], gb["ffn1_w2"], got_late = ffn_bwd_chunks(
        sv["x"], sw["ffn1_norm"], dh1, sv["ups1"], bw["ffn1_w1"], bw["ffn1_w3"], bw["ffn1_w2"], tm, "ffn1_bwd" + tag,
        scatter_items(gb, GATHER_LATE))
    dx, gs["ffn1_norm"], *g_below = ffn_bwd_norm(sv["x"], sw["ffn1_norm"], parts1, dh1, tm, "ffn1_norm_bwd" + tag,
                                                 below)
    return dx, gb, gs, got, got_late, (g_below[0] if g_below else None)


FFN_UP = ("ffn1_w1", "ffn1_w3", "ffn2_w1", "ffn2_w3")
FFN_DOWN = ("ffn1_w2", "ffn2_w2")
ROW_SHARDED = ("w_mem_kv", "w_out")
GATHER_EARLY = ("ffn1_w1", "ffn1_w3", "ffn1_w2")
GATHER_LATE = ("w_in", "w_mem_kv", "w_out", "ffn2_w1", "ffn2_w3", "ffn2_w2")
BIG = GATHER_EARLY + GATHER_LATE
SMALL = ("ffn1_norm", "mix_norm", "conv_b", "conv_ln_g", "conv_ln_b", "swa_q_norm", "swa_k_norm", "swa_sinks",
         "mem_norm", "mem_q_norm", "mem_k_norm", "ffn2_norm", "final_norm")
WEIGHTS = ("ffn1_norm", "ffn1_w1", "ffn1_w3", "ffn1_w2", "mix_norm", "w_in", "conv_w", "conv_b", "conv_ln_g",
           "conv_ln_b", "swa_q_norm", "swa_k_norm", "swa_sinks", "mem_norm", "w_mem_kv", "mem_q_norm", "mem_k_norm",
           "w_out", "ffn2_norm", "ffn2_w1", "ffn2_w3", "ffn2_w2", "final_norm")


def _round_up(n, m):
    return -(-n // m) * m


def _row_block(rows, target, mult=8):
    best = rows
    for cand in range(mult, min(rows, target) + 1, mult):
        if rows % cand == 0:
            best = cand
    return best


def _pad_rows(flat, cols, mult):
    n = flat.shape[0]
    rows = _round_up(-(-n // cols), mult)
    return jnp.pad(flat, (0, rows * cols - n)).reshape(rows, cols)


def send_form(w):
    out = {}
    for n in BIG:
        t = w[n].astype(BF)
        if n in FFN_UP:
            t = jnp.pad(t, ((0, 0), (0, 0), (0, _round_up(t.shape[2], LANE) - t.shape[2])))
        elif n in FFN_DOWN:
            t = jnp.pad(t, ((0, 0), (0, _round_up(t.shape[1], LANE) - t.shape[1]), (0, 0)))
        out[n] = t
    return out


def gather_items(sent, l, names):
    return [gather_cols_item(sent[n][l]) if n in FFN_UP else gather_item(sent[n][l]) for n in names]


def kernel_layouts(got, names):
    out = {}
    for n, t in zip(names, got):
        if n in FFN_DOWN:
            t = t.reshape(N_DEV // 2, 2 * t.shape[1], t.shape[2])
        elif n == "w_in":
            t = t.transpose(1, 0, 2).reshape(t.shape[1], N_DEV * t.shape[2])
        elif n in ROW_SHARDED:
            t = t.reshape(N_DEV * t.shape[1], t.shape[2])
        out[n] = t
    return out


def scatter_items(gb, names):
    items = []
    for n in names:
        t = gb[n]
        if n in FFN_UP:
            items.append(scatter_cols_item(t))
            continue
        if n in FFN_DOWN:
            t = t.reshape(N_DEV, t.shape[1] // 2, t.shape[2])
        elif n == "w_in":
            t = t.reshape(t.shape[0], N_DEV, t.shape[1] // N_DEV).transpose(1, 0, 2).astype(BF)
        else:
            t = t.reshape(N_DEV, t.shape[0] // N_DEV, t.shape[1]).astype(BF)
        items.append(scatter_item(t))
    return items


def small_layer_params(w, l):
    sw = {n: w[n][l][None, :] for n in SMALL if n != "swa_sinks"}
    sw["swa_sinks"] = jnp.pad(w["swa_sinks"][l], (0, LANE - N_Q))[None, :]
    sw["conv_w"] = jnp.pad(w["conv_w_full"][l], ((0, 1), (0, 0)))
    return sw


def kernel(x, mem, positions, ffn1_norm, ffn1_w1, ffn1_w3, ffn1_w2, mix_norm, w_in, conv_w, conv_b, conv_ln_g, conv_ln_b, swa_q_norm, swa_k_norm, swa_sinks, mem_norm, w_mem_kv, mem_q_norm, mem_k_norm, w_out, ffn2_norm, ffn2_w1, ffn2_w3, ffn2_w2, final_norm, loss_target, m_ffn1_norm, m_ffn1_w1, m_ffn1_w3, m_ffn1_w2, m_mix_norm, m_w_in, m_conv_w, m_conv_b, m_conv_ln_g, m_conv_ln_b, m_swa_q_norm, m_swa_k_norm, m_swa_sinks, m_mem_norm, m_w_mem_kv, m_mem_q_norm, m_mem_k_norm, m_w_out, m_ffn2_norm, m_ffn2_w1, m_ffn2_w3, m_ffn2_w2, m_final_norm, v_ffn1_norm, v_ffn1_w1, v_ffn1_w3, v_ffn1_w2, v_mix_norm, v_w_in, v_conv_w, v_conv_b, v_conv_ln_g, v_conv_ln_b, v_swa_q_norm, v_swa_k_norm, v_swa_sinks, v_mem_norm, v_w_mem_kv, v_mem_q_norm, v_mem_k_norm, v_w_out, v_ffn2_norm, v_ffn2_w1, v_ffn2_w3, v_ffn2_w2, v_final_norm):
    loc = locals()
    w = {n: loc[n] for n in WEIGHTS}
    m = {n: loc["m_" + n] for n in WEIGHTS}
    v = {n: loc["v_" + n] for n in WEIGHTS}
    depth = ffn1_norm.shape[0]
    seq = x.shape[1]
    tm = min(512, seq)
    me = 4 * lax.axis_index("x") + 2 * lax.axis_index("y") + lax.axis_index("c")
    xs, mems, target = x[0], mem[0], loss_target[0]

    sent = send_form(w)
    cw = conv_w.shape[2]
    conv_rows = _pad_rows(conv_w.reshape(-1), LANE, 8)
    got = comm_call(gather_items(sent, 0, GATHER_EARLY) + [gather_item(conv_rows)], "gather_l0")
    bw = kernel_layouts(got[:-1], GATHER_EARLY)
    conv_full = got[-1].reshape(N_DEV, -1)[:, :conv_w.size].reshape((N_DEV,) + conv_w.shape)
    small = {n: w[n] for n in SMALL}
    small["conv_w_full"] = conv_full.transpose(1, 2, 0, 3).reshape(depth, CONV_W, N_DEV * cw)

    inv = ROPE_THETA ** (-jnp.arange(0, HEAD, 2, dtype=F32) / HEAD)
    tabs = rope_tables(positions[0].reshape(-1, 1), jnp.concatenate([inv, inv])[None, :], tm)
    saved, bws = [], []
    h = xs
    for l in range(depth):
        h, sv, bw_all, bw = layer_fwd(h, mems, tabs, bw, sent, small_layer_params(small, l), tm, l, depth)
        saved.append(sv)
        bws.append(bw_all)
    top = small_layer_params(small, depth - 1)["final_norm"]
    loss_sum, dh, g_final = loss_and_grad(saved[-1]["h3"], top, target, tm)
    loss = lax.psum(loss_sum[0, 0], ("x", "y", "c"))

    recv = [dict() for _ in range(depth)]
    gss = [None] * depth
    items = ()
    for l in reversed(range(depth)):
        below = (saved[l - 1]["h3"], small_layer_params(small, l - 1)["final_norm"]) if l else None
        dh, gb, gss[l], got, got_late, g_below = layer_bwd(dh, saved[l], mems, tabs, bws[l],
                                                           small_layer_params(small, l), tm, l, items, below)
        gss[l]["final_norm"] = g_final
        g_final = g_below
        if items:
            recv[l + 1].update(zip(GATHER_EARLY, got))
        recv[l].update(zip(GATHER_LATE, got_late))
        items = scatter_items(gb, GATHER_EARLY)
    recv[0].update(zip(GATHER_EARLY, comm_call(items, "grad_exchange_l0")))

    g_all = {}
    for n in BIG:
        a, b = w[n].shape[1:]
        blocks = [recv[l][n] for l in range(depth)]
        if n in FFN_DOWN:
            rows = _row_block(a, 256, BF16_ROWS)
        else:
            rows = _row_block(a, 256)
        g_all[n] = sum_layers(blocks, a, b, rows, "grad_sum_" + n)

    small_names = SMALL + ("conv_w",)
    sizes = {n: (w[n].shape[1] if n != "conv_w" else CONV_W * N_DEV * cw) for n in small_names}
    flat = []
    for l in range(depth):
        for n in small_names:
            t = gss[l][n]
            if n == "swa_sinks":
                t = t[:, :N_Q]
            elif n == "conv_w":
                t = t[:CONV_W]
            flat.append(t.reshape(-1))
    flat = _pad_rows(jnp.concatenate(flat), LANE, 8)
    (small_got,) = comm_call([gather_item(flat)], "small_grad_gather")
    summed = sum_layers([small_got], flat.shape[0], LANE, flat.shape[0], "small_grad_sum").reshape(-1)
    g_small = {n: [] for n in small_names}
    o = 0
    for l in range(depth):
        for n in small_names:
            g_small[n].append(summed[o:o + sizes[n]])
            o += sizes[n]
    for n in SMALL:
        g_all[n] = jnp.stack(g_small[n])
    gcw = jnp.stack(g_small["conv_w"]).reshape(depth, CONV_W, N_DEV, cw)
    g_all["conv_w"] = lax.dynamic_slice(gcw, (0, 0, me, 0), (depth, CONV_W, 1, cw)).reshape(depth, CONV_W, cw)

    outs = {"delta": {}, "new_m": {}, "new_v": {}}
    for n in BIG:
        shape = w[n].shape
        two_d = (shape[0] * shape[1], shape[2])
        res = adamw(w[n].reshape(two_d), g_all[n].reshape(two_d), m[n].reshape(two_d), v[n].reshape(two_d),
                    _row_block(two_d[0], 512), "adamw_" + n)
        for key, arr in zip(("delta", "new_m", "new_v"), res):
            outs[key][n] = arr.reshape(shape)

    def packed_small(t):
        return _pad_rows(jnp.concatenate([t[n].reshape(-1) for n in small_names]), LANE, 8)

    ps = [packed_small(t) for t in (w, g_all, m, v)]
    res = adamw(*ps, ps[0].shape[0], "adamw_small")
    for key, arr in zip(("delta", "new_m", "new_v"), res):
        fl, o = arr.reshape(-1), 0
        for n in small_names:
            outs[key][n] = fl[o:o + w[n].size].reshape(w[n].shape)
            o += w[n].size

    return (loss, dh[None], *[g_all[n] for n in WEIGHTS], *[outs["delta"][n] for n in WEIGHTS],
            *[outs["new_m"][n] for n in WEIGHTS], *[outs["new_v"][n] for n in WEIGHTS])
```

```python
import jax
import jax.numpy as jnp
from jax import lax
from jax.experimental import pallas as pl
from jax.experimental.pallas import tpu as pltpu

F32 = jnp.float32
BF = jnp.bfloat16
EPS = 1e-6
HEAD = 64
N_Q, N_KV, N_MEMH = 6, 2, 4
CONV_CH, CONV_W = 384, 31
BLOCK = 128
ROPE_THETA = 10000.0
N_DEV = 8
V7X_VMEM_LIMIT = 56 * 1024 * 1024
LANE = 128
BF16_ROWS = 16

ADAM_LR, ADAM_B1, ADAM_B2, ADAM_EPS, ADAM_WD, ADAM_STEP = 0.001, 0.9, 0.999, 1e-08, 0.01, 10

TILE, HALO, PARAM = "tile", "halo", "param"
MESH = pl.DeviceIdType.MESH
ANY = pl.BlockSpec(memory_space=pl.ANY)


def _cparams(sem=None):
    kw = dict(vmem_limit_bytes=V7X_VMEM_LIMIT)
    if sem is not None:
        kw["dimension_semantics"] = sem
    return pltpu.CompilerParams(**kw)


def _dot(a, b, dims):
    return lax.dot_general(a, b, (dims, ((), ())), preferred_element_type=F32)


_NN, _NT, _TN = ((1,), (0,)), ((1,), (1,)), ((0,), (0,))


@jax.custom_vjp
def mm(a, b):
    return _dot(a.astype(BF), b.astype(BF), _NN)


def _mm_fwd(a, b):
    return mm(a, b), (a, b)


def _mm_bwd(res, g):
    a, b = res
    gb = g.astype(BF)
    return _dot(gb, b.astype(BF), _NT), _dot(a.astype(BF), gb, _TN)


mm.defvjp(_mm_fwd, _mm_bwd)


@jax.custom_vjp
def mm_nt(a, b):
    return _dot(a.astype(BF), b.astype(BF), _NT)


def _mm_nt_fwd(a, b):
    return mm_nt(a, b), (a, b)


def _mm_nt_bwd(res, g):
    a, b = res
    gb = g.astype(BF)
    return _dot(gb, b.astype(BF), _NN), _dot(gb, a.astype(BF), _TN)


mm_nt.defvjp(_mm_nt_fwd, _mm_nt_bwd)


def rms(x, g):
    return x * lax.rsqrt(jnp.mean(x * x, axis=-1, keepdims=True) + EPS) * g


def _swap_halves(x):
    half = HEAD // 2
    return jnp.concatenate([x[:, half:], x[:, :half]], axis=1)


@jax.custom_vjp
def rope(x, c, s):
    return x * c + _swap_halves(x) * s


def _rope_fwd(x, c, s):
    return rope(x, c, s), (c, s)


def _rope_bwd(res, g):
    c, s = res
    return g * c + _swap_halves(g * s), jnp.zeros_like(c), jnp.zeros_like(s)


rope.defvjp(_rope_fwd, _rope_bwd)


SUBLANES = 8


def _row_shifts(t, rows):
    return [t] + [t[b:b + rows + 24] for b in range(1, SUBLANES)]


CONV_ROWS = 64


def _window(shifts, offset, start, rows):
    base = offset - offset % SUBLANES + start
    return shifts[offset % SUBLANES][base:base + rows]


def _row_blocks(rows):
    return [(r, min(CONV_ROWS, rows - r)) for r in range(0, rows, CONV_ROWS)]


def _conv_taps(gs, w, tm, first_offset=2, step=1):
    wr = [w[j:j + 1, :] for j in range(CONV_W)]
    blocks = []
    for r, n in _row_blocks(tm):
        acc = wr[0] * _window(gs, first_offset, r, n)
        for j in range(1, CONV_W):
            acc = acc + wr[j] * _window(gs, first_offset + step * j, r, n)
        blocks.append(acc)
    return jnp.concatenate(blocks, axis=0)


@jax.custom_vjp
def causal_dw_conv(g, w):
    tm = g.shape[0] - 32
    return _conv_taps(_row_shifts(g, tm), w, tm)


def _conv_fwd(g, w):
    tm = g.shape[0] - 32
    gs = _row_shifts(g, tm)
    return _conv_taps(gs, w, tm), (gs, w)


def _conv_bwd(res, dc):
    gs, w = res
    tm, ch = dc.shape
    z = jnp.zeros((32, ch), F32)
    ds = _row_shifts(jnp.concatenate([z, dc, z], axis=0), tm + 32)
    dg = _conv_taps(ds, w, tm + 32, first_offset=30, step=-1)
    rows = lax.broadcasted_iota(jnp.int32, (32, 1), 0)
    dcb = [dc[r:r + n] for r, n in _row_blocks(tm)]
    dw = jnp.zeros((32, ch), F32)
    for j in range(CONV_W):
        part = jnp.zeros((SUBLANES, ch), F32)
        for (r, n), d in zip(_row_blocks(tm), dcb):
            prod = d * _window(gs, 2 + j, r, n)
            for s in range(0, n, SUBLANES):
                part = part + prod[s:s + SUBLANES]
        dw = dw + jnp.where(rows == j, jnp.sum(part, axis=0, keepdims=True), 0.0)
    return dg, dw


causal_dw_conv.defvjp(_conv_fwd, _conv_bwd)


def f_proj_in(first, h, g, w):
    return (mm(rms(h, g), w),)


def _to_heads(t, n):
    return jnp.stack([t[:, h * HEAD:(h + 1) * HEAD] for h in range(n)])


def f_proj_split(first, h, g, w):
    p = mm(rms(h, g), w)
    o, outs = 0, []
    for width, n in ((CONV_CH, 0), (CONV_CH, 0), (N_Q * HEAD, N_Q), (N_KV * HEAD, N_KV), (N_KV * HEAD, N_KV),
                     (N_MEMH * HEAD, N_MEMH)):
        t = p[:, o:o + width]
        outs.append(_to_heads(t, n) if n else t)
        o += width
    return tuple(outs)


def f_out_join(first, h, y_conv, y_swa, y_mem, w):
    y = jnp.concatenate([y_conv] + [y_swa[i] for i in range(N_Q)] + [y_mem[i] for i in range(N_MEMH)], axis=1)
    return (h + mm(y, w),)


def f_conv(first, a, gate, w, b, lg, lb):
    keep = 1.0 - first
    av = jnp.concatenate([a[0] * keep, a[1]], axis=0)
    gv = jnp.concatenate([gate[0], gate[1]], axis=0)
    glu = av * jax.nn.sigmoid(gv)
    c = causal_dw_conv(glu, w) + b
    mu = jnp.mean(c, axis=-1, keepdims=True)
    var = jnp.mean(jnp.square(c - mu), axis=-1, keepdims=True)
    z = (c - mu) * lax.rsqrt(var + EPS) * lg + lb
    return (z * jax.nn.sigmoid(z),)


def _softmax_with_extra(s, extra):
    m = jnp.max(s, axis=-1, keepdims=True)
    if extra is not None:
        m = jnp.maximum(m, extra)
    m = lax.stop_gradient(m)
    e = jnp.exp(s - m)
    den = jnp.sum(e, axis=-1, keepdims=True)
    if extra is not None:
        den = den + jnp.exp(extra - m)
    return e / den


def f_swa(first, q, k, v, ct, st, qn, kn, sinks):
    tm = q.shape[1]
    nb = tm // BLOCK
    g = N_Q // N_KV
    c_all = jnp.concatenate([ct[0], ct[1]], axis=0)
    s_all = jnp.concatenate([st[0], st[1]], axis=0)
    qi = lax.broadcasted_iota(jnp.int32, (g * BLOCK, 2 * BLOCK), 0)
    kj = lax.broadcasted_iota(jnp.int32, (g * BLOCK, 2 * BLOCK), 1)
    qpos = jnp.where(qi >= 2 * BLOCK, qi - 2 * BLOCK, jnp.where(qi >= BLOCK, qi - BLOCK, qi)) + BLOCK
    rel = qpos - kj
    band = (rel >= 0) & (rel < BLOCK)
    band_first = band & ((kj >= BLOCK) | (first < 0.5))
    lane = lax.broadcasted_iota(jnp.int32, (g * BLOCK, LANE), 1)
    hrow = lax.broadcasted_iota(jnp.int32, (g * BLOCK, LANE), 0)
    head_in_group = jnp.where(hrow >= 2 * BLOCK, 2, jnp.where(hrow >= BLOCK, 1, 0))
    qr = [rope(rms(q[h], qn), ct[1], st[1]) for h in range(N_Q)]
    outs = [[None] * nb for _ in range(N_Q)]
    for hk in range(N_KV):
        kk = rope(rms(jnp.concatenate([k[0][hk], k[1][hk]], axis=0), kn), c_all, s_all)
        vv = jnp.concatenate([v[0][hk], v[1][hk]], axis=0)
        sel = (lane == head_in_group + hk * g).astype(F32)
        sink_col = jnp.sum(sel * sinks, axis=1, keepdims=True)
        for j in range(nb):
            keys = kk[j * BLOCK:(j + 2) * BLOCK]
            vals = vv[j * BLOCK:(j + 2) * BLOCK]
            qs = jnp.concatenate([qr[hk * g + gg][j * BLOCK:(j + 1) * BLOCK] for gg in range(g)], axis=0)
            s = mm_nt(qs, keys) * (HEAD ** -0.5)
            s = jnp.where(band_first if j == 0 else band, s, -1e30)
            o = mm(_softmax_with_extra(s, sink_col), vals)
            for gg in range(g):
                outs[hk * g + gg][j] = o[gg * BLOCK:(gg + 1) * BLOCK]
    return (jnp.stack([jnp.concatenate(outs[h], axis=0) for h in range(N_Q)]),)


def f_mem(first, qm, mk, mv, qn, kn):
    outs = []
    for h in range(N_MEMH):
        qh = rms(qm[h], qn)
        kh = rms(mk[h], kn)
        s = mm_nt(qh, kh) * (HEAD ** -0.5)
        outs.append(mm(_softmax_with_extra(s, None), mv[h]))
    return (jnp.stack(outs),)


def _seq_len(arr):
    return arr.shape[0] if arr.ndim == 2 else arr.shape[1]


def _tile_spec(arr, rows, imap):
    if arr.ndim == 2:
        return pl.BlockSpec((rows, arr.shape[1]), lambda i: (imap(i), 0))
    return pl.BlockSpec((arr.shape[0], rows, arr.shape[2]), lambda i: (0, imap(i), 0))


def _full_spec(arr):
    nd = arr.ndim
    return pl.BlockSpec(arr.shape, lambda i: (0,) * nd)


def _in_specs(args, tm, hl, nt):
    specs = []
    ratio = tm // hl
    cur = lambda i: jnp.minimum(i, nt - 1)
    prev = lambda i: jnp.maximum(jnp.minimum(i, nt - 1) * ratio - 1, 0)
    for arr, kind, _ in args:
        if kind == TILE:
            specs.append(_tile_spec(arr, tm, cur))
        elif kind == HALO:
            specs.append(_tile_spec(arr, hl, prev))
            specs.append(_tile_spec(arr, tm, cur))
        else:
            specs.append(_full_spec(arr))
    return specs


def _operands(args):
    ops = []
    for arr, kind, _ in args:
        ops.append(arr)
        if kind == HALO:
            ops.append(arr)
    return ops


def _load_values(args, refs):
    vals, k = [], 0
    for arr, kind, _ in args:
        if kind == HALO:
            vals.append((refs[k][...].astype(F32), refs[k + 1][...].astype(F32)))
            k += 2
        else:
            vals.append(refs[k][...].astype(F32))
            k += 1
    return vals


def seq_fwd(f, args, outs, tm, hl, name):
    out_shape = [jax.ShapeDtypeStruct(s, d) for s, d in outs]
    nt = _seq_len(out_shape[0]) // tm
    n_in = len(_operands(args))

    def body(*refs):
        first = (pl.program_id(0) == 0).astype(F32)
        res = f(first, *_load_values(args, refs[:n_in]))
        for r, o in zip(refs[n_in:], res):
            r[...] = o.astype(r.dtype)

    cur = lambda i: i
    return pl.pallas_call(
        body, grid=(nt,), in_specs=_in_specs(args, tm, hl, nt),
        out_specs=[_tile_spec(o, tm, cur) for o in out_shape], out_shape=out_shape,
        compiler_params=_cparams(("arbitrary",)), name=name,
    )(*_operands(args))


def seq_bwd(f, args, douts, tm, hl, name, add_to_first=None):
    seq = _seq_len(douts[0])
    nt = seq // tm
    lag = any(kind == HALO and diff for _, kind, diff in args)
    steps = nt + 1 if lag else nt
    n_in = len(_operands(args))
    n_do = len(douts)
    dargs = [(arr, kind) for arr, kind, diff in args if diff]
    extra = [] if add_to_first is None else [add_to_first]
    assert not extra or (dargs[0][1] == TILE and not lag)

    def body(*refs):
        in_refs = refs[:n_in]
        do_refs = refs[n_in:n_in + n_do]
        add_refs = refs[n_in + n_do:n_in + n_do + len(extra)]
        refs = refs[len(extra):]
        g_refs = refs[n_in + n_do:n_in + n_do + len(dargs)]
        carries = refs[n_in + n_do + len(dargs):]
        i = pl.program_id(0)
        first = (i == 0).astype(F32)

        def compute():
            vals = _load_values(args, in_refs)
            dvals = [v for v, (_, _, diff) in zip(vals, args) if diff]

            def fd(*dv):
                it = iter(dv)
                return f(first, *[next(it) if diff else v for v, (_, _, diff) in zip(vals, args)])

            _, vjp = jax.vjp(fd, *dvals)
            grads = vjp(tuple(r[...].astype(F32) for r in do_refs))
            c = 0
            for pos, (gref, gval, (arr, kind)) in enumerate(zip(g_refs, grads, dargs)):
                if kind == TILE:
                    if pos == 0 and extra:
                        gval = gval + add_refs[0][...]
                    gref[...] = gval.astype(gref.dtype)
                elif kind == PARAM:
                    @pl.when(i == 0)
                    def _():
                        gref[...] = gval

                    @pl.when(i > 0)
                    def _():
                        gref[...] += gval
                else:
                    carry = carries[c]
                    c += 1
                    g_prev, g_cur = gval

                    @pl.when(i > 0)
                    def _():
                        gref[...] = carry[...]
                        if arr.ndim == 2:
                            gref[tm - hl:tm, :] += g_prev
                        else:
                            gref[:, tm - hl:tm, :] += g_prev

                    carry[...] = g_cur

        if lag:
            pl.when(i < nt)(compute)

            @pl.when(i == nt)
            def _():
                c = 0
                for gref, (arr, kind) in zip(g_refs, dargs):
                    if kind == HALO:
                        gref[...] = carries[c][...]
                        c += 1
        else:
            compute()

    cur = lambda i: jnp.minimum(i, nt - 1)
    lagged = lambda i: jnp.maximum(i - 1, 0)
    out_shape, out_specs, scratch = [], [], []
    for arr, kind in dargs:
        out_shape.append(jax.ShapeDtypeStruct(arr.shape, F32))
        if kind == PARAM:
            out_specs.append(_full_spec(arr))
        else:
            out_specs.append(_tile_spec(arr, tm, lagged if kind == HALO else cur))
            if kind == HALO:
                blk = (tm, arr.shape[1]) if arr.ndim == 2 else (arr.shape[0], tm, arr.shape[2])
                scratch.append(pltpu.VMEM(blk, F32))
    in_specs = _in_specs(args, tm, hl, nt) + [_tile_spec(d, tm, cur) for d in list(douts) + extra]
    return pl.pallas_call(
        body, grid=(steps,), in_specs=in_specs, out_specs=out_specs, out_shape=out_shape,
        scratch_shapes=scratch, compiler_params=_cparams(("arbitrary",)), name=name,
    )(*_operands(args), *douts, *extra)


class Item:
    def __init__(self, operand, out_shape, src, dst, two_level=False):
        self.operand, self.out_shape, self.src, self.dst, self.two_level = operand, out_shape, src, dst, two_level


def _dev(p):
    return 4 * p[0] + 2 * p[1] + p[2]


def gather_item(shard):
    return Item(shard, jax.ShapeDtypeStruct((N_DEV,) + shard.shape, shard.dtype),
                lambda r, peer: r, lambda r, s: r.at[_dev(s)], two_level=True)


def gather_cols_item(shard):
    d, w = shard.shape
    return Item(shard, jax.ShapeDtypeStruct((N_DEV // 2, d, 2 * w), shard.dtype),
                lambda r, peer: r, lambda r, s: r.at[2 * s[0] + s[1], :, pl.ds(s[2] * w, w)], two_level=True)


def scatter_item(blocks):
    return Item(blocks, jax.ShapeDtypeStruct(blocks.shape, blocks.dtype),
                lambda r, peer: r.at[_dev(peer)], lambda r, s: r.at[_dev(s)])


def scatter_cols_item(full):
    n, d, w2 = full.shape
    w = w2 // 2
    return Item(full, jax.ShapeDtypeStruct((N_DEV, d, w), full.dtype),
                lambda r, peer: r.at[2 * peer[0] + peer[1], :, pl.ds(peer[2] * w, w)], lambda r, s: r.at[_dev(s)])


def _comm_sems(items):
    n = len(items) * (N_DEV - 1)
    return [pltpu.SemaphoreType.DMA((n,)), pltpu.SemaphoreType.DMA((n,)), pltpu.SemaphoreType.DMA((len(items),))]


PASSED_ON = (3, 5, 7)


def _comm_copies(items, in_refs, out_refs, sems, x, y, c):
    send_sems, recv_sems, local_sems = sems
    me = (x, y, c)
    sibling = (x, y, 1 - c)
    local, first, passed = [], [], []

    def remote(t, k, src, dst, to):
        n = t * (N_DEV - 1) + k - 1
        return lambda: pltpu.make_async_remote_copy(
            src_ref=src(), dst_ref=dst(), send_sem=send_sems.at[n], recv_sem=recv_sems.at[n],
            device_id=(to[0], to[1], jnp.int32(to[2])), device_id_type=MESH)

    for t, it in enumerate(items):
        local.append(lambda t=t, it=it: pltpu.make_async_copy(it.src(in_refs[t], me), it.dst(out_refs[t], me),
                                                               local_sems.at[t]))
        for k in range(1, N_DEV):
            peer = (1 - x if k & 4 else x, 1 - y if k & 2 else y, 1 - c if k & 1 else c)
            if it.two_level and k in PASSED_ON:
                origin = (peer[0], peer[1], c)
                landed = lambda t=t, it=it, origin=origin: it.dst(out_refs[t], origin)
                passed.append((t, k, remote(t, k, landed, landed, sibling)))
            else:
                first.append((t, k, remote(t, k, lambda t=t, it=it, peer=peer: it.src(in_refs[t], peer),
                                           lambda t=t, it=it: it.dst(out_refs[t], me), peer)))
    return local, first, passed


def _on_my_core(fn):
    x, y, c = lax.axis_index("x"), lax.axis_index("y"), lax.axis_index("c")
    for cv in (0, 1):
        @pl.when(c == cv)
        def _():
            fn(x, y, cv)


def comm_start(items, in_refs, out_refs, sems):
    def go(x, y, c):
        local, first, _ = _comm_copies(items, in_refs, out_refs, sems, x, y, c)
        for make in local:
            make().start()
        for _, _, make in first:
            make().start()

    _on_my_core(go)


def comm_pass_on(items, in_refs, out_refs, sems):
    def go(x, y, c):
        _, first, passed = _comm_copies(items, in_refs, out_refs, sems, x, y, c)
        arrived = {(t, k): make for t, k, make in first}
        for t, k, make in passed:
            arrived[(t, k - 1)]().wait_recv()
            make().start()

    _on_my_core(go)


def comm_wait(items, in_refs, out_refs, sems):
    def go(x, y, c):
        local, first, passed = _comm_copies(items, in_refs, out_refs, sems, x, y, c)
        waited = {(t, k - 1) for t, k, _ in passed}
        for t, k, make in first:
            cp = make()
            if (t, k) not in waited:
                cp.wait_recv()
            cp.wait_send()
        for _, _, make in passed:
            cp = make()
            cp.wait_recv()
            cp.wait_send()
        for make in local:
            make().wait()

    _on_my_core(go)


def comm_call(items, name):
    n = len(items)

    def body(*refs):
        in_refs, out_refs, sems = refs[:n], refs[n:2 * n], refs[2 * n:]
        comm_start(items, in_refs, out_refs, sems)
        comm_pass_on(items, in_refs, out_refs, sems)
        comm_wait(items, in_refs, out_refs, sems)

    return pl.pallas_call(
        body, in_specs=[ANY] * n, out_specs=[ANY] * n, out_shape=[it.out_shape for it in items],
        scratch_shapes=_comm_sems(items), name=name,
    )(*[it.operand for it in items])


def ffn_fwd(x, g, w1, w3, w2, tm, name, items=(), out_norm=None):
    seq, dm = x.shape
    nc, _, fc = w1.shape
    nt = seq // tm
    n = len(items)
    e = 0 if out_norm is None else 1

    def body(*refs):
        x_ref, g_ref, w1_ref, w3_ref, w2_ref = refs[:5]
        refs = refs[5:]
        gf_ref, refs = refs[:e], refs[e:]
        c_in, (h_ref, a_ref, b_ref, xn_s), refs = refs[:n], refs[n:n + 4], refs[n + 4:]
        xo_ref, refs = refs[:e], refs[e:]
        c_out, acc_s, sems = refs[:n], refs[n], refs[n + 1:]
        i, c = pl.program_id(0), pl.program_id(1)

        if n:
            @pl.when((i == 0) & (c == 0))
            def _():
                comm_start(items, c_in, c_out, sems)

        @pl.when(c == 0)
        def _():
            xn_s[...] = rms(x_ref[...], g_ref[...]).astype(BF)
            acc_s[...] = jnp.zeros_like(acc_s)

        xn = xn_s[...]
        a = _dot(xn, w1_ref[0], _NN)
        b = _dot(xn, w3_ref[0], _NN)
        a_ref[...] = a.astype(BF)
        b_ref[...] = b.astype(BF)
        hid = (a * jax.nn.sigmoid(a)) * b
        acc_s[...] += _dot(hid.astype(BF), w2_ref[0], _NN)

        @pl.when(c == nc - 1)
        def _():
            h = x_ref[...] + 0.5 * acc_s[...]
            h_ref[...] = h
            if e:
                xo_ref[0][...] = rms(h, gf_ref[0][...])

        if n:
            step = i * nc + c
            total = nt * nc

            @pl.when(step == max(1, (13 * total) // 16))
            def _():
                comm_pass_on(items, c_in, c_out, sems)

            @pl.when(step == total - 1)
            def _():
                comm_wait(items, c_in, c_out, sems)

    res = pl.pallas_call(
        body, grid=(nt, nc),
        in_specs=[pl.BlockSpec((tm, dm), lambda i, c: (i, 0)), pl.BlockSpec((1, dm), lambda i, c: (0, 0)),
                  pl.BlockSpec((1, dm, fc), lambda i, c: (c, 0, 0)), pl.BlockSpec((1, dm, fc), lambda i, c: (c, 0, 0)),
                  pl.BlockSpec((1, fc, dm), lambda i, c: (c, 0, 0))]
                 + [pl.BlockSpec((1, dm), lambda i, c: (0, 0))] * e + [ANY] * n,
        out_specs=[pl.BlockSpec((tm, dm), lambda i, c: (i, 0)), pl.BlockSpec((tm, fc), lambda i, c: (i, c)),
                   pl.BlockSpec((tm, fc), lambda i, c: (i, c)), pl.BlockSpec((tm, dm), lambda i, c: (i, 0))]
                  + [pl.BlockSpec((tm, dm), lambda i, c: (i, 0))] * e + [ANY] * n,
        out_shape=[jax.ShapeDtypeStruct((seq, dm), F32), jax.ShapeDtypeStruct((seq, nc * fc), BF),
                   jax.ShapeDtypeStruct((seq, nc * fc), BF), jax.ShapeDtypeStruct((seq, dm), BF)]
                  + [jax.ShapeDtypeStruct((seq, dm), F32)] * e + [it.out_shape for it in items],
        scratch_shapes=[pltpu.VMEM((tm, dm), F32)] + (_comm_sems(items) if n else []),
        compiler_params=_cparams(("arbitrary", "arbitrary")), name=name,
    )(x, g, w1, w3, w2, *([out_norm] * e), *[it.operand for it in items])
    return res[0], (res[1], res[2], res[3]), (res[4] if e else None), list(res[4 + e:])


def ffn_bwd_chunks(dh, ups, w1, w3, w2, tm, name, items=()):
    seq, dm = dh.shape
    nc, _, fc = w1.shape
    nt = seq // tm
    n = len(items)

    def body(*refs):
        dh_ref, a_ref, b_ref, xn_ref, w1_ref, w3_ref, w2_ref = refs[:7]
        c_in = refs[7:7 + n]
        dxn_ref, dw1_ref, dw3_ref, dw2_ref = refs[7 + n:11 + n]
        c_out = refs[11 + n:11 + 2 * n]
        a1_s, a3_s, a2_s = refs[11 + 2 * n:14 + 2 * n]
        sems = refs[14 + 2 * n:]
        c, i = pl.program_id(0), pl.program_id(1)

        if n:
            @pl.when((i == 0) & (c == 0))
            def _():
                comm_start(items, c_in, c_out, sems)

        @pl.when(i == 0)
        def _():
            a1_s[...] = jnp.zeros_like(a1_s)
            a3_s[...] = jnp.zeros_like(a3_s)
            a2_s[...] = jnp.zeros_like(a2_s)

        xn = xn_ref[...]
        dy = (0.5 * dh_ref[...]).astype(BF)
        w1v, w3v, w2v = w1_ref[0], w3_ref[0], w2_ref[0]
        a = a_ref[...].astype(F32)
        b = b_ref[...].astype(F32)
        sig = jax.nn.sigmoid(a)
        sa = a * sig
        dhid = _dot(dy, w2v, _NT)
        db = (dhid * sa).astype(BF)
        da = (dhid * b * (sig * (1.0 + a * (1.0 - sig)))).astype(BF)
        dxn_ref[0] = (_dot(da, w1v, _NT) + _dot(db, w3v, _NT)).astype(dxn_ref.dtype)
        a1_s[...] += _dot(xn, da, _TN)
        a3_s[...] += _dot(xn, db, _TN)
        a2_s[...] += _dot((sa * b).astype(BF), dy, _TN)

        @pl.when(i == nt - 1)
        def _():
            dw1_ref[0] = a1_s[...].astype(BF)
            dw3_ref[0] = a3_s[...].astype(BF)
            dw2_ref[0] = a2_s[...].astype(BF)

        if n:
            @pl.when((i == nt - 1) & (c == nc - 1))
            def _():
                comm_wait(items, c_in, c_out, sems)

    res = pl.pallas_call(
        body, grid=(nc, nt),
        in_specs=[pl.BlockSpec((tm, dm), lambda c, i: (i, 0)),
                  pl.BlockSpec((tm, fc), lambda c, i: (i, c)), pl.BlockSpec((tm, fc), lambda c, i: (i, c)),
                  pl.BlockSpec((tm, dm), lambda c, i: (i, 0)),
                  pl.BlockSpec((1, dm, fc), lambda c, i: (c, 0, 0)), pl.BlockSpec((1, dm, fc), lambda c, i: (c, 0, 0)),
                  pl.BlockSpec((1, fc, dm), lambda c, i: (c, 0, 0))] + [ANY] * n,
        out_specs=[pl.BlockSpec((1, tm, dm), lambda c, i: (c, i, 0)),
                   pl.BlockSpec((1, dm, fc), lambda c, i: (c, 0, 0)), pl.BlockSpec((1, dm, fc), lambda c, i: (c, 0, 0)),
                   pl.BlockSpec((1, fc, dm), lambda c, i: (c, 0, 0))] + [ANY] * n,
        out_shape=[jax.ShapeDtypeStruct((nc, seq, dm), BF), jax.ShapeDtypeStruct((nc, dm, fc), BF),
                   jax.ShapeDtypeStruct((nc, dm, fc), BF), jax.ShapeDtypeStruct((nc, fc, dm), BF)]
                  + [it.out_shape for it in items],
        scratch_shapes=[pltpu.VMEM((dm, fc), F32), pltpu.VMEM((dm, fc), F32), pltpu.VMEM((fc, dm), F32)]
                       + (_comm_sems(items) if n else []),
        compiler_params=_cparams(("arbitrary", "arbitrary")), name=name,
    )(dh, ups[0], ups[1], ups[2], w1, w3, w2, *[it.operand for it in items])
    return res[0], res[1], res[2], res[3], list(res[4:])


def _accumulate(ref, val, i):
    @pl.when(i == 0)
    def _():
        ref[...] = val

    @pl.when(i > 0)
    def _():
        ref[...] += val


def _through_norm_below(dx, below_refs):
    h3_ref, gf_ref = below_refs
    _, vjp = jax.vjp(rms, h3_ref[...], gf_ref[...])
    return vjp(dx)


def ffn_bwd_norm(x, g, parts, dh, tm, name, below=None):
    seq, dm = x.shape
    nc = parts.shape[0]
    nt = seq // tm
    nb = 0 if below is None else 2

    def body(*refs):
        x_ref, g_ref, p_ref, dh_ref = refs[:4]
        dx_ref, dg_ref = refs[4 + nb:6 + nb]
        i = pl.program_id(0)
        dxn = p_ref[0].astype(F32)
        for c in range(1, nc):
            dxn = dxn + p_ref[c].astype(F32)
        _, vjp = jax.vjp(rms, x_ref[...], g_ref[...])
        dx, dg = vjp(dxn)
        dx = dx + dh_ref[...]
        if nb:
            dx, dgf = _through_norm_below(dx, refs[4:6])
            _accumulate(refs[6 + nb], dgf, i)
        dx_ref[...] = dx
        _accumulate(dg_ref, dg, i)

    tile = pl.BlockSpec((tm, dm), lambda i: (i, 0))
    row = pl.BlockSpec((1, dm), lambda i: (0, 0))
    return pl.pallas_call(
        body, grid=(nt,),
        in_specs=[tile, row, pl.BlockSpec((nc, tm, dm), lambda i: (0, i, 0)), tile] + [tile, row][:nb],
        out_specs=[tile, row] + [row] * (nb // 2),
        out_shape=[jax.ShapeDtypeStruct((seq, dm), F32)] + [jax.ShapeDtypeStruct((1, dm), F32)] * (1 + nb // 2),
        compiler_params=_cparams(("arbitrary",)), name=name,
    )(x, g, parts, dh, *(below or ()))


def rope_tables(pos_col, inv_freq, tm):
    seq = pos_col.shape[0]

    def body(p_ref, f_ref, c_ref, s_ref):
        ang = p_ref[...].astype(F32) * f_ref[...]
        lane = lax.broadcasted_iota(jnp.int32, ang.shape, 1)
        c_ref[...] = jnp.cos(ang)
        s_ref[...] = jnp.where(lane < HEAD // 2, -jnp.sin(ang), jnp.sin(ang))

    return pl.pallas_call(
        body, grid=(seq // tm,),
        in_specs=[pl.BlockSpec((tm, 1), lambda i: (i, 0)), pl.BlockSpec((1, HEAD), lambda i: (0, 0))],
        out_specs=[pl.BlockSpec((tm, HEAD), lambda i: (i, 0))] * 2,
        out_shape=[jax.ShapeDtypeStruct((seq, HEAD), F32)] * 2,
        compiler_params=_cparams(("arbitrary",)), name="rope_tables",
    )(pos_col, inv_freq)


def loss_and_grad(h3, gf, target, tm):
    seq, dm = h3.shape

    def body(h_ref, g_ref, t_ref, l_ref, dh_ref, dg_ref):
        i = pl.program_id(0)
        y, vjp = jax.vjp(rms, h_ref[...], g_ref[...])
        err = y - t_ref[...]
        dh, dg = vjp(err * (1.0 / dm))
        dh_ref[...] = dh
        _accumulate(dg_ref, dg, i)
        part = 0.5 * jnp.sum(jnp.mean(err * err, axis=-1, keepdims=True), axis=0, keepdims=True)
        _accumulate(l_ref, jnp.broadcast_to(part, (1, LANE)), i)

    tile = pl.BlockSpec((tm, dm), lambda i: (i, 0))
    row = pl.BlockSpec((1, dm), lambda i: (0, 0))
    return pl.pallas_call(
        body, grid=(seq // tm,),
        in_specs=[tile, row, tile],
        out_specs=[pl.BlockSpec((1, LANE), lambda i: (0, 0)), tile, row],
        out_shape=[jax.ShapeDtypeStruct((1, LANE), F32), jax.ShapeDtypeStruct((seq, dm), F32),
                   jax.ShapeDtypeStruct((1, dm), F32)],
        compiler_params=_cparams(("arbitrary",)), name="loss_and_grad",
    )(h3, gf, target)


def adamw(w, g, m, v, rows, name):
    r, c = w.shape

    def body(w_ref, g_ref, m_ref, v_ref, d_ref, nm_ref, nv_ref):
        gv = g_ref[...]
        nm = ADAM_B1 * m_ref[...] + (1.0 - ADAM_B1) * gv
        nv = ADAM_B2 * v_ref[...] + (1.0 - ADAM_B2) * (gv * gv)
        m_hat = nm / (1.0 - ADAM_B1 ** ADAM_STEP)
        v_hat = nv / (1.0 - ADAM_B2 ** ADAM_STEP)
        d_ref[...] = -ADAM_LR * (m_hat / (jnp.sqrt(v_hat) + ADAM_EPS) + ADAM_WD * w_ref[...])
        nm_ref[...] = nm
        nv_ref[...] = nv

    spec = pl.BlockSpec((rows, c), lambda i: (i, 0))
    return pl.pallas_call(
        body, grid=(r // rows,), in_specs=[spec] * 4, out_specs=[spec] * 3,
        out_shape=[jax.ShapeDtypeStruct((r, c), F32)] * 3,
        compiler_params=_cparams(("arbitrary",)), name=name,
    )(w, g, m, v)


def sum_layers(recvs, out_rows, out_cols, rows, name):
    depth = len(recvs)
    n, _, c_in = recvs[0].shape

    def body(*refs):
        o_ref = refs[depth]
        for l in range(depth):
            @pl.when(pl.program_id(0) == l)
            def _():
                acc = refs[l][0].astype(F32)
                for j in range(1, n):
                    acc = acc + refs[l][j].astype(F32)
                o_ref[0] = acc[:, :out_cols]

    return pl.pallas_call(
        body, grid=(depth, out_rows // rows),
        in_specs=[pl.BlockSpec((n, rows, c_in), lambda ll, i, l=l: (0, jnp.where(ll == l, i, 0), 0))
                  for l in range(depth)],
        out_specs=pl.BlockSpec((1, rows, out_cols), lambda ll, i: (ll, i, 0)),
        out_shape=jax.ShapeDtypeStruct((depth, out_rows, out_cols), F32),
        compiler_params=_cparams(("arbitrary", "arbitrary")), name=name,
    )(*recvs)


def _heads(t, n):
    return t.reshape(t.shape[0], n, HEAD).transpose(1, 0, 2)


def _unheads(t):
    return t.transpose(1, 0, 2).reshape(t.shape[1], t.shape[0] * HEAD)


def _mkv_heads(mkv):
    mw = N_MEMH * HEAD
    return _heads(mkv[:, :mw], N_MEMH), _heads(mkv[:, mw:], N_MEMH)


def _mix_args(sv, tabs, sw):
    mk, mv = _mkv_heads(sv["mkv"])
    ct, st = tabs
    conv_args = [(sv["a"], HALO, True), (sv["gate"], HALO, True), (sw["conv_w"], PARAM, True),
                 (sw["conv_b"], PARAM, True), (sw["conv_ln_g"], PARAM, True), (sw["conv_ln_b"], PARAM, True)]
    swa_args = [(sv["q"], TILE, True), (sv["k"], HALO, True), (sv["v"], HALO, True), (ct, HALO, False),
                (st, HALO, False), (sw["swa_q_norm"], PARAM, True), (sw["swa_k_norm"], PARAM, True),
                (sw["swa_sinks"], PARAM, True)]
    mem_args = [(sv["qm"], TILE, True), (mk, PARAM, True), (mv, PARAM, True), (sw["mem_q_norm"], PARAM, True),
                (sw["mem_k_norm"], PARAM, True)]
    return conv_args, swa_args, mem_args


def _proj_args(h1, sw, bw):
    return [(h1, TILE, True), (sw["mix_norm"], PARAM, True), (bw["w_in"], PARAM, True)]


def _join_args(sv, bw):
    return [(sv["h1"], TILE, True), (sv["y_conv"], TILE, True), (sv["y_swa"], TILE, True), (sv["y_mem"], TILE, True),
            (bw["w_out"], PARAM, True)]


def layer_fwd(x, mem, tabs, bw, sent, sw, tm, l, depth):
    seq, dm = x.shape
    tag = f"_l{l}"
    sv = dict(x=x)
    sv["h1"], sv["ups1"], _, got1 = ffn_fwd(x, sw["ffn1_norm"], bw["ffn1_w1"], bw["ffn1_w3"], bw["ffn1_w2"], tm,
                                            "ffn1_fwd" + tag, gather_items(sent, l, GATHER_LATE))
    bw = {**bw, **kernel_layouts(got1, GATHER_LATE)}
    items2 = gather_items(sent, l + 1, GATHER_EARLY) if l + 1 < depth else ()
    split_outs = [((seq, CONV_CH), F32), ((seq, CONV_CH), F32), ((N_Q, seq, HEAD), F32), ((N_KV, seq, HEAD), F32),
                  ((N_KV, seq, HEAD), F32), ((N_MEMH, seq, HEAD), F32)]
    sv["a"], sv["gate"], sv["q"], sv["k"], sv["v"], sv["qm"] = seq_fwd(
        f_proj_split, _proj_args(sv["h1"], sw, bw), split_outs, tm, tm, "proj_in_fwd" + tag)
    ml = mem.shape[0]
    (sv["mkv"],) = seq_fwd(f_proj_in, [(mem, TILE, False), (sw["mem_norm"], PARAM, True), (bw["w_mem_kv"], PARAM, True)],
                           [((ml, bw["w_mem_kv"].shape[1]), F32)], ml, ml, "mem_kv_fwd" + tag)
    conv_args, swa_args, mem_args = _mix_args(sv, tabs, sw)
    (sv["y_conv"],) = seq_fwd(f_conv, conv_args, [((seq, CONV_CH), F32)], tm, 32, "conv_fwd" + tag)
    (sv["y_swa"],) = seq_fwd(f_swa, swa_args, [((N_Q, seq, HEAD), F32)], tm, BLOCK, "swa_fwd" + tag)
    (sv["y_mem"],) = seq_fwd(f_mem, mem_args, [((N_MEMH, seq, HEAD), F32)], tm, tm, "mem_attn_fwd" + tag)
    (sv["h2"],) = seq_fwd(f_out_join, _join_args(sv, bw), [((seq, dm), F32)], tm, tm, "out_proj_fwd" + tag)
    sv["h3"], sv["ups2"], xo, got2 = ffn_fwd(sv["h2"], sw["ffn2_norm"], bw["ffn2_w1"], bw["ffn2_w3"], bw["ffn2_w2"],
                                             tm, "ffn2_fwd" + tag, items2,
                                             out_norm=sw["final_norm"] if l + 1 < depth else None)
    return xo, sv, bw, kernel_layouts(got2, GATHER_EARLY)


def layer_bwd(dh3, sv, mem, tabs, bw, sw, tm, l, items=(), below=None):
    tag = f"_l{l}"
    gb, gs = {}, {}
    parts2, gb["ffn2_w1"], gb["ffn2_w3"], gb["ffn2_w2"], got = ffn_bwd_chunks(
        dh3, sv["ups2"], bw["ffn2_w1"], bw["ffn2_w3"], bw["ffn2_w2"], tm, "ffn2_bwd" + tag, items)
    dh2, gs["ffn2_norm"] = ffn_bwd_norm(sv["h2"], sw["ffn2_norm"], parts2, dh3, tm, "ffn2_norm_bwd" + tag)
    dh1_a, dy_conv, dy_swa, dy_mem, gb["w_out"] = seq_bwd(f_out_join, _join_args(sv, bw), [dh2], tm, tm,
                                                           "out_proj_bwd" + tag)
    conv_args, swa_args, mem_args = _mix_args(sv, tabs, sw)
    da, dgate, gs["conv_w"], gs["conv_b"], gs["conv_ln_g"], gs["conv_ln_b"] = seq_bwd(
        f_conv, conv_args, [dy_conv], tm, 32, "conv_bwd" + tag)
    dq, dk, dv, gs["swa_q_norm"], gs["swa_k_norm"], gs["swa_sinks"] = seq_bwd(
        f_swa, swa_args, [dy_swa], tm, BLOCK, "swa_bwd" + tag)
    dqm, dmk, dmv, gs["mem_q_norm"], gs["mem_k_norm"] = seq_bwd(f_mem, mem_args, [dy_mem], tm, tm, "mem_attn_bwd" + tag)
    dmkv = jnp.concatenate([_unheads(dmk), _unheads(dmv)], axis=-1)
    ml = mem.shape[0]
    gs["mem_norm"], gb["w_mem_kv"] = seq_bwd(
        f_proj_in, [(mem, TILE, False), (sw["mem_norm"], PARAM, True), (bw["w_mem_kv"], PARAM, True)], [dmkv], ml, ml,
        "mem_kv_bwd" + tag)
    dh1, gs["mix_norm"], gb["w_in"] = seq_bwd(f_proj_split, _proj_args(sv["h1"], sw, bw),
                                               [da, dgate, dq, dk, dv, dqm], tm // 2, tm // 2, "proj_in_bwd" + tag,
                                               add_to_first=dh1_a)
    parts1, gb["ffn1_w1"], gb["ffn1_w3"], gb["ffn1_w2"], got_late = ffn_bwd_chunks(
        dh1, sv["ups1"], bw["ffn1_w1"], bw["ffn1_w3"], bw["ffn1_w2"], tm, "ffn1_bwd" + tag,
        scatter_items(gb, GATHER_LATE))
    dx, gs["ffn1_norm"], *g_below = ffn_bwd_norm(sv["x"], sw["ffn1_norm"], parts1, dh1, tm, "ffn1_norm_bwd" + tag,
                                                 below)
    return dx, gb, gs, got, got_late, (g_below[0] if g_below else None)


FFN_UP = ("ffn1_w1", "ffn1_w3", "ffn2_w1", "ffn2_w3")
FFN_DOWN = ("ffn1_w2", "ffn2_w2")
ROW_SHARDED = ("w_mem_kv", "w_out")
GATHER_EARLY = ("ffn1_w1", "ffn1_w3", "ffn1_w2")
GATHER_LATE = ("w_in", "w_mem_kv", "w_out", "ffn2_w1", "ffn2_w3", "ffn2_w2")
BIG = GATHER_EARLY + GATHER_LATE
SMALL = ("ffn1_norm", "mix_norm", "conv_b", "conv_ln_g", "conv_ln_b", "swa_q_norm", "swa_k_norm", "swa_sinks",
         "mem_norm", "mem_q_norm", "mem_k_norm", "ffn2_norm", "final_norm")
WEIGHTS = ("ffn1_norm", "ffn1_w1", "ffn1_w3", "ffn1_w2", "mix_norm", "w_in", "conv_w", "conv_b", "conv_ln_g",
           "conv_ln_b", "swa_q_norm", "swa_k_norm", "swa_sinks", "mem_norm", "w_mem_kv", "mem_q_norm", "mem_k_norm",
           "w_out", "ffn2_norm", "ffn2_w1", "ffn2_w3", "ffn2_w2", "final_norm")


def _round_up(n, m):
    return -(-n // m) * m


def _row_block(rows, target, mult=8):
    best = rows
    for cand in range(mult, min(rows, target) + 1, mult):
        if rows % cand == 0:
            best = cand
    return best


def _pad_rows(flat, cols, mult):
    n = flat.shape[0]
    rows = _round_up(-(-n // cols), mult)
    return jnp.pad(flat, (0, rows * cols - n)).reshape(rows, cols)


def send_form(w):
    out = {}
    for n in BIG:
        t = w[n].astype(BF)
        if n in FFN_UP:
            t = jnp.pad(t, ((0, 0), (0, 0), (0, _round_up(t.shape[2], LANE) - t.shape[2])))
        elif n in FFN_DOWN:
            t = jnp.pad(t, ((0, 0), (0, _round_up(t.shape[1], LANE) - t.shape[1]), (0, 0)))
        out[n] = t
    return out


def gather_items(sent, l, names):
    return [gather_cols_item(sent[n][l]) if n in FFN_UP else gather_item(sent[n][l]) for n in names]


def kernel_layouts(got, names):
    out = {}
    for n, t in zip(names, got):
        if n in FFN_DOWN:
            t = t.reshape(N_DEV // 2, 2 * t.shape[1], t.shape[2])
        elif n == "w_in":
            t = t.transpose(1, 0, 2).reshape(t.shape[1], N_DEV * t.shape[2])
        elif n in ROW_SHARDED:
            t = t.reshape(N_DEV * t.shape[1], t.shape[2])
        out[n] = t
    return out


def scatter_items(gb, names):
    items = []
    for n in names:
        t = gb[n]
        if n in FFN_UP:
            items.append(scatter_cols_item(t))
            continue
        if n in FFN_DOWN:
            t = t.reshape(N_DEV, t.shape[1] // 2, t.shape[2])
        elif n == "w_in":
            t = t.reshape(t.shape[0], N_DEV, t.shape[1] // N_DEV).transpose(1, 0, 2).astype(BF)
        else:
            t = t.reshape(N_DEV, t.shape[0] // N_DEV, t.shape[1]).astype(BF)
        items.append(scatter_item(t))
    return items


def small_layer_params(w, l):
    sw = {n: w[n][l][None, :] for n in SMALL if n != "swa_sinks"}
    sw["swa_sinks"] = jnp.pad(w["swa_sinks"][l], (0, LANE - N_Q))[None, :]
    sw["conv_w"] = jnp.pad(w["conv_w_full"][l], ((0, 1), (0, 0)))
    return sw


def kernel(x, mem, positions, ffn1_norm, ffn1_w1, ffn1_w3, ffn1_w2, mix_norm, w_in, conv_w, conv_b, conv_ln_g, conv_ln_b, swa_q_norm, swa_k_norm, swa_sinks, mem_norm, w_mem_kv, mem_q_norm, mem_k_norm, w_out, ffn2_norm, ffn2_w1, ffn2_w3, ffn2_w2, final_norm, loss_target, m_ffn1_norm, m_ffn1_w1, m_ffn1_w3, m_ffn1_w2, m_mix_norm, m_w_in, m_conv_w, m_conv_b, m_conv_ln_g, m_conv_ln_b, m_swa_q_norm, m_swa_k_norm, m_swa_sinks, m_mem_norm, m_w_mem_kv, m_mem_q_norm, m_mem_k_norm, m_w_out, m_ffn2_norm, m_ffn2_w1, m_ffn2_w3, m_ffn2_w2, m_final_norm, v_ffn1_norm, v_ffn1_w1, v_ffn1_w3, v_ffn1_w2, v_mix_norm, v_w_in, v_conv_w, v_conv_b, v_conv_ln_g, v_conv_ln_b, v_swa_q_norm, v_swa_k_norm, v_swa_sinks, v_mem_norm, v_w_mem_kv, v_mem_q_norm, v_mem_k_norm, v_w_out, v_ffn2_norm, v_ffn2_w1, v_ffn2_w3, v_ffn2_w2, v_final_norm):
    loc = locals()
    w = {n: loc[n] for n in WEIGHTS}
    m = {n: loc["m_" + n] for n in WEIGHTS}
    v = {n: loc["v_" + n] for n in WEIGHTS}
    depth = ffn1_norm.shape[0]
    seq = x.shape[1]
    tm = min(512, seq)
    me = 4 * lax.axis_index("x") + 2 * lax.axis_index("y") + lax.axis_index("c")
    xs, mems, target = x[0], mem[0], loss_target[0]

    sent = send_form(w)
    cw = conv_w.shape[2]
    conv_rows = _pad_rows(conv_w.reshape(-1), LANE, 8)
    got = comm_call(gather_items(sent, 0, GATHER_EARLY) + [gather_item(conv_rows)], "gather_l0")
    bw = kernel_layouts(got[:-1], GATHER_EARLY)
    conv_full = got[-1].reshape(N_DEV, -1)[:, :conv_w.size].reshape((N_DEV,) + conv_w.shape)
    small = {n: w[n] for n in SMALL}
    small["conv_w_full"] = conv_full.transpose(1, 2, 0, 3).reshape(depth, CONV_W, N_DEV * cw)

    inv = ROPE_THETA ** (-jnp.arange(0, HEAD, 2, dtype=F32) / HEAD)
    tabs = rope_tables(positions[0].reshape(-1, 1), jnp.concatenate([inv, inv])[None, :], tm)
    saved, bws = [], []
    h = xs
    for l in range(depth):
        h, sv, bw_all, bw = layer_fwd(h, mems, tabs, bw, sent, small_layer_params(small, l), tm, l, depth)
        saved.append(sv)
        bws.append(bw_all)
    top = small_layer_params(small, depth - 1)["final_norm"]
    loss_sum, dh, g_final = loss_and_grad(saved[-1]["h3"], top, target, tm)
    loss = lax.psum(loss_sum[0, 0], ("x", "y", "c"))

    recv = [dict() for _ in range(depth)]
    gss = [None] * depth
    items = ()
    for l in reversed(range(depth)):
        below = (saved[l - 1]["h3"], small_layer_params(small, l - 1)["final_norm"]) if l else None
        dh, gb, gss[l], got, got_late, g_below = layer_bwd(dh, saved[l], mems, tabs, bws[l],
                                                           small_layer_params(small, l), tm, l, items, below)
        gss[l]["final_norm"] = g_final
        g_final = g_below
        if items:
            recv[l + 1].update(zip(GATHER_EARLY, got))
        recv[l].update(zip(GATHER_LATE, got_late))
        items = scatter_items(gb, GATHER_EARLY)
    recv[0].update(zip(GATHER_EARLY, comm_call(items, "grad_exchange_l0")))

    g_all = {}
    for n in BIG:
        a, b = w[n].shape[1:]
        blocks = [recv[l][n] for l in range(depth)]
        if n in FFN_DOWN:
            rows = _row_block(a, 256, BF16_ROWS)
        else:
            rows = _row_block(a, 256)
        g_all[n] = sum_layers(blocks, a, b, rows, "grad_sum_" + n)

    small_names = SMALL + ("conv_w",)
    sizes = {n: (w[n].shape[1] if n != "conv_w" else CONV_W * N_DEV * cw) for n in small_names}
    flat = []
    for l in range(depth):
        for n in small_names:
            t = gss[l][n]
            if n == "swa_sinks":
                t = t[:, :N_Q]
            elif n == "conv_w":
                t = t[:CONV_W]
            flat.append(t.reshape(-1))
    flat = _pad_rows(jnp.concatenate(flat), LANE, 8)
    (small_got,) = comm_call([gather_item(flat)], "small_grad_gather")
    summed = sum_layers([small_got], flat.shape[0], LANE, flat.shape[0], "small_grad_sum").reshape(-1)
    g_small = {n: [] for n in small_names}
    o = 0
    for l in range(depth):
        for n in small_names:
            g_small[n].append(summed[o:o + sizes[n]])
            o += sizes[n]
    for n in SMALL:
        g_all[n] = jnp.stack(g_small[n])
    gcw = jnp.stack(g_small["conv_w"]).reshape(depth, CONV_W, N_DEV, cw)
    g_all["conv_w"] = lax.dynamic_slice(gcw, (0, 0, me, 0), (depth, CONV_W, 1, cw)).reshape(depth, CONV_W, cw)

    outs = {"delta": {}, "new_m": {}, "new_v": {}}
    for n in BIG:
        shape = w[n].shape
        two_d = (shape[0] * shape[1], shape[2])
        res = adamw(w[n].reshape(two_d), g_all[n].reshape(two_d), m[n].reshape(two_d), v[n].reshape(two_d),
                    _row_block(two_d[0], 512), "adamw_" + n)
        for key, arr in zip(("delta", "new_m", "new_v"), res):
            outs[key][n] = arr.reshape(shape)

    def packed_small(t):
        return _pad_rows(jnp.concatenate([t[n].reshape(-1) for n in small_names]), LANE, 8)

    ps = [packed_small(t) for t in (w, g_all, m, v)]
    res = adamw(*ps, ps[0].shape[0], "adamw_small")
    for key, arr in zip(("delta", "new_m", "new_v"), res):
        fl, o = arr.reshape(-1), 0
        for n in small_names:
            outs[key][n] = fl[o:o + w[n].size].reshape(w[n].shape)
            o += w[n].size

    return (loss, dh[None], *[g_all[n] for n in WEIGHTS], *[outs["delta"][n] for n in WEIGHTS],
            *[outs["new_m"][n] for n in WEIGHTS], *[outs["new_v"][n] for n in WEIGHTS])
```

```python
import jax
import jax.numpy as jnp
from jax import lax
from jax.experimental import pallas as pl
from jax.experimental.pallas import tpu as pltpu

F32 = jnp.float32
BF = jnp.bfloat16
EPS = 1e-6
HEAD = 64
N_Q, N_KV, N_MEMH = 6, 2, 4
CONV_CH, CONV_W = 384, 31
BLOCK = 128
ROPE_THETA = 10000.0
N_DEV = 8
V7X_VMEM_LIMIT = 56 * 1024 * 1024
LANE = 128
BF16_ROWS = 16

ADAM_LR, ADAM_B1, ADAM_B2, ADAM_EPS, ADAM_WD, ADAM_STEP = 0.001, 0.9, 0.999, 1e-08, 0.01, 10

TILE, HALO, PARAM = "tile", "halo", "param"
MESH = pl.DeviceIdType.MESH
ANY = pl.BlockSpec(memory_space=pl.ANY)


def _cparams(sem=None):
    kw = dict(vmem_limit_bytes=V7X_VMEM_LIMIT)
    if sem is not None:
        kw["dimension_semantics"] = sem
    return pltpu.CompilerParams(**kw)


def _dot(a, b, dims):
    return lax.dot_general(a, b, (dims, ((), ())), preferred_element_type=F32)


_NN, _NT, _TN = ((1,), (0,)), ((1,), (1,)), ((0,), (0,))


@jax.custom_vjp
def mm(a, b):
    return _dot(a.astype(BF), b.astype(BF), _NN)


def _mm_fwd(a, b):
    return mm(a, b), (a, b)


def _mm_bwd(res, g):
    a, b = res
    gb = g.astype(BF)
    return _dot(gb, b.astype(BF), _NT), _dot(a.astype(BF), gb, _TN)


mm.defvjp(_mm_fwd, _mm_bwd)


@jax.custom_vjp
def mm_nt(a, b):
    return _dot(a.astype(BF), b.astype(BF), _NT)


def _mm_nt_fwd(a, b):
    return mm_nt(a, b), (a, b)


def _mm_nt_bwd(res, g):
    a, b = res
    gb = g.astype(BF)
    return _dot(gb, b.astype(BF), _NN), _dot(gb, a.astype(BF), _TN)


mm_nt.defvjp(_mm_nt_fwd, _mm_nt_bwd)


def rms(x, g):
    return x * lax.rsqrt(jnp.mean(x * x, axis=-1, keepdims=True) + EPS) * g


def _swap_halves(x):
    half = HEAD // 2
    return jnp.concatenate([x[:, half:], x[:, :half]], axis=1)


@jax.custom_vjp
def rope(x, c, s):
    return x * c + _swap_halves(x) * s


def _rope_fwd(x, c, s):
    return rope(x, c, s), (c, s)


def _rope_bwd(res, g):
    c, s = res
    return g * c + _swap_halves(g * s), jnp.zeros_like(c), jnp.zeros_like(s)


rope.defvjp(_rope_fwd, _rope_bwd)


SUBLANES = 8


def _row_shifts(t, rows):
    return [t] + [t[b:b + rows + 24] for b in range(1, SUBLANES)]


CONV_ROWS = 64


def _window(shifts, offset, start, rows):
    base = offset - offset % SUBLANES + start
    return shifts[offset % SUBLANES][base:base + rows]


def _row_blocks(rows):
    return [(r, min(CONV_ROWS, rows - r)) for r in range(0, rows, CONV_ROWS)]


def _conv_taps(gs, w, tm, first_offset=2, step=1):
    wr = [w[j:j + 1, :] for j in range(CONV_W)]
    blocks = []
    for r, n in _row_blocks(tm):
        acc = wr[0] * _window(gs, first_offset, r, n)
        for j in range(1, CONV_W):
            acc = acc + wr[j] * _window(gs, first_offset + step * j, r, n)
        blocks.append(acc)
    return jnp.concatenate(blocks, axis=0)


@jax.custom_vjp
def causal_dw_conv(g, w):
    tm = g.shape[0] - 32
    return _conv_taps(_row_shifts(g, tm), w, tm)


def _conv_fwd(g, w):
    tm = g.shape[0] - 32
    gs = _row_shifts(g, tm)
    return _conv_taps(gs, w, tm), (gs, w)


def _conv_bwd(res, dc):
    gs, w = res
    tm, ch = dc.shape
    z = jnp.zeros((32, ch), F32)
    ds = _row_shifts(jnp.concatenate([z, dc, z], axis=0), tm + 32)
    dg = _conv_taps(ds, w, tm + 32, first_offset=30, step=-1)
    rows = lax.broadcasted_iota(jnp.int32, (32, 1), 0)
    dcb = [dc[r:r + n] for r, n in _row_blocks(tm)]
    dw = jnp.zeros((32, ch), F32)
    for j in range(CONV_W):
        part = jnp.zeros((SUBLANES, ch), F32)
        for (r, n), d in zip(_row_blocks(tm), dcb):
            prod = d * _window(gs, 2 + j, r, n)
            for s in range(0, n, SUBLANES):
                part = part + prod[s:s + SUBLANES]
        dw = dw + jnp.where(rows == j, jnp.sum(part, axis=0, keepdims=True), 0.0)
    return dg, dw


causal_dw_conv.defvjp(_conv_fwd, _conv_bwd)


def f_proj_in(first, h, g, w):
    return (mm(rms(h, g), w),)


def _to_heads(t, n):
    return jnp.stack([t[:, h * HEAD:(h + 1) * HEAD] for h in range(n)])


def f_proj_split(first, h, g, w):
    p = mm(rms(h, g), w)
    o, outs = 0, []
    for width, n in ((CONV_CH, 0), (CONV_CH, 0), (N_Q * HEAD, N_Q), (N_KV * HEAD, N_KV), (N_KV * HEAD, N_KV),
                     (N_MEMH * HEAD, N_MEMH)):
        t = p[:, o:o + width]
        outs.append(_to_heads(t, n) if n else t)
        o += width
    return tuple(outs)


def f_out_join(first, h, y_conv, y_swa, y_mem, w):
    y = jnp.concatenate([y_conv] + [y_swa[i] for i in range(N_Q)] + [y_mem[i] for i in range(N_MEMH)], axis=1)
    return (h + mm(y, w),)


def f_conv(first, a, gate, w, b, lg, lb):
    keep = 1.0 - first
    av = jnp.concatenate([a[0] * keep, a[1]], axis=0)
    gv = jnp.concatenate([gate[0], gate[1]], axis=0)
    glu = av * jax.nn.sigmoid(gv)
    c = causal_dw_conv(glu, w) + b
    mu = jnp.mean(c, axis=-1, keepdims=True)
    var = jnp.mean(jnp.square(c - mu), axis=-1, keepdims=True)
    z = (c - mu) * lax.rsqrt(var + EPS) * lg + lb
    return (z * jax.nn.sigmoid(z),)


def _softmax_with_extra(s, extra):
    m = jnp.max(s, axis=-1, keepdims=True)
    if extra is not None:
        m = jnp.maximum(m, extra)
    m = lax.stop_gradient(m)
    e = jnp.exp(s - m)
    den = jnp.sum(e, axis=-1, keepdims=True)
    if extra is not None:
        den = den + jnp.exp(extra - m)
    return e / den


def f_swa(first, q, k, v, ct, st, qn, kn, sinks):
    tm = q.shape[1]
    nb = tm // BLOCK
    g = N_Q // N_KV
    c_all = jnp.concatenate([ct[0], ct[1]], axis=0)
    s_all = jnp.concatenate([st[0], st[1]], axis=0)
    qi = lax.broadcasted_iota(jnp.int32, (g * BLOCK, 2 * BLOCK), 0)
    kj = lax.broadcasted_iota(jnp.int32, (g * BLOCK, 2 * BLOCK), 1)
    qpos = jnp.where(qi >= 2 * BLOCK, qi - 2 * BLOCK, jnp.where(qi >= BLOCK, qi - BLOCK, qi)) + BLOCK
    rel = qpos - kj
    band = (rel >= 0) & (rel < BLOCK)
    band_first = band & ((kj >= BLOCK) | (first < 0.5))
    lane = lax.broadcasted_iota(jnp.int32, (g * BLOCK, LANE), 1)
    hrow = lax.broadcasted_iota(jnp.int32, (g * BLOCK, LANE), 0)
    head_in_group = jnp.where(hrow >= 2 * BLOCK, 2, jnp.where(hrow >= BLOCK, 1, 0))
    qr = [rope(rms(q[h], qn), ct[1], st[1]) for h in range(N_Q)]
    outs = [[None] * nb for _ in range(N_Q)]
    for hk in range(N_KV):
        kk = rope(rms(jnp.concatenate([k[0][hk], k[1][hk]], axis=0), kn), c_all, s_all)
        vv = jnp.concatenate([v[0][hk], v[1][hk]], axis=0)
        sel = (lane == head_in_group + hk * g).astype(F32)
        sink_col = jnp.sum(sel * sinks, axis=1, keepdims=True)
        for j in range(nb):
            keys = kk[j * BLOCK:(j + 2) * BLOCK]
            vals = vv[j * BLOCK:(j + 2) * BLOCK]
            qs = jnp.concatenate([qr[hk * g + gg][j * BLOCK:(j + 1) * BLOCK] for gg in range(g)], axis=0)
            s = mm_nt(qs, keys) * (HEAD ** -0.5)
            s = jnp.where(band_first if j == 0 else band, s, -1e30)
            o = mm(_softmax_with_extra(s, sink_col), vals)
            for gg in range(g):
                outs[hk * g + gg][j] = o[gg * BLOCK:(gg + 1) * BLOCK]
    return (jnp.stack([jnp.concatenate(outs[h], axis=0) for h in range(N_Q)]),)


def f_mem(first, qm, mk, mv, qn, kn):
    outs = []
    for h in range(N_MEMH):
        qh = rms(qm[h], qn)
        kh = rms(mk[h], kn)
        s = mm_nt(qh, kh) * (HEAD ** -0.5)
        outs.append(mm(_softmax_with_extra(s, None), mv[h]))
    return (jnp.stack(outs),)


def _seq_len(arr):
    return arr.shape[0] if arr.ndim == 2 else arr.shape[1]


def _tile_spec(arr, rows, imap):
    if arr.ndim == 2:
        return pl.BlockSpec((rows, arr.shape[1]), lambda i: (imap(i), 0))
    return pl.BlockSpec((arr.shape[0], rows, arr.shape[2]), lambda i: (0, imap(i), 0))


def _full_spec(arr):
    nd = arr.ndim
    return pl.BlockSpec(arr.shape, lambda i: (0,) * nd)


def _in_specs(args, tm, hl, nt):
    specs = []
    ratio = tm // hl
    cur = lambda i: jnp.minimum(i, nt - 1)
    prev = lambda i: jnp.maximum(jnp.minimum(i, nt - 1) * ratio - 1, 0)
    for arr, kind, _ in args:
        if kind == TILE:
            specs.append(_tile_spec(arr, tm, cur))
        elif kind == HALO:
            specs.append(_tile_spec(arr, hl, prev))
            specs.append(_tile_spec(arr, tm, cur))
        else:
            specs.append(_full_spec(arr))
    return specs


def _operands(args):
    ops = []
    for arr, kind, _ in args:
        ops.append(arr)
        if kind == HALO:
            ops.append(arr)
    return ops


def _load_values(args, refs):
    vals, k = [], 0
    for arr, kind, _ in args:
        if kind == HALO:
            vals.append((refs[k][...].astype(F32), refs[k + 1][...].astype(F32)))
            k += 2
        else:
            vals.append(refs[k][...].astype(F32))
            k += 1
    return vals


def seq_fwd(f, args, outs, tm, hl, name):
    out_shape = [jax.ShapeDtypeStruct(s, d) for s, d in outs]
    nt = _seq_len(out_shape[0]) // tm
    n_in = len(_operands(args))

    def body(*refs):
        first = (pl.program_id(0) == 0).astype(F32)
        res = f(first, *_load_values(args, refs[:n_in]))
        for r, o in zip(refs[n_in:], res):
            r[...] = o.astype(r.dtype)

    cur = lambda i: i
    return pl.pallas_call(
        body, grid=(nt,), in_specs=_in_specs(args, tm, hl, nt),
        out_specs=[_tile_spec(o, tm, cur) for o in out_shape], out_shape=out_shape,
        compiler_params=_cparams(("arbitrary",)), name=name,
    )(*_operands(args))


def seq_bwd(f, args, douts, tm, hl, name, add_to_first=None):
    seq = _seq_len(douts[0])
    nt = seq // tm
    lag = any(kind == HALO and diff for _, kind, diff in args)
    steps = nt + 1 if lag else nt
    n_in = len(_operands(args))
    n_do = len(douts)
    dargs = [(arr, kind) for arr, kind, diff in args if diff]
    extra = [] if add_to_first is None else [add_to_first]
    assert not extra or (dargs[0][1] == TILE and not lag)

    def body(*refs):
        in_refs = refs[:n_in]
        do_refs = refs[n_in:n_in + n_do]
        add_refs = refs[n_in + n_do:n_in + n_do + len(extra)]
        refs = refs[len(extra):]
        g_refs = refs[n_in + n_do:n_in + n_do + len(dargs)]
        carries = refs[n_in + n_do + len(dargs):]
        i = pl.program_id(0)
        first = (i == 0).astype(F32)

        def compute():
            vals = _load_values(args, in_refs)
            dvals = [v for v, (_, _, diff) in zip(vals, args) if diff]

            def fd(*dv):
                it = iter(dv)
                return f(first, *[next(it) if diff else v for v, (_, _, diff) in zip(vals, args)])

            _, vjp = jax.vjp(fd, *dvals)
            grads = vjp(tuple(r[...].astype(F32) for r in do_refs))
            c = 0
            for pos, (gref, gval, (arr, kind)) in enumerate(zip(g_refs, grads, dargs)):
                if kind == TILE:
                    if pos == 0 and extra:
                        gval = gval + add_refs[0][...]
                    gref[...] = gval.astype(gref.dtype)
                elif kind == PARAM:
                    @pl.when(i == 0)
                    def _():
                        gref[...] = gval

                    @pl.when(i > 0)
                    def _():
                        gref[...] += gval
                else:
                    carry = carries[c]
                    c += 1
                    g_prev, g_cur = gval

                    @pl.when(i > 0)
                    def _():
                        gref[...] = carry[...]
                        if arr.ndim == 2:
                            gref[tm - hl:tm, :] += g_prev
                        else:
                            gref[:, tm - hl:tm, :] += g_prev

                    carry[...] = g_cur

        if lag:
            pl.when(i < nt)(compute)

            @pl.when(i == nt)
            def _():
                c = 0
                for gref, (arr, kind) in zip(g_refs, dargs):
                    if kind == HALO:
                        gref[...] = carries[c][...]
                        c += 1
        else:
            compute()

    cur = lambda i: jnp.minimum(i, nt - 1)
    lagged = lambda i: jnp.maximum(i - 1, 0)
    out_shape, out_specs, scratch = [], [], []
    for arr, kind in dargs:
        out_shape.append(jax.ShapeDtypeStruct(arr.shape, F32))
        if kind == PARAM:
            out_specs.append(_full_spec(arr))
        else:
            out_specs.append(_tile_spec(arr, tm, lagged if kind == HALO else cur))
            if kind == HALO:
                blk = (tm, arr.shape[1]) if arr.ndim == 2 else (arr.shape[0], tm, arr.shape[2])
                scratch.append(pltpu.VMEM(blk, F32))
    in_specs = _in_specs(args, tm, hl, nt) + [_tile_spec(d, tm, cur) for d in list(douts) + extra]
    return pl.pallas_call(
        body, grid=(steps,), in_specs=in_specs, out_specs=out_specs, out_shape=out_shape,
        scratch_shapes=scratch, compiler_params=_cparams(("arbitrary",)), name=name,
    )(*_operands(args), *douts, *extra)


class Item:
    def __init__(self, operand, out_shape, src, dst, two_level=False):
        self.operand, self.out_shape, self.src, self.dst, self.two_level = operand, out_shape, src, dst, two_level


def _dev(p):
    return 4 * p[0] + 2 * p[1] + p[2]


def gather_item(shard):
    return Item(shard, jax.ShapeDtypeStruct((N_DEV,) + shard.shape, shard.dtype),
                lambda r, peer: r, lambda r, s: r.at[_dev(s)], two_level=True)


def gather_cols_item(shard):
    d, w = shard.shape
    return Item(shard, jax.ShapeDtypeStruct((N_DEV // 2, d, 2 * w), shard.dtype),
                lambda r, peer: r, lambda r, s: r.at[2 * s[0] + s[1], :, pl.ds(s[2] * w, w)], two_level=True)


def scatter_item(blocks):
    return Item(blocks, jax.ShapeDtypeStruct(blocks.shape, blocks.dtype),
                lambda r, peer: r.at[_dev(peer)], lambda r, s: r.at[_dev(s)])


def scatter_cols_item(full):
    n, d, w2 = full.shape
    w = w2 // 2
    return Item(full, jax.ShapeDtypeStruct((N_DEV, d, w), full.dtype),
                lambda r, peer: r.at[2 * peer[0] + peer[1], :, pl.ds(peer[2] * w, w)], lambda r, s: r.at[_dev(s)])


def _comm_sems(items):
    n = len(items) * (N_DEV - 1)
    return [pltpu.SemaphoreType.DMA((n,)), pltpu.SemaphoreType.DMA((n,)), pltpu.SemaphoreType.DMA((len(items),))]


PASSED_ON = (3, 5, 7)


def _comm_copies(items, in_refs, out_refs, sems, x, y, c):
    send_sems, recv_sems, local_sems = sems
    me = (x, y, c)
    sibling = (x, y, 1 - c)
    local, first, passed = [], [], []

    def remote(t, k, src, dst, to):
        n = t * (N_DEV - 1) + k - 1
        return lambda: pltpu.make_async_remote_copy(
            src_ref=src(), dst_ref=dst(), send_sem=send_sems.at[n], recv_sem=recv_sems.at[n],
            device_id=(to[0], to[1], jnp.int32(to[2])), device_id_type=MESH)

    for t, it in enumerate(items):
        local.append(lambda t=t, it=it: pltpu.make_async_copy(it.src(in_refs[t], me), it.dst(out_refs[t], me),
                                                               local_sems.at[t]))
        for k in range(1, N_DEV):
            peer = (1 - x if k & 4 else x, 1 - y if k & 2 else y, 1 - c if k & 1 else c)
            if it.two_level and k in PASSED_ON:
                origin = (peer[0], peer[1], c)
                landed = lambda t=t, it=it, origin=origin: it.dst(out_refs[t], origin)
                passed.append((t, k, remote(t, k, landed, landed, sibling)))
            else:
                first.append((t, k, remote(t, k, lambda t=t, it=it, peer=peer: it.src(in_refs[t], peer),
                                           lambda t=t, it=it: it.dst(out_refs[t], me), peer)))
    return local, first, passed


def _on_my_core(fn):
    x, y, c = lax.axis_index("x"), lax.axis_index("y"), lax.axis_index("c")
    for cv in (0, 1):
        @pl.when(c == cv)
        def _():
            fn(x, y, cv)


def comm_start(items, in_refs, out_refs, sems):
    def go(x, y, c):
        local, first, _ = _comm_copies(items, in_refs, out_refs, sems, x, y, c)
        for make in local:
            make().start()
        for _, _, make in first:
            make().start()

    _on_my_core(go)


def comm_pass_on(items, in_refs, out_refs, sems):
    def go(x, y, c):
        _, first, passed = _comm_copies(items, in_refs, out_refs, sems, x, y, c)
        arrived = {(t, k): make for t, k, make in first}
        for t, k, make in passed:
            arrived[(t, k - 1)]().wait_recv()
            make().start()

    _on_my_core(go)


def comm_wait(items, in_refs, out_refs, sems):
    def go(x, y, c):
        local, first, passed = _comm_copies(items, in_refs, out_refs, sems, x, y, c)
        waited = {(t, k - 1) for t, k, _ in passed}
        for t, k, make in first:
            cp = make()
            if (t, k) not in waited:
                cp.wait_recv()
            cp.wait_send()
        for _, _, make in passed:
            cp = make()
            cp.wait_recv()
            cp.wait_send()
        for make in local:
            make().wait()

    _on_my_core(go)


def comm_call(items, name):
    n = len(items)

    def body(*refs):
        in_refs, out_refs, sems = refs[:n], refs[n:2 * n], refs[2 * n:]
        comm_start(items, in_refs, out_refs, sems)
        comm_pass_on(items, in_refs, out_refs, sems)
        comm_wait(items, in_refs, out_refs, sems)

    return pl.pallas_call(
        body, in_specs=[ANY] * n, out_specs=[ANY] * n, out_shape=[it.out_shape for it in items],
        scratch_shapes=_comm_sems(items), name=name,
    )(*[it.operand for it in items])


def ffn_fwd(x, g, w1, w3, w2, tm, name, items=(), out_norm=None):
    seq, dm = x.shape
    nc, _, fc = w1.shape
    nt = seq // tm
    n = len(items)
    e = 0 if out_norm is None else 1

    def body(*refs):
        x_ref, g_ref, w1_ref, w3_ref, w2_ref = refs[:5]
        refs = refs[5:]
        gf_ref, refs = refs[:e], refs[e:]
        c_in, (h_ref, a_ref, b_ref, xn_s), refs = refs[:n], refs[n:n + 4], refs[n + 4:]
        xo_ref, refs = refs[:e], refs[e:]
        c_out, acc_s, sems = refs[:n], refs[n], refs[n + 1:]
        i, c = pl.program_id(0), pl.program_id(1)

        if n:
            @pl.when((i == 0) & (c == 0))
            def _():
                comm_start(items, c_in, c_out, sems)

        @pl.when(c == 0)
        def _():
            xn_s[...] = rms(x_ref[...], g_ref[...]).astype(BF)
            acc_s[...] = jnp.zeros_like(acc_s)

        xn = xn_s[...]
        a = _dot(xn, w1_ref[0], _NN)
        b = _dot(xn, w3_ref[0], _NN)
        a_ref[...] = a.astype(BF)
        b_ref[...] = b.astype(BF)
        hid = (a * jax.nn.sigmoid(a)) * b
        acc_s[...] += _dot(hid.astype(BF), w2_ref[0], _NN)

        @pl.when(c == nc - 1)
        def _():
            h = x_ref[...] + 0.5 * acc_s[...]
            h_ref[...] = h
            if e:
                xo_ref[0][...] = rms(h, gf_ref[0][...])

        if n:
            step = i * nc + c
            total = nt * nc

            @pl.when(step == max(1, (13 * total) // 16))
            def _():
                comm_pass_on(items, c_in, c_out, sems)

            @pl.when(step == total - 1)
            def _():
                comm_wait(items, c_in, c_out, sems)

    res = pl.pallas_call(
        body, grid=(nt, nc),
        in_specs=[pl.BlockSpec((tm, dm), lambda i, c: (i, 0)), pl.BlockSpec((1, dm), lambda i, c: (0, 0)),
                  pl.BlockSpec((1, dm, fc), lambda i, c: (c, 0, 0)), pl.BlockSpec((1, dm, fc), lambda i, c: (c, 0, 0)),
                  pl.BlockSpec((1, fc, dm), lambda i, c: (c, 0, 0))]
                 + [pl.BlockSpec((1, dm), lambda i, c: (0, 0))] * e + [ANY] * n,
        out_specs=[pl.BlockSpec((tm, dm), lambda i, c: (i, 0)), pl.BlockSpec((tm, fc), lambda i, c: (i, c)),
                   pl.BlockSpec((tm, fc), lambda i, c: (i, c)), pl.BlockSpec((tm, dm), lambda i, c: (i, 0))]
                  + [pl.BlockSpec((tm, dm), lambda i, c: (i, 0))] * e + [ANY] * n,
        out_shape=[jax.ShapeDtypeStruct((seq, dm), F32), jax.ShapeDtypeStruct((seq, nc * fc), BF),
                   jax.ShapeDtypeStruct((seq, nc * fc), BF), jax.ShapeDtypeStruct((seq, dm), BF)]
                  + [jax.ShapeDtypeStruct((seq, dm), F32)] * e + [it.out_shape for it in items],
        scratch_shapes=[pltpu.VMEM((tm, dm), F32)] + (_comm_sems(items) if n else []),
        compiler_params=_cparams(("arbitrary", "arbitrary")), name=name,
    )(x, g, w1, w3, w2, *([out_norm] * e), *[it.operand for it in items])
    return res[0], (res[1], res[2], res[3]), (res[4] if e else None), list(res[4 + e:])


def ffn_bwd_chunks(dh, ups, w1, w3, w2, tm, name, items=()):
    seq, dm = dh.shape
    nc, _, fc = w1.shape
    nt = seq // tm
    n = len(items)

    def body(*refs):
        dh_ref, a_ref, b_ref, xn_ref, w1_ref, w3_ref, w2_ref = refs[:7]
        c_in = refs[7:7 + n]
        dxn_ref, dw1_ref, dw3_ref, dw2_ref = refs[7 + n:11 + n]
        c_out = refs[11 + n:11 + 2 * n]
        a1_s, a3_s, a2_s = refs[11 + 2 * n:14 + 2 * n]
        sems = refs[14 + 2 * n:]
        c, i = pl.program_id(0), pl.program_id(1)

        if n:
            @pl.when((i == 0) & (c == 0))
            def _():
                comm_start(items, c_in, c_out, sems)

        @pl.when(i == 0)
        def _():
            a1_s[...] = jnp.zeros_like(a1_s)
            a3_s[...] = jnp.zeros_like(a3_s)
            a2_s[...] = jnp.zeros_like(a2_s)

        xn = xn_ref[...]
        dy = (0.5 * dh_ref[...]).astype(BF)
        w1v, w3v, w2v = w1_ref[0], w3_ref[0], w2_ref[0]
        a = a_ref[...].astype(F32)
        b = b_ref[...].astype(F32)
        sig = jax.nn.sigmoid(a)
        sa = a * sig
        dhid = _dot(dy, w2v, _NT)
        db = (dhid * sa).astype(BF)
        da = (dhid * b * (sig * (1.0 + a * (1.0 - sig)))).astype(BF)
        dxn_ref[0] = (_dot(da, w1v, _NT) + _dot(db, w3v, _NT)).astype(dxn_ref.dtype)
        a1_s[...] += _dot(xn, da, _TN)
        a3_s[...] += _dot(xn, db, _TN)
        a2_s[...] += _dot((sa * b).astype(BF), dy, _TN)

        @pl.when(i == nt - 1)
        def _():
            dw1_ref[0] = a1_s[...].astype(BF)
            dw3_ref[0] = a3_s[...].astype(BF)
            dw2_ref[0] = a2_s[...].astype(BF)

        if n:
            @pl.when((i == nt - 1) & (c == nc - 1))
            def _():
                comm_wait(items, c_in, c_out, sems)

    res = pl.pallas_call(
        body, grid=(nc, nt),
        in_specs=[pl.BlockSpec((tm, dm), lambda c, i: (i, 0)),
                  pl.BlockSpec((tm, fc), lambda c, i: (i, c)), pl.BlockSpec((tm, fc), lambda c, i: (i, c)),
                  pl.BlockSpec((tm, dm), lambda c, i: (i, 0)),
                  pl.BlockSpec((1, dm, fc), lambda c, i: (c, 0, 0)), pl.BlockSpec((1, dm, fc), lambda c, i: (c, 0, 0)),
                  pl.BlockSpec((1, fc, dm), lambda c, i: (c, 0, 0))] + [ANY] * n,
        out_specs=[pl.BlockSpec((1, tm, dm), lambda c, i: (c, i, 0)),
                   pl.BlockSpec((1, dm, fc), lambda c, i: (c, 0, 0)), pl.BlockSpec((1, dm, fc), lambda c, i: (c, 0, 0)),
                   pl.BlockSpec((1, fc, dm), lambda c, i: (c, 0, 0))] + [ANY] * n,
        out_shape=[jax.ShapeDtypeStruct((nc, seq, dm), BF), jax.ShapeDtypeStruct((nc, dm, fc), BF),
                   jax.ShapeDtypeStruct((nc, dm, fc), BF), jax.ShapeDtypeStruct((nc, fc, dm), BF)]
                  + [it.out_shape for it in items],
        scratch_shapes=[pltpu.VMEM((dm, fc), F32), pltpu.VMEM((dm, fc), F32), pltpu.VMEM((fc, dm), F32)]
                       + (_comm_sems(items) if n else []),
        compiler_params=_cparams(("arbitrary", "arbitrary")), name=name,
    )(dh, ups[0], ups[1], ups[2], w1, w3, w2, *[it.operand for it in items])
    return res[0], res[1], res[2], res[3], list(res[4:])


def _accumulate(ref, val, i):
    @pl.when(i == 0)
    def _():
        ref[...] = val

    @pl.when(i > 0)
    def _():
        ref[...] += val


def _through_norm_below(dx, below_refs):
    h3_ref, gf_ref = below_refs
    _, vjp = jax.vjp(rms, h3_ref[...], gf_ref[...])
    return vjp(dx)


def ffn_bwd_norm(x, g, parts, dh, tm, name, below=None, items=()):
    seq, dm = x.shape
    nc = parts.shape[0]
    nt = seq // tm
    nb = 0 if below is None else 2
    n = len(items)
    n_out = 2 + nb // 2

    def body(*refs):
        x_ref, g_ref, p_ref, dh_ref = refs[:4]
        c_in = refs[4 + nb:4 + nb + n]
        outs = refs[4 + nb + n:]
        dx_ref, dg_ref = outs[:2]
        c_out, sems = outs[n_out:n_out + n], outs[n_out + n:]
        i = pl.program_id(0)

        if n:
            @pl.when(i == 0)
            def _():
                comm_start(items, c_in, c_out, sems)

        dxn = p_ref[0].astype(F32)
        for c in range(1, nc):
            dxn = dxn + p_ref[c].astype(F32)
        _, vjp = jax.vjp(rms, x_ref[...], g_ref[...])
        dx, dg = vjp(dxn)
        dx = dx + dh_ref[...]
        if nb:
            dx, dgf = _through_norm_below(dx, refs[4:6])
            _accumulate(outs[2], dgf, i)
        dx_ref[...] = dx
        _accumulate(dg_ref, dg, i)

        if n:
            @pl.when(i == nt - 1)
            def _():
                comm_wait(items, c_in, c_out, sems)

    tile = pl.BlockSpec((tm, dm), lambda i: (i, 0))
    row = pl.BlockSpec((1, dm), lambda i: (0, 0))
    res = pl.pallas_call(
        body, grid=(nt,),
        in_specs=[tile, row, pl.BlockSpec((nc, tm, dm), lambda i: (0, i, 0)), tile] + [tile, row][:nb] + [ANY] * n,
        out_specs=[tile, row] + [row] * (nb // 2) + [ANY] * n,
        out_shape=[jax.ShapeDtypeStruct((seq, dm), F32)] + [jax.ShapeDtypeStruct((1, dm), F32)] * (1 + nb // 2)
                  + [it.out_shape for it in items],
        scratch_shapes=_comm_sems(items) if n else [],
        compiler_params=_cparams(("arbitrary",)), name=name,
    )(x, g, parts, dh, *(below or ()), *[it.operand for it in items])
    return (*res[:n_out], list(res[n_out:]))


def rope_tables(pos_col, inv_freq, tm):
    seq = pos_col.shape[0]

    def body(p_ref, f_ref, c_ref, s_ref):
        ang = p_ref[...].astype(F32) * f_ref[...]
        lane = lax.broadcasted_iota(jnp.int32, ang.shape, 1)
        c_ref[...] = jnp.cos(ang)
        s_ref[...] = jnp.where(lane < HEAD // 2, -jnp.sin(ang), jnp.sin(ang))

    return pl.pallas_call(
        body, grid=(seq // tm,),
        in_specs=[pl.BlockSpec((tm, 1), lambda i: (i, 0)), pl.BlockSpec((1, HEAD), lambda i: (0, 0))],
        out_specs=[pl.BlockSpec((tm, HEAD), lambda i: (i, 0))] * 2,
        out_shape=[jax.ShapeDtypeStruct((seq, HEAD), F32)] * 2,
        compiler_params=_cparams(("arbitrary",)), name="rope_tables",
    )(pos_col, inv_freq)


def loss_and_grad(h3, gf, target, tm):
    seq, dm = h3.shape

    def body(h_ref, g_ref, t_ref, l_ref, dh_ref, dg_ref):
        i = pl.program_id(0)
        y, vjp = jax.vjp(rms, h_ref[...], g_ref[...])
        err = y - t_ref[...]
        dh, dg = vjp(err * (1.0 / dm))
        dh_ref[...] = dh
        _accumulate(dg_ref, dg, i)
        part = 0.5 * jnp.sum(jnp.mean(err * err, axis=-1, keepdims=True), axis=0, keepdims=True)
        _accumulate(l_ref, jnp.broadcast_to(part, (1, LANE)), i)

    tile = pl.BlockSpec((tm, dm), lambda i: (i, 0))
    row = pl.BlockSpec((1, dm), lambda i: (0, 0))
    return pl.pallas_call(
        body, grid=(seq // tm,),
        in_specs=[tile, row, tile],
        out_specs=[pl.BlockSpec((1, LANE), lambda i: (0, 0)), tile, row],
        out_shape=[jax.ShapeDtypeStruct((1, LANE), F32), jax.ShapeDtypeStruct((seq, dm), F32),
                   jax.ShapeDtypeStruct((1, dm), F32)],
        compiler_params=_cparams(("arbitrary",)), name="loss_and_grad",
    )(h3, gf, target)


def adamw(w, g, m, v, rows, name):
    r, c = w.shape

    def body(w_ref, g_ref, m_ref, v_ref, d_ref, nm_ref, nv_ref):
        gv = g_ref[...]
        nm = ADAM_B1 * m_ref[...] + (1.0 - ADAM_B1) * gv
        nv = ADAM_B2 * v_ref[...] + (1.0 - ADAM_B2) * (gv * gv)
        m_hat = nm / (1.0 - ADAM_B1 ** ADAM_STEP)
        v_hat = nv / (1.0 - ADAM_B2 ** ADAM_STEP)
        d_ref[...] = -ADAM_LR * (m_hat / (jnp.sqrt(v_hat) + ADAM_EPS) + ADAM_WD * w_ref[...])
        nm_ref[...] = nm
        nv_ref[...] = nv

    spec = pl.BlockSpec((rows, c), lambda i: (i, 0))
    return pl.pallas_call(
        body, grid=(r // rows,), in_specs=[spec] * 4, out_specs=[spec] * 3,
        out_shape=[jax.ShapeDtypeStruct((r, c), F32)] * 3,
        compiler_params=_cparams(("arbitrary",)), name=name,
    )(w, g, m, v)


def sum_layers(recvs, out_rows, out_cols, rows, name):
    depth = len(recvs)
    n, _, c_in = recvs[0].shape

    def body(*refs):
        o_ref = refs[depth]
        for l in range(depth):
            @pl.when(pl.program_id(0) == l)
            def _():
                acc = refs[l][0].astype(F32)
                for j in range(1, n):
                    acc = acc + refs[l][j].astype(F32)
                o_ref[0] = acc[:, :out_cols]

    return pl.pallas_call(
        body, grid=(depth, out_rows // rows),
        in_specs=[pl.BlockSpec((n, rows, c_in), lambda ll, i, l=l: (0, jnp.where(ll == l, i, 0), 0))
                  for l in range(depth)],
        out_specs=pl.BlockSpec((1, rows, out_cols), lambda ll, i: (ll, i, 0)),
        out_shape=jax.ShapeDtypeStruct((depth, out_rows, out_cols), F32),
        compiler_params=_cparams(("arbitrary", "arbitrary")), name=name,
    )(*recvs)


def _heads(t, n):
    return t.reshape(t.shape[0], n, HEAD).transpose(1, 0, 2)


def _unheads(t):
    return t.transpose(1, 0, 2).reshape(t.shape[1], t.shape[0] * HEAD)


def _mkv_heads(mkv):
    mw = N_MEMH * HEAD
    return _heads(mkv[:, :mw], N_MEMH), _heads(mkv[:, mw:], N_MEMH)


def _mix_args(sv, tabs, sw):
    mk, mv = _mkv_heads(sv["mkv"])
    ct, st = tabs
    conv_args = [(sv["a"], HALO, True), (sv["gate"], HALO, True), (sw["conv_w"], PARAM, True),
                 (sw["conv_b"], PARAM, True), (sw["conv_ln_g"], PARAM, True), (sw["conv_ln_b"], PARAM, True)]
    swa_args = [(sv["q"], TILE, True), (sv["k"], HALO, True), (sv["v"], HALO, True), (ct, HALO, False),
                (st, HALO, False), (sw["swa_q_norm"], PARAM, True), (sw["swa_k_norm"], PARAM, True),
                (sw["swa_sinks"], PARAM, True)]
    mem_args = [(sv["qm"], TILE, True), (mk, PARAM, True), (mv, PARAM, True), (sw["mem_q_norm"], PARAM, True),
                (sw["mem_k_norm"], PARAM, True)]
    return conv_args, swa_args, mem_args


def _proj_args(h1, sw, bw):
    return [(h1, TILE, True), (sw["mix_norm"], PARAM, True), (bw["w_in"], PARAM, True)]


def _join_args(sv, bw):
    return [(sv["h1"], TILE, True), (sv["y_conv"], TILE, True), (sv["y_swa"], TILE, True), (sv["y_mem"], TILE, True),
            (bw["w_out"], PARAM, True)]


def layer_fwd(x, mem, tabs, bw, sent, sw, tm, l, depth):
    seq, dm = x.shape
    tag = f"_l{l}"
    sv = dict(x=x)
    sv["h1"], sv["ups1"], _, got1 = ffn_fwd(x, sw["ffn1_norm"], bw["ffn1_w1"], bw["ffn1_w3"], bw["ffn1_w2"], tm,
                                            "ffn1_fwd" + tag, gather_items(sent, l, GATHER_LATE))
    bw = {**bw, **kernel_layouts(got1, GATHER_LATE)}
    items2 = gather_items(sent, l + 1, GATHER_EARLY) if l + 1 < depth else ()
    split_outs = [((seq, CONV_CH), F32), ((seq, CONV_CH), F32), ((N_Q, seq, HEAD), F32), ((N_KV, seq, HEAD), F32),
                  ((N_KV, seq, HEAD), F32), ((N_MEMH, seq, HEAD), F32)]
    sv["a"], sv["gate"], sv["q"], sv["k"], sv["v"], sv["qm"] = seq_fwd(
        f_proj_split, _proj_args(sv["h1"], sw, bw), split_outs, tm, tm, "proj_in_fwd" + tag)
    ml = mem.shape[0]
    (sv["mkv"],) = seq_fwd(f_proj_in, [(mem, TILE, False), (sw["mem_norm"], PARAM, True), (bw["w_mem_kv"], PARAM, True)],
                           [((ml, bw["w_mem_kv"].shape[1]), F32)], ml, ml, "mem_kv_fwd" + tag)
    conv_args, swa_args, mem_args = _mix_args(sv, tabs, sw)
    (sv["y_conv"],) = seq_fwd(f_conv, conv_args, [((seq, CONV_CH), F32)], tm, 32, "conv_fwd" + tag)
    (sv["y_swa"],) = seq_fwd(f_swa, swa_args, [((N_Q, seq, HEAD), F32)], tm, BLOCK, "swa_fwd" + tag)
    (sv["y_mem"],) = seq_fwd(f_mem, mem_args, [((N_MEMH, seq, HEAD), F32)], tm, tm, "mem_attn_fwd" + tag)
    (sv["h2"],) = seq_fwd(f_out_join, _join_args(sv, bw), [((seq, dm), F32)], tm, tm, "out_proj_fwd" + tag)
    sv["h3"], sv["ups2"], xo, got2 = ffn_fwd(sv["h2"], sw["ffn2_norm"], bw["ffn2_w1"], bw["ffn2_w3"], bw["ffn2_w2"],
                                             tm, "ffn2_fwd" + tag, items2,
                                             out_norm=sw["final_norm"] if l + 1 < depth else None)
    return xo, sv, bw, kernel_layouts(got2, GATHER_EARLY)


def layer_bwd(dh3, sv, mem, tabs, bw, sw, tm, l, items=(), below=None):
    tag = f"_l{l}"
    gb, gs = {}, {}
    parts2, gb["ffn2_w1"], gb["ffn2_w3"], gb["ffn2_w2"], got = ffn_bwd_chunks(
        dh3, sv["ups2"], bw["ffn2_w1"], bw["ffn2_w3"], bw["ffn2_w2"], tm, "ffn2_bwd" + tag, items)
    dh2, gs["ffn2_norm"], _ = ffn_bwd_norm(sv["h2"], sw["ffn2_norm"], parts2, dh3, tm, "ffn2_norm_bwd" + tag)
    dh1_a, dy_conv, dy_swa, dy_mem, gb["w_out"] = seq_bwd(f_out_join, _join_args(sv, bw), [dh2], tm, tm,
                                                           "out_proj_bwd" + tag)
    conv_args, swa_args, mem_args = _mix_args(sv, tabs, sw)
    da, dgate, gs["conv_w"], gs["conv_b"], gs["conv_ln_g"], gs["conv_ln_b"] = seq_bwd(
        f_conv, conv_args, [dy_conv], tm, 32, "conv_bwd" + tag)
    dq, dk, dv, gs["swa_q_norm"], gs["swa_k_norm"], gs["swa_sinks"] = seq_bwd(
        f_swa, swa_args, [dy_swa], tm, BLOCK, "swa_bwd" + tag)
    dqm, dmk, dmv, gs["mem_q_norm"], gs["mem_k_norm"] = seq_bwd(f_mem, mem_args, [dy_mem], tm, tm, "mem_attn_bwd" + tag)
    dmkv = jnp.concatenate([_unheads(dmk), _unheads(dmv)], axis=-1)
    ml = mem.shape[0]
    gs["mem_norm"], gb["w_mem_kv"] = seq_bwd(
        f_proj_in, [(mem, TILE, False), (sw["mem_norm"], PARAM, True), (bw["w_mem_kv"], PARAM, True)], [dmkv], ml, ml,
        "mem_kv_bwd" + tag)
    dh1, gs["mix_norm"], gb["w_in"] = seq_bwd(f_proj_split, _proj_args(sv["h1"], sw, bw),
                                               [da, dgate, dq, dk, dv, dqm], tm // 2, tm // 2, "proj_in_bwd" + tag,
                                               add_to_first=dh1_a)
    parts1, gb["ffn1_w1"], gb["ffn1_w3"], gb["ffn1_w2"], got_late = ffn_bwd_chunks(
        dh1, sv["ups1"], bw["ffn1_w1"], bw["ffn1_w3"], bw["ffn1_w2"], tm, "ffn1_bwd" + tag,
        scatter_items(gb, GATHER_LATE))
    tail = () if below else scatter_items(gb, GATHER_EARLY)
    dx, gs["ffn1_norm"], *g_below, got_tail = ffn_bwd_norm(sv["x"], sw["ffn1_norm"], parts1, dh1, tm,
                                                           "ffn1_norm_bwd" + tag, below, tail)
    return dx, gb, gs, got, got_late, (g_below[0] if g_below else None), got_tail


FFN_UP = ("ffn1_w1", "ffn1_w3", "ffn2_w1", "ffn2_w3")
FFN_DOWN = ("ffn1_w2", "ffn2_w2")
ROW_SHARDED = ("w_mem_kv", "w_out")
GATHER_EARLY = ("ffn1_w1", "ffn1_w3", "ffn1_w2")
GATHER_LATE = ("w_in", "w_mem_kv", "w_out", "ffn2_w1", "ffn2_w3", "ffn2_w2")
BIG = GATHER_EARLY + GATHER_LATE
SMALL = ("ffn1_norm", "mix_norm", "conv_b", "conv_ln_g", "conv_ln_b", "swa_q_norm", "swa_k_norm", "swa_sinks",
         "mem_norm", "mem_q_norm", "mem_k_norm", "ffn2_norm", "final_norm")
WEIGHTS = ("ffn1_norm", "ffn1_w1", "ffn1_w3", "ffn1_w2", "mix_norm", "w_in", "conv_w", "conv_b", "conv_ln_g",
           "conv_ln_b", "swa_q_norm", "swa_k_norm", "swa_sinks", "mem_norm", "w_mem_kv", "mem_q_norm", "mem_k_norm",
           "w_out", "ffn2_norm", "ffn2_w1", "ffn2_w3", "ffn2_w2", "final_norm")


def _round_up(n, m):
    return -(-n // m) * m


def _row_block(rows, target, mult=8):
    best = rows
    for cand in range(mult, min(rows, target) + 1, mult):
        if rows % cand == 0:
            best = cand
    return best


def _pad_rows(flat, cols, mult):
    n = flat.shape[0]
    rows = _round_up(-(-n // cols), mult)
    return jnp.pad(flat, (0, rows * cols - n)).reshape(rows, cols)


def send_form(w):
    out = {}
    for n in BIG:
        t = w[n].astype(BF)
        if n in FFN_UP:
            t = jnp.pad(t, ((0, 0), (0, 0), (0, _round_up(t.shape[2], LANE) - t.shape[2])))
        elif n in FFN_DOWN:
            t = jnp.pad(t, ((0, 0), (0, _round_up(t.shape[1], LANE) - t.shape[1]), (0, 0)))
        out[n] = t
    return out


def gather_items(sent, l, names):
    return [gather_cols_item(sent[n][l]) if n in FFN_UP else gather_item(sent[n][l]) for n in names]


def kernel_layouts(got, names):
    out = {}
    for n, t in zip(names, got):
        if n in FFN_DOWN:
            t = t.reshape(N_DEV // 2, 2 * t.shape[1], t.shape[2])
        elif n == "w_in":
            t = t.transpose(1, 0, 2).reshape(t.shape[1], N_DEV * t.shape[2])
        elif n in ROW_SHARDED:
            t = t.reshape(N_DEV * t.shape[1], t.shape[2])
        out[n] = t
    return out


def scatter_items(gb, names):
    items = []
    for n in names:
        t = gb[n]
        if n in FFN_UP:
            items.append(scatter_cols_item(t))
            continue
        if n in FFN_DOWN:
            t = t.reshape(N_DEV, t.shape[1] // 2, t.shape[2])
        elif n == "w_in":
            t = t.reshape(t.shape[0], N_DEV, t.shape[1] // N_DEV).transpose(1, 0, 2).astype(BF)
        else:
            t = t.reshape(N_DEV, t.shape[0] // N_DEV, t.shape[1]).astype(BF)
        items.append(scatter_item(t))
    return items


def small_layer_params(w, l):
    sw = {n: w[n][l][None, :] for n in SMALL if n != "swa_sinks"}
    sw["swa_sinks"] = jnp.pad(w["swa_sinks"][l], (0, LANE - N_Q))[None, :]
    sw["conv_w"] = jnp.pad(w["conv_w_full"][l], ((0, 1), (0, 0)))
    return sw


def kernel(x, mem, positions, ffn1_norm, ffn1_w1, ffn1_w3, ffn1_w2, mix_norm, w_in, conv_w, conv_b, conv_ln_g, conv_ln_b, swa_q_norm, swa_k_norm, swa_sinks, mem_norm, w_mem_kv, mem_q_norm, mem_k_norm, w_out, ffn2_norm, ffn2_w1, ffn2_w3, ffn2_w2, final_norm, loss_target, m_ffn1_norm, m_ffn1_w1, m_ffn1_w3, m_ffn1_w2, m_mix_norm, m_w_in, m_conv_w, m_conv_b, m_conv_ln_g, m_conv_ln_b, m_swa_q_norm, m_swa_k_norm, m_swa_sinks, m_mem_norm, m_w_mem_kv, m_mem_q_norm, m_mem_k_norm, m_w_out, m_ffn2_norm, m_ffn2_w1, m_ffn2_w3, m_ffn2_w2, m_final_norm, v_ffn1_norm, v_ffn1_w1, v_ffn1_w3, v_ffn1_w2, v_mix_norm, v_w_in, v_conv_w, v_conv_b, v_conv_ln_g, v_conv_ln_b, v_swa_q_norm, v_swa_k_norm, v_swa_sinks, v_mem_norm, v_w_mem_kv, v_mem_q_norm, v_mem_k_norm, v_w_out, v_ffn2_norm, v_ffn2_w1, v_ffn2_w3, v_ffn2_w2, v_final_norm):
    loc = locals()
    w = {n: loc[n] for n in WEIGHTS}
    m = {n: loc["m_" + n] for n in WEIGHTS}
    v = {n: loc["v_" + n] for n in WEIGHTS}
    depth = ffn1_norm.shape[0]
    seq = x.shape[1]
    tm = min(512, seq)
    me = 4 * lax.axis_index("x") + 2 * lax.axis_index("y") + lax.axis_index("c")
    xs, mems, target = x[0], mem[0], loss_target[0]

    sent = send_form(w)
    cw = conv_w.shape[2]
    conv_rows = _pad_rows(conv_w.reshape(-1), LANE, 8)
    got = comm_call(gather_items(sent, 0, GATHER_EARLY) + [gather_item(conv_rows)], "gather_l0")
    bw = kernel_layouts(got[:-1], GATHER_EARLY)
    conv_full = got[-1].reshape(N_DEV, -1)[:, :conv_w.size].reshape((N_DEV,) + conv_w.shape)
    small = {n: w[n] for n in SMALL}
    small["conv_w_full"] = conv_full.transpose(1, 2, 0, 3).reshape(depth, CONV_W, N_DEV * cw)

    inv = ROPE_THETA ** (-jnp.arange(0, HEAD, 2, dtype=F32) / HEAD)
    tabs = rope_tables(positions[0].reshape(-1, 1), jnp.concatenate([inv, inv])[None, :], tm)
    saved, bws = [], []
    h = xs
    for l in range(depth):
        h, sv, bw_all, bw = layer_fwd(h, mems, tabs, bw, sent, small_layer_params(small, l), tm, l, depth)
        saved.append(sv)
        bws.append(bw_all)
    top = small_layer_params(small, depth - 1)["final_norm"]
    loss_sum, dh, g_final = loss_and_grad(saved[-1]["h3"], top, target, tm)
    loss = lax.psum(loss_sum[0, 0], ("x", "y", "c"))

    recv = [dict() for _ in range(depth)]
    gss = [None] * depth
    items = ()
    for l in reversed(range(depth)):
        below = (saved[l - 1]["h3"], small_layer_params(small, l - 1)["final_norm"]) if l else None
        dh, gb, gss[l], got, got_late, g_below, got_tail = layer_bwd(dh, saved[l], mems, tabs, bws[l],
                                                                     small_layer_params(small, l), tm, l, items, below)
        gss[l]["final_norm"] = g_final
        g_final = g_below
        if items:
            recv[l + 1].update(zip(GATHER_EARLY, got))
        recv[l].update(zip(GATHER_LATE, got_late))
        items = scatter_items(gb, GATHER_EARLY)
    recv[0].update(zip(GATHER_EARLY, got_tail))

    g_all = {}
    for n in BIG:
        a, b = w[n].shape[1:]
        blocks = [recv[l][n] for l in range(depth)]
        if n in FFN_DOWN:
            rows = _row_block(a, 256, BF16_ROWS)
        else:
            rows = _row_block(a, 256)
        g_all[n] = sum_layers(blocks, a, b, rows, "grad_sum_" + n)

    small_names = SMALL + ("conv_w",)
    sizes = {n: (w[n].shape[1] if n != "conv_w" else CONV_W * N_DEV * cw) for n in small_names}
    flat = []
    for l in range(depth):
        for n in small_names:
            t = gss[l][n]
            if n == "swa_sinks":
                t = t[:, :N_Q]
            elif n == "conv_w":
                t = t[:CONV_W]
            flat.append(t.reshape(-1))
    flat = _pad_rows(jnp.concatenate(flat), LANE, 8)
    (small_got,) = comm_call([gather_item(flat)], "small_grad_gather")
    summed = sum_layers([small_got], flat.shape[0], LANE, flat.shape[0], "small_grad_sum").reshape(-1)
    g_small = {n: [] for n in small_names}
    o = 0
    for l in range(depth):
        for n in small_names:
            g_small[n].append(summed[o:o + sizes[n]])
            o += sizes[n]
    for n in SMALL:
        g_all[n] = jnp.stack(g_small[n])
    gcw = jnp.stack(g_small["conv_w"]).reshape(depth, CONV_W, N_DEV, cw)
    g_all["conv_w"] = lax.dynamic_slice(gcw, (0, 0, me, 0), (depth, CONV_W, 1, cw)).reshape(depth, CONV_W, cw)

    outs = {"delta": {}, "new_m": {}, "new_v": {}}
    for n in BIG:
        shape = w[n].shape
        two_d = (shape[0] * shape[1], shape[2])
        res = adamw(w[n].reshape(two_d), g_all[n].reshape(two_d), m[n].reshape(two_d), v[n].reshape(two_d),
                    _row_block(two_d[0], 512), "adamw_" + n)
        for key, arr in zip(("delta", "new_m", "new_v"), res):
            outs[key][n] = arr.reshape(shape)

    def packed_small(t):
        return _pad_rows(jnp.concatenate([t[n].reshape(-1) for n in small_names]), LANE, 8)

    ps = [packed_small(t) for t in (w, g_all, m, v)]
    res = adamw(*ps, ps[0].shape[0], "adamw_small")
    for key, arr in zip(("delta", "new_m", "new_v"), res):
        fl, o = arr.reshape(-1), 0
        for n in small_names:
            outs[key][n] = fl[o:o + w[n].size].reshape(w[n].shape)
            o += w[n].size

    return (loss, dh[None], *[g_all[n] for n in WEIGHTS], *[outs["delta"][n] for n in WEIGHTS],
            *[outs["new_m"][n] for n in WEIGHTS], *[outs["new_v"][n] for n in WEIGHTS])
```
